```python
import jax, jax.numpy as jnp
from jax import lax
import numpy as np

D_MODEL = 4096
BATCH = 32
SEQ = 256
DEPTH = 2
DEC_BATCH = 8
DEC_SEQ = 1024
PAST_LEN = 256

GRID_W = 64
BRANCH_W = D_MODEL // 4
N_BRANCH = 4
HEAD_DIM = 128
N_HEADS = BRANCH_W // HEAD_DIM
N_KV = N_HEADS // 4
N_GROUP = N_HEADS // N_KV
Q_W = N_HEADS * HEAD_DIM
KV_W = N_KV * HEAD_DIM
Q_BLOCK = 128
WINDOW = 128
ROPE_THETA = 10000.0
ATTN_SCALE = HEAD_DIM ** -0.5
NEG_INF = -1e30
POOL_WINDOWS = (2, 4, 8, 16)
POOL_W = BRANCH_W
POOL_GW = POOL_W // len(POOL_WINDOWS)
LRU_W = BRANCH_W
LRU_BLOCKS = N_HEADS
LRU_BW = LRU_W // LRU_BLOCKS
LRU_C = 8.0
LRU_CONV = 4
LRU_CONV_LEFT = 2
D_FF = 11008
FFN_CONV = 3
FFN_CONV_LEFT = 1
ALPHA = (2 * DEPTH) ** 0.25
BETA = (8 * DEPTH) ** -0.25
IN_SIZES = (POOL_W, Q_W, KV_W, KV_W, Q_W, KV_W, KV_W, LRU_W, LRU_W, N_BRANCH * D_MODEL)
IN_W = sum(IN_SIZES)
IN_SPLITS = tuple(sum(IN_SIZES[:i + 1]) for i in range(len(IN_SIZES) - 1))

kernel_name = 'hybrid_dit_prefix_context_step'


def layer_norm(x, g=None, b=None, eps=1e-6):
    xf = x.astype(jnp.float32)
    mu = jnp.mean(xf, axis=-1, keepdims=True)
    var = jnp.mean(jnp.square(xf - mu), axis=-1, keepdims=True)
    y = (xf - mu) * lax.rsqrt(var + eps)
    if g is not None:
        y = y * g.astype(jnp.float32) + b.astype(jnp.float32)
    return y.astype(x.dtype)


def rms_norm(x, g, eps=1e-6):
    xf = x.astype(jnp.float32)
    y = xf * lax.rsqrt(jnp.mean(xf * xf, axis=-1, keepdims=True) + eps)
    return (y * g.astype(jnp.float32)).astype(x.dtype)


def modulate(h, shift, scale):
    return h * (1.0 + scale) + shift


def axial_rope(x):
    L = x.shape[1]
    t = jnp.arange(L)
    row = (t // GRID_W).astype(jnp.float32)
    col = (t % GRID_W).astype(jnp.float32)
    half = HEAD_DIM // 2
    inv = ROPE_THETA ** (-jnp.arange(0, half, 2, dtype=jnp.float32) / half)

    def rot(xa, pos):
        ang = pos[:, None] * inv[None, :]
        cos = jnp.cos(ang)[None, :, None, :]
        sin = jnp.sin(ang)[None, :, None, :]
        x1, x2 = jnp.split(xa.astype(jnp.float32), 2, axis=-1)
        return jnp.concatenate([x1 * cos - x2 * sin, x2 * cos + x1 * sin], axis=-1)

    xr, xc = jnp.split(x, 2, axis=-1)
    return jnp.concatenate([rot(xr, row), rot(xc, col)], axis=-1).astype(x.dtype)


def dwconv(x, w, b, left):
    K = w.shape[0]
    L = x.shape[1]
    xp = jnp.pad(x, ((0, 0), (left, K - 1 - left), (0, 0)))
    y = b
    for k in range(K):
        y = y + xp[:, k:k + L] * w[k]
    return y


def pool_mixer(u, w_pool, scale):
    B, L, _ = u.shape
    uf = u.astype(jnp.float32)
    cs = jnp.concatenate([jnp.zeros((B, 1, POOL_W), jnp.float32), jnp.cumsum(uf, axis=1)], axis=1)
    t = jnp.arange(L)
    outs = []
    for gi, w in enumerate(POOL_WINDOWS):
        lo = jnp.clip(t - w // 2, 0, L)
        hi = jnp.clip(t + w // 2, 0, L)
        csg = cs[..., gi * POOL_GW:(gi + 1) * POOL_GW]
        cnt = (hi - lo).astype(jnp.float32)[None, :, None]
        mean = (jnp.take(csg, hi, axis=1) - jnp.take(csg, lo, axis=1)) / cnt
        outs.append(mean - uf[..., gi * POOL_GW:(gi + 1) * POOL_GW])
    d = jnp.stack(outs, axis=2)
    y = jnp.einsum('blgc,gce->blge', d, w_pool.astype(jnp.float32)).reshape(B, L, POOL_W)
    return (y * scale.astype(jnp.float32)).astype(u.dtype)


def block_attention(q, k, v, sink=None):
    B, Lq = q.shape[0], q.shape[1]
    nb = Lq // Q_BLOCK
    qb = jnp.moveaxis(q.reshape(B, nb, Q_BLOCK, N_KV, N_GROUP, HEAD_DIM), 1, 0)

    def one_block(qi):
        s = jnp.einsum('bqkgd,bskd->bkgqs', qi, k, preferred_element_type=jnp.float32) * ATTN_SCALE
        if sink is not None:
            sk = jnp.broadcast_to(sink.astype(jnp.float32).reshape(1, N_KV, N_GROUP, 1, 1), s.shape[:-1] + (1,))
            s = jnp.concatenate([s, sk], axis=-1)
        p = jax.nn.softmax(s, axis=-1)
        if sink is not None:
            p = p[..., :-1]
        return jnp.einsum('bkgqs,bskd->bqkgd', p.astype(v.dtype), v)

    o = lax.map(one_block, qb)
    return jnp.moveaxis(o, 0, 1).reshape(B, Lq, N_HEADS * HEAD_DIM)


def window_attention(q, k, v, kc, vc, sink):
    B, L = q.shape[0], q.shape[1]
    Lc = kc.shape[1]
    nb = L // Q_BLOCK
    qb = q.reshape(B, nb, Q_BLOCK, N_KV, N_GROUP, HEAD_DIM)

    def band(a):
        ap = jnp.pad(a, ((0, 0), (Q_BLOCK, Q_BLOCK), (0, 0), (0, 0))).reshape(B, nb + 2, Q_BLOCK, N_KV, HEAD_DIM)
        return jnp.concatenate([ap[:, :-2], ap[:, 1:-1], ap[:, 2:]], axis=2)

    kb, vb = band(k), band(v)
    tq = jnp.arange(L).reshape(nb, Q_BLOCK)
    tk = (jnp.arange(nb)[:, None] - 1) * Q_BLOCK + jnp.arange(3 * Q_BLOCK)[None, :]
    valid = (jnp.abs(tq[:, :, None] - tk[:, None, :]) <= WINDOW) & (tk[:, None, :] >= 0) & (tk[:, None, :] < L)
    s_loc = jnp.einsum('bnqkgd,bnskd->bnkgqs', qb, kb, preferred_element_type=jnp.float32) * ATTN_SCALE
    s_loc = jnp.where(valid[None, :, None, None], s_loc, NEG_INF)
    s_ctx = jnp.einsum('bnqkgd,bckd->bnkgqc', qb, kc, preferred_element_type=jnp.float32) * ATTN_SCALE
    sk = jnp.broadcast_to(sink.astype(jnp.float32).reshape(1, 1, N_KV, N_GROUP, 1, 1), s_loc.shape[:-1] + (1,))
    p = jax.nn.softmax(jnp.concatenate([s_loc, s_ctx, sk], axis=-1), axis=-1)
    n_loc = 3 * Q_BLOCK
    p_loc = p[..., :n_loc].astype(v.dtype)
    p_ctx = p[..., n_loc:n_loc + Lc].astype(v.dtype)
    o = jnp.einsum('bnkgqs,bnskd->bnqkgd', p_loc, vb) + jnp.einsum('bnkgqc,bckd->bnqkgd', p_ctx, vc)
    return o.reshape(B, L, N_HEADS * HEAD_DIM)


def block_diag(x, w, b):
    B, L, _ = x.shape
    y = jnp.einsum('blnc,nce->blne', x.reshape(B, L, LRU_BLOCKS, LRU_BW), w).reshape(B, L, LRU_W)
    return y + b


def _scan_combine(e1, e2):
    a1, b1 = e1
    a2, b2 = e2
    return a1 * a2, a2 * b1 + b2


def rglru_scan(xc, lam, wa, ba, wx, bx, h0, reverse):
    xf = xc.astype(jnp.float32)
    if reverse:
        xf = jnp.flip(xf, axis=1)
    r = jax.nn.sigmoid(block_diag(xf, wa, ba))
    i = jax.nn.sigmoid(block_diag(xf, wx, bx))
    log_a = -LRU_C * r * jax.nn.softplus(-lam.astype(jnp.float32))
    a = jnp.exp(log_a)
    bterm = jnp.sqrt(-jnp.expm1(2.0 * log_a)) * (i * xf)
    A, Bc = lax.associative_scan(_scan_combine, (a, bterm), axis=1)
    h = A * h0[:, None, :].astype(jnp.float32) + Bc
    return jnp.flip(h, axis=1) if reverse else h


def mixer_block(h, P, ctx):
    B, L, _ = h.shape
    z = h @ P['w_in'] + P['b_in']
    u_pool, qg, kg, vg, qw, kw, vw, x_lru, y_lru, z_gate = jnp.split(z, IN_SPLITS, axis=-1)
    qg = rms_norm(qg.reshape(B, L, N_HEADS, HEAD_DIM), P['qn_g'])
    kg = rms_norm(kg.reshape(B, L, N_KV, HEAD_DIM), P['kn_g'])
    vg = vg.reshape(B, L, N_KV, HEAD_DIM)
    qw = qw.reshape(B, L, N_HEADS, HEAD_DIM)
    kw = kw.reshape(B, L, N_KV, HEAD_DIM)
    vw = vw.reshape(B, L, N_KV, HEAD_DIM)
    xc = dwconv(x_lru, P['lru_conv_w'], P['lru_conv_b'], LRU_CONV_LEFT)
    if ctx is None:
        o_g = block_attention(qg, kg, vg)
        o_w = block_attention(qw, kw, vw, P['sink'])
        h0 = jnp.zeros((B, LRU_W), jnp.float32)
        h_f = rglru_scan(xc, P['lru_lambda'][0], P['lru_wa'][0], P['lru_ba'][0], P['lru_wx'][0], P['lru_bx'][0], h0, False)
        h_b = rglru_scan(xc, P['lru_lambda'][1], P['lru_wa'][1], P['lru_ba'][1], P['lru_wx'][1], P['lru_bx'][1], h0, True)
        st = jnp.stack([h_f[:, -1], h_b[:, 0]], axis=1).astype(h.dtype)
        ctx_out = (kg, vg, kw, vw, st)
    else:
        ck_g, cv_g, ck_w, cv_w, st = ctx
        o_g = block_attention(axial_rope(qg), jnp.concatenate([ck_g, axial_rope(kg)], axis=1),
                              jnp.concatenate([cv_g, vg], axis=1))
        o_w = window_attention(axial_rope(qw), axial_rope(kw), vw, ck_w, cv_w, P['sink'])
        h_f = rglru_scan(xc, P['lru_lambda'][0], P['lru_wa'][0], P['lru_ba'][0], P['lru_wx'][0], P['lru_bx'][0], st[:, 0], False)
        h_b = rglru_scan(xc, P['lru_lambda'][1], P['lru_wa'][1], P['lru_ba'][1], P['lru_wx'][1], P['lru_bx'][1], st[:, 1], True)
        ctx_out = None
    o_lru = ((h_f + h_b) * jax.nn.gelu(y_lru.astype(jnp.float32))).astype(h.dtype)
    o_pool = pool_mixer(u_pool, P['pool_w'], P['pool_scale'])
    gates = jax.nn.sigmoid(z_gate.astype(jnp.float32)).astype(h.dtype).reshape(B, L, N_BRANCH, D_MODEL)
    branches = (o_pool, o_g, o_w, o_lru)
    merged = gates[:, :, 0] * (branches[0] @ P['w_br'][0])
    for n in range(1, N_BRANCH):
        merged = merged + gates[:, :, n] * (branches[n] @ P['w_br'][n])
    return merged @ P['w_o'], ctx_out


def conv_ffn(h, P):
    up = dwconv(h @ P['w_up'], P['ffn_conv_w'], P['ffn_conv_b'], FFN_CONV_LEFT)
    u, g = jnp.split(up, 2, axis=-1)
    return (jax.nn.silu(g) * u) @ P['w_down']


def trunk_layer(x, ada, P, ctx):
    sh1, sc1, g1, sh2, sc2, g2 = [a[:, None, :] for a in jnp.split(ada, 6, axis=-1)]
    mix, ctx_out = mixer_block(modulate(layer_norm(x), sh1, sc1), P, ctx)
    x = layer_norm(ALPHA * x + g1 * mix, P['ln1_g'], P['ln1_b'])
    ff = conv_ffn(modulate(layer_norm(x), sh2, sc2), P)
    x = layer_norm(ALPHA * x + g2 * ff, P['ln2_g'], P['ln2_b'])
    return x, ctx_out


def setup_inputs(seed: int = 0) -> dict:
    key = jax.random.key(seed)
    ks = iter(jax.random.split(key, 48))
    f32 = jnp.float32

    def nrm(shape, s):
        return jax.random.normal(next(ks), shape, f32) * s

    D, F = D_MODEL, D_FF
    x_prompt = nrm((BATCH, SEQ, D), 1.0)
    x_sample = nrm((DEC_BATCH, DEC_SEQ, D), 1.0)
    cache_gk = nrm((DEC_BATCH, DEPTH, PAST_LEN, N_KV, HEAD_DIM), 1.0)
    cache_gv = nrm((DEC_BATCH, DEPTH, PAST_LEN, N_KV, HEAD_DIM), 1.0)
    cache_wk = nrm((DEC_BATCH, DEPTH, PAST_LEN, N_KV, HEAD_DIM), 1.0)
    cache_wv = nrm((DEC_BATCH, DEPTH, PAST_LEN, N_KV, HEAD_DIM), 1.0)
    state_lru = nrm((DEC_BATCH, DEPTH, 2, LRU_W), 0.5)
    c = nrm((DEC_BATCH, D), 1.0)
    c_ctx = nrm((D,), 1.0)
    w_ada = nrm((DEPTH, D, 6 * D), 0.5 * D ** -0.5)
    b_ada = nrm((DEPTH, 6 * D), 0.02)
    w_in = nrm((DEPTH, D, IN_W), D ** -0.5)
    b_in = nrm((DEPTH, IN_W), 0.02)
    pool_w = nrm((DEPTH, len(POOL_WINDOWS), POOL_GW, POOL_GW), POOL_GW ** -0.5)
    pool_scale = 1.0 + nrm((DEPTH, POOL_W), 0.02)
    qn_g = 1.0 + nrm((DEPTH, HEAD_DIM), 0.02)
    kn_g = 1.0 + nrm((DEPTH, HEAD_DIM), 0.02)
    sink = nrm((DEPTH, N_HEADS), 0.5)
    lru_conv_w = nrm((DEPTH, LRU_CONV, LRU_W), LRU_CONV ** -0.5)
    lru_conv_b = nrm((DEPTH, LRU_W), 0.02)
    u = jax.random.uniform(next(ks), (DEPTH, 2, LRU_W), f32, 0.9, 0.999)
    s = u ** (1.0 / LRU_C)
    lru_lambda = jnp.log(s) - jnp.log1p(-s)
    lru_wa = nrm((DEPTH, 2, LRU_BLOCKS, LRU_BW, LRU_BW), LRU_BW ** -0.5)
    lru_ba = nrm((DEPTH, 2, LRU_W), 0.02)
    lru_wx = nrm((DEPTH, 2, LRU_BLOCKS, LRU_BW, LRU_BW), LRU_BW ** -0.5)
    lru_bx = nrm((DEPTH, 2, LRU_W), 0.02)
    w_br = nrm((DEPTH, N_BRANCH, BRANCH_W, D), BETA * BRANCH_W ** -0.5)
    w_o = nrm((DEPTH, D, D), BETA * D ** -0.5)
    ln1_g = 1.0 + nrm((DEPTH, D), 0.02)
    ln1_b = nrm((DEPTH, D), 0.02)
    w_up = nrm((DEPTH, D, 2 * F), D ** -0.5)
    ffn_conv_w = nrm((DEPTH, FFN_CONV, 2 * F), FFN_CONV ** -0.5)
    ffn_conv_b = nrm((DEPTH, 2 * F), 0.02)
    w_down = nrm((DEPTH, F, D), BETA * F ** -0.5)
    ln2_g = 1.0 + nrm((DEPTH, D), 0.02)
    ln2_b = nrm((DEPTH, D), 0.02)
    return {'x_prompt': x_prompt, 'x_sample': x_sample, 'cache_gk': cache_gk, 'cache_gv': cache_gv,
            'cache_wk': cache_wk, 'cache_wv': cache_wv, 'state_lru': state_lru, 'c': c, 'c_ctx': c_ctx,
            'w_ada': w_ada, 'b_ada': b_ada, 'w_in': w_in, 'b_in': b_in, 'pool_w': pool_w, 'pool_scale': pool_scale,
            'qn_g': qn_g, 'kn_g': kn_g, 'sink': sink, 'lru_conv_w': lru_conv_w, 'lru_conv_b': lru_conv_b,
            'lru_lambda': lru_lambda, 'lru_wa': lru_wa, 'lru_ba': lru_ba, 'lru_wx': lru_wx, 'lru_bx': lru_bx,
            'w_br': w_br, 'w_o': w_o, 'ln1_g': ln1_g, 'ln1_b': ln1_b, 'w_up': w_up, 'ffn_conv_w': ffn_conv_w,
            'ffn_conv_b': ffn_conv_b, 'w_down': w_down, 'ln2_g': ln2_g, 'ln2_b': ln2_b}


def reference(x_prompt, x_sample, cache_gk, cache_gv, cache_wk, cache_wv, state_lru, c, c_ctx,
              w_ada, b_ada, w_in, b_in, pool_w, pool_scale, qn_g, kn_g, sink, lru_conv_w, lru_conv_b,
              lru_lambda, lru_wa, lru_ba, lru_wx, lru_bx, w_br, w_o, ln1_g, ln1_b, w_up, ffn_conv_w,
              ffn_conv_b, w_down, ln2_g, ln2_b):
    xp = x_prompt
    xs = x_sample
    gk_list, gv_list, wk_list, wv_list, st_list = [], [], [], [], []
    for l in range(DEPTH):
        P = {'w_in': w_in[l], 'b_in': b_in[l], 'pool_w': pool_w[l], 'pool_scale': pool_scale[l],
             'qn_g': qn_g[l], 'kn_g': kn_g[l], 'sink': sink[l], 'lru_conv_w': lru_conv_w[l],
             'lru_conv_b': lru_conv_b[l], 'lru_lambda': lru_lambda[l], 'lru_wa': lru_wa[l], 'lru_ba': lru_ba[l],
             'lru_wx': lru_wx[l], 'lru_bx': lru_bx[l], 'w_br': w_br[l], 'w_o': w_o[l],
             'ln1_g': ln1_g[l], 'ln1_b': ln1_b[l], 'w_up': w_up[l], 'ffn_conv_w': ffn_conv_w[l],
             'ffn_conv_b': ffn_conv_b[l], 'w_down': w_down[l], 'ln2_g': ln2_g[l], 'ln2_b': ln2_b[l]}
        ada_ctx = jax.nn.silu(c_ctx[None, :]) @ w_ada[l] + b_ada[l]
        xp, (gk, gv, wk, wv, st) = trunk_layer(xp, ada_ctx, P, None)
        gk_list.append(gk)
        gv_list.append(gv)
        wk_list.append(wk)
        wv_list.append(wv)
        st_list.append(st)
        ada_s = jax.nn.silu(c) @ w_ada[l] + b_ada[l]
        ctx_l = (cache_gk[:, l], cache_gv[:, l], cache_wk[:, l], cache_wv[:, l], state_lru[:, l])
        xs, _ = trunk_layer(xs, ada_s, P, ctx_l)
    y_prompt = xp
    y_sample = xs
    new_gk = jnp.stack(gk_list, axis=1)
    new_gv = jnp.stack(gv_list, axis=1)
    new_wk = jnp.stack(wk_list, axis=1)
    new_wv = jnp.stack(wv_list, axis=1)
    new_lru = jnp.stack(st_list, axis=1)
    return (y_prompt, y_sample, new_gk, new_gv, new_wk, new_wv, new_lru)
```

```python
import functools

import jax
import jax.numpy as jnp
from jax import lax
from jax.experimental import pallas as pl
from jax.experimental.pallas import tpu as pltpu

F32 = jnp.float32
BF16 = jnp.bfloat16

D_MODEL = 4096
DEPTH = 2
GRID_W = 64
BRANCH_W = D_MODEL // 4
N_BRANCH = 4
HEAD_DIM = 128
N_HEADS = BRANCH_W // HEAD_DIM
N_KV = N_HEADS // 4
N_GROUP = N_HEADS // N_KV
Q_W = N_HEADS * HEAD_DIM
KV_W = N_KV * HEAD_DIM
WINDOW = 128
ROPE_THETA = 10000.0
ATTN_SCALE = HEAD_DIM ** -0.5
NEG_INF = -1e30
POOL_WINDOWS = (2, 4, 8, 16)
POOL_GW = BRANCH_W // len(POOL_WINDOWS)
LRU_W = BRANCH_W
LRU_BW = 128
LRU_C = 8.0
D_FF = 11008
ALPHA = (2 * DEPTH) ** 0.25
IN_SIZES = (BRANCH_W, Q_W, KV_W, KV_W, Q_W, KV_W, KV_W, LRU_W, LRU_W, N_BRANCH * D_MODEL)
IN_W = sum(IN_SIZES)

OFF_POOL = 0
OFF_QG = OFF_POOL + BRANCH_W
OFF_KG = OFF_QG + Q_W
OFF_VG = OFF_KG + KV_W
OFF_QW = OFF_VG + KV_W
OFF_KW = OFF_QW + Q_W
OFF_VW = OFF_KW + KV_W
OFF_XL = OFF_VW + KV_W
OFF_YL = OFF_XL + LRU_W
OFF_GATE = OFF_YL + LRU_W

LANE = 128
SUBLANE = 8
D_FF_PAD = 11264
ADA_ROWS = 16
GQ_W = N_GROUP * HEAD_DIM
VMEM_LIMIT = 56 << 20


def _cparams(*sem):
    return pltpu.CompilerParams(dimension_semantics=sem, vmem_limit_bytes=VMEM_LIMIT)


def _dot(a, b):
    return jnp.dot(a, b, preferred_element_type=F32)


def _dot_nt(a, b):
    return lax.dot_general(a, b, (((1,), (1,)), ((), ())), preferred_element_type=F32)


def _ln(x, eps=1e-6):
    mu = jnp.mean(x, axis=-1, keepdims=True)
    xc = x - mu
    var = jnp.mean(xc * xc, axis=-1, keepdims=True)
    return xc * lax.rsqrt(var + eps)


def _rms(x, g, eps=1e-6):
    return x * lax.rsqrt(jnp.mean(x * x, axis=-1, keepdims=True) + eps) * g


def _ada_kernel(c_ref, w_ref, b_ref, o_ref):
    c = c_ref[...]
    s = (c * jax.nn.sigmoid(c)).astype(BF16)
    o_ref[...] = _dot(s, w_ref[...].astype(BF16)) + b_ref[...]


def _ada(c_all, w_ada, b_ada, tn=512):
    n = w_ada.shape[-1]
    return pl.pallas_call(
        _ada_kernel,
        grid=(DEPTH, n // tn),
        in_specs=[pl.BlockSpec((ADA_ROWS, D_MODEL), lambda l, j: (0, 0)),
                  pl.BlockSpec((None, D_MODEL, tn), lambda l, j: (l, 0, j)),
                  pl.BlockSpec((None, 1, tn), lambda l, j: (l, 0, j))],
        out_specs=pl.BlockSpec((None, ADA_ROWS, tn), lambda l, j: (l, 0, j)),
        out_shape=jax.ShapeDtypeStruct((DEPTH, ADA_ROWS, n), F32),
        compiler_params=_cparams("parallel", "parallel"),
        name="ada",
    )(c_all, w_ada, b_ada.reshape(DEPTH, 1, n))


def _ada_spec(which, tm, rows_per_cond):
    if rows_per_cond is None:
        return pl.BlockSpec((None, 1, D_MODEL), lambda i: (which, 0, 0))
    return pl.BlockSpec((None, 1, D_MODEL), lambda i: ((1 + (i * tm) // rows_per_cond) * 6 + which, 0, 0))


def _lnmod_kernel(x_ref, sh_ref, sc_ref, h_ref):
    h_ref[...] = (_ln(x_ref[...]) * (1.0 + sc_ref[...]) + sh_ref[...]).astype(BF16)


def _lnmod(x, ada3, rows_per_cond, tm=256):
    m = x.shape[0]
    return pl.pallas_call(
        _lnmod_kernel,
        grid=(m // tm,),
        in_specs=[pl.BlockSpec((tm, D_MODEL), lambda i: (i, 0)),
                  _ada_spec(0, tm, rows_per_cond), _ada_spec(1, tm, rows_per_cond)],
        out_specs=pl.BlockSpec((tm, D_MODEL), lambda i: (i, 0)),
        out_shape=jax.ShapeDtypeStruct((m, D_MODEL), BF16),
        compiler_params=_cparams("parallel"),
        name="lnmod",
    )(x, ada3, ada3)


def _post_kernel(x_ref, y_ref, g_ref, lg_ref, lb_ref, sh_ref, sc_ref, xo_ref, h_ref):
    xn = _ln(ALPHA * x_ref[...] + g_ref[...] * y_ref[...]) * lg_ref[...] + lb_ref[...]
    xo_ref[...] = xn
    h_ref[...] = (_ln(xn) * (1.0 + sc_ref[...]) + sh_ref[...]).astype(BF16)


def _post_last_kernel(x_ref, y_ref, g_ref, lg_ref, lb_ref, xo_ref):
    xo_ref[...] = _ln(ALPHA * x_ref[...] + g_ref[...] * y_ref[...]) * lg_ref[...] + lb_ref[...]


def _post(x, y, ada3, gate_idx, ln_g, ln_b, rows_per_cond, nxt=None, tm=256):
    m = x.shape[0]
    row = pl.BlockSpec((tm, D_MODEL), lambda i: (i, 0))
    vec = pl.BlockSpec((1, D_MODEL), lambda i: (0, 0))
    in_specs = [row, row, _ada_spec(gate_idx, tm, rows_per_cond), vec, vec]
    args = [x, y, ada3, ln_g.reshape(1, D_MODEL), ln_b.reshape(1, D_MODEL)]
    if nxt is None:
        return pl.pallas_call(
            _post_last_kernel, grid=(m // tm,), in_specs=in_specs, out_specs=row,
            out_shape=jax.ShapeDtypeStruct((m, D_MODEL), F32),
            compiler_params=_cparams("parallel"), name="post_last",
        )(*args), None
    nxt_ada3, sh_idx, sc_idx = nxt
    in_specs += [_ada_spec(sh_idx, tm, rows_per_cond), _ada_spec(sc_idx, tm, rows_per_cond)]
    args += [nxt_ada3, nxt_ada3]
    return pl.pallas_call(
        _post_kernel, grid=(m // tm,), in_specs=in_specs, out_specs=(row, row),
        out_shape=(jax.ShapeDtypeStruct((m, D_MODEL), F32), jax.ShapeDtypeStruct((m, D_MODEL), BF16)),
        compiler_params=_cparams("parallel"), name="post",
    )(*args)


def _mm_bias_kernel(x_ref, w_ref, b_ref, o_ref):
    o_ref[...] = (_dot(x_ref[...], w_ref[...]) + b_ref[...]).astype(o_ref.dtype)


def _mm_kernel(x_ref, w_ref, o_ref):
    o_ref[...] = _dot(x_ref[...], w_ref[...]).astype(o_ref.dtype)


def _matmul(x, w, b, tm, tn, name):
    m, k = x.shape
    n = w.shape[1]
    in_specs = [pl.BlockSpec((tm, k), lambda i, j: (i, 0)), pl.BlockSpec((k, tn), lambda i, j: (0, j))]
    args = [x, w]
    body = _mm_kernel
    if b is not None:
        in_specs.append(pl.BlockSpec((1, tn), lambda i, j: (0, j)))
        args.append(b.reshape(1, n))
        body = _mm_bias_kernel
    return pl.pallas_call(
        body, grid=(m // tm, n // tn), in_specs=in_specs,
        out_specs=pl.BlockSpec((tm, tn), lambda i, j: (i, j)),
        out_shape=jax.ShapeDtypeStruct((m, n), F32),
        compiler_params=_cparams("parallel", "arbitrary"), name=name,
    )(*args)


def _pool_kernel(u_ref, w_ref, sc_ref, o_ref):
    seq = u_ref.shape[0]
    t = lax.broadcasted_iota(jnp.int32, (seq, seq), 0)
    s = lax.broadcasted_iota(jnp.int32, (seq, seq), 1)
    d = s - t
    tcol = lax.broadcasted_iota(jnp.int32, (seq, 1), 0)
    for gi, win in enumerate(POOL_WINDOWS):
        half = win // 2
        cols = slice(gi * POOL_GW, (gi + 1) * POOL_GW)
        band = jnp.where((d >= -half) & (d < half), 1.0, 0.0).astype(BF16)
        cnt = jnp.minimum(tcol + half, seq) - jnp.maximum(tcol - half, 0)
        u = u_ref[:, cols]
        u_hi = u.astype(BF16)
        u_lo = (u - u_hi.astype(F32)).astype(BF16)
        mean = (_dot(band, u_hi) + _dot(band, u_lo)) / cnt.astype(F32)
        y = _dot((mean - u).astype(BF16), w_ref[gi])
        o_ref[:, cols] = (y * sc_ref[:, cols]).astype(BF16)


def _pool(z, pool_w, pool_scale, nb, seq):
    return pl.pallas_call(
        _pool_kernel,
        grid=(nb,),
        in_specs=[pl.BlockSpec((seq, BRANCH_W), lambda b: (b, OFF_POOL // BRANCH_W)),
                  pl.BlockSpec((len(POOL_WINDOWS), POOL_GW, POOL_GW), lambda b: (0, 0, 0)),
                  pl.BlockSpec((1, BRANCH_W), lambda b: (0, 0))],
        out_specs=pl.BlockSpec((seq, BRANCH_W), lambda b: (b, 0)),
        out_shape=jax.ShapeDtypeStruct((nb * seq, BRANCH_W), BF16),
        compiler_params=_cparams("parallel"), name="pool",
    )(z, pool_w, pool_scale.reshape(1, BRANCH_W))


def _softmax_attend(q, k, v, sink):
    s = _dot_nt(q, k) * ATTN_SCALE
    m = jnp.max(s, axis=-1, keepdims=True)
    if sink is not None:
        m = jnp.maximum(m, sink)
    e = jnp.exp(s - m)
    den = jnp.sum(e, axis=-1, keepdims=True)
    if sink is not None:
        den = den + jnp.exp(sink - m)
    return _dot((e * (1.0 / den)).astype(BF16), v)


def _ctx_attn_kernel(sink_ref, qg_ref, kg_ref, vg_ref, qw_ref, kw_ref, vw_ref, qn_ref, kn_ref,
                     og_ref, ow_ref, gk_ref, gv_ref, wk_ref, wv_ref):
    kvh = pl.program_id(1)
    kg = _rms(kg_ref[...], kn_ref[...])
    gk_ref[...] = kg
    gv_ref[...] = vg_ref[...]
    wk_ref[...] = kw_ref[...]
    wv_ref[...] = vw_ref[...]
    kg_b, vg_b = kg.astype(BF16), vg_ref[...].astype(BF16)
    kw_b, vw_b = kw_ref[...].astype(BF16), vw_ref[...].astype(BF16)
    for h in range(N_GROUP):
        cols = slice(h * HEAD_DIM, (h + 1) * HEAD_DIM)
        qg = _rms(qg_ref[:, cols], qn_ref[...]).astype(BF16)
        og_ref[:, cols] = _softmax_attend(qg, kg_b, vg_b, None).astype(BF16)
        sink = sink_ref[kvh * N_GROUP + h]
        ow_ref[:, cols] = _softmax_attend(qw_ref[:, cols].astype(BF16), kw_b, vw_b, sink).astype(BF16)


def _ctx_attn(z, sink, qn_g, kn_g, nb, seq):
    q_spec = lambda off: pl.BlockSpec((seq, GQ_W), lambda b, k: (b, off // GQ_W + k))
    kv_spec = lambda off: pl.BlockSpec((seq, HEAD_DIM), lambda b, k: (b, off // HEAD_DIM + k))
    vec = pl.BlockSpec((1, HEAD_DIM), lambda b, k: (0, 0))
    o_spec = pl.BlockSpec((seq, GQ_W), lambda b, k: (b, k))
    c_spec = pl.BlockSpec((seq, HEAD_DIM), lambda b, k: (b, k))
    m = nb * seq
    return pl.pallas_call(
        _ctx_attn_kernel,
        grid=(nb, N_KV),
        in_specs=[pl.BlockSpec(memory_space=pltpu.SMEM),
                  q_spec(OFF_QG), kv_spec(OFF_KG), kv_spec(OFF_VG),
                  q_spec(OFF_QW), kv_spec(OFF_KW), kv_spec(OFF_VW), vec, vec],
        out_specs=(o_spec, o_spec, c_spec, c_spec, c_spec, c_spec),
        out_shape=(jax.ShapeDtypeStruct((m, Q_W), BF16), jax.ShapeDtypeStruct((m, Q_W), BF16))
        + tuple(jax.ShapeDtypeStruct((m, KV_W), F32) for _ in range(4)),
        compiler_params=_cparams("parallel", "parallel"), name="ctx_attn",
    )(sink, z, z, z, z, z, z, qn_g.reshape(1, HEAD_DIM), kn_g.reshape(1, HEAD_DIM))


def _rope(x, cos, sin, low_half):
    partner = jnp.where(low_half, pltpu.roll(x, HEAD_DIM - 32, axis=1), pltpu.roll(x, 32, axis=1))
    return x * cos + partner * sin


def _low_half(rows):
    lane = lax.broadcasted_iota(jnp.int32, (rows, HEAD_DIM), 1)
    return (lane & 63) < 32


def _lat_global_kernel(q_ref, k_ref, v_ref, ck_ref, cv_ref, cos_ref, sin_ref, qn_ref, kn_ref,
                       o_ref, kall, vall, *, tq):
    seq = q_ref.shape[0]
    past = ck_ref.shape[0]
    kall[0:past, :] = ck_ref[...].astype(BF16)
    vall[0:past, :] = cv_ref[...].astype(BF16)
    kn = _rope(_rms(k_ref[...], kn_ref[...]), cos_ref[...], sin_ref[...], _low_half(seq))
    kall[past:past + seq, :] = kn.astype(BF16)
    vall[past:past + seq, :] = v_ref[...].astype(BF16)
    low = _low_half(tq)

    def q_block(qb, carry):
        rows = pl.ds(pl.multiple_of(qb * tq, tq), tq)
        cos, sin = cos_ref[rows, :], sin_ref[rows, :]
        for h in range(N_GROUP):
            cols = slice(h * HEAD_DIM, (h + 1) * HEAD_DIM)
            q = _rope(_rms(q_ref[rows, cols], qn_ref[...]), cos, sin, low).astype(BF16)
            o_ref[rows, cols] = _softmax_attend(q, kall[...], vall[...], None).astype(BF16)
        return carry

    lax.fori_loop(0, seq // tq, q_block, 0)


def _lat_global(z, cache_k, cache_v, layer, cos, sin, qn_g, kn_g, nb, seq, tq=256):
    past = cache_k.shape[2]
    q_spec = pl.BlockSpec((seq, GQ_W), lambda b, k: (b, OFF_QG // GQ_W + k))
    kv_spec = lambda off: pl.BlockSpec((seq, HEAD_DIM), lambda b, k: (b, off // HEAD_DIM + k))
    c_spec = pl.BlockSpec((None, None, past, HEAD_DIM), lambda b, k: (b, layer, 0, k))
    tab = pl.BlockSpec((seq, HEAD_DIM), lambda b, k: (0, 0))
    vec = pl.BlockSpec((1, HEAD_DIM), lambda b, k: (0, 0))
    return pl.pallas_call(
        functools.partial(_lat_global_kernel, tq=tq),
        grid=(nb, N_KV),
        in_specs=[q_spec, kv_spec(OFF_KG), kv_spec(OFF_VG), c_spec, c_spec, tab, tab, vec, vec],
        out_specs=pl.BlockSpec((seq, GQ_W), lambda b, k: (b, k)),
        out_shape=jax.ShapeDtypeStruct((nb * seq, Q_W), BF16),
        scratch_shapes=[pltpu.VMEM((past + seq, HEAD_DIM), BF16), pltpu.VMEM((past + seq, HEAD_DIM), BF16)],
        compiler_params=_cparams("parallel", "parallel"), name="lat_global",
    )(z, z, z, cache_k, cache_v, cos, sin, qn_g.reshape(1, HEAD_DIM), kn_g.reshape(1, HEAD_DIM))


def _lat_window_kernel(sink_ref, q_ref, k_ref, v_ref, ck_ref, cv_ref, cos_ref, sin_ref,
                       o_ref, kpad, vpad):
    seq = q_ref.shape[0]
    blk = WINDOW
    kvh = pl.program_id(1)
    zeros = jnp.zeros((blk, HEAD_DIM), BF16)
    kpad[0:blk, :] = zeros
    vpad[0:blk, :] = zeros
    kpad[blk + seq:2 * blk + seq, :] = zeros
    vpad[blk + seq:2 * blk + seq, :] = zeros
    kpad[blk:blk + seq, :] = _rope(k_ref[...], cos_ref[...], sin_ref[...], _low_half(seq)).astype(BF16)
    vpad[blk:blk + seq, :] = v_ref[...].astype(BF16)
    ck = ck_ref[...].astype(BF16)
    cv = cv_ref[...].astype(BF16)
    low = _low_half(blk)
    rows4 = N_GROUP * blk
    head = lax.broadcasted_iota(jnp.int32, (rows4, 1), 0) // blk
    sink = jnp.zeros((rows4, 1), F32)
    for h in range(N_GROUP):
        sink = jnp.where(head == h, sink_ref[kvh * N_GROUP + h], sink)
    qi = lax.broadcasted_iota(jnp.int32, (rows4, 3 * blk), 0) & (blk - 1)
    kj = lax.broadcasted_iota(jnp.int32, (rows4, 3 * blk), 1)
    in_band = jnp.abs(qi + blk - kj) <= WINDOW

    def q_block(n, carry):
        r0 = pl.multiple_of(n * blk, blk)
        rows = pl.ds(r0, blk)
        cos, sin = cos_ref[rows, :], sin_ref[rows, :]
        q4 = jnp.concatenate(
            [_rope(q_ref[rows, h * HEAD_DIM:(h + 1) * HEAD_DIM], cos, sin, low) for h in range(N_GROUP)],
            axis=0).astype(BF16)
        tk = (n - 1) * blk + kj
        valid = in_band & (tk >= 0) & (tk < seq)
        s_loc = jnp.where(valid, _dot_nt(q4, kpad[pl.ds(r0, 3 * blk), :]) * ATTN_SCALE, NEG_INF)
        s_ctx = _dot_nt(q4, ck) * ATTN_SCALE
        m = jnp.maximum(jnp.maximum(jnp.max(s_loc, axis=-1, keepdims=True),
                                    jnp.max(s_ctx, axis=-1, keepdims=True)), sink)
        e_loc = jnp.exp(s_loc - m)
        e_ctx = jnp.exp(s_ctx - m)
        den = (jnp.sum(e_loc, axis=-1, keepdims=True) + jnp.sum(e_ctx, axis=-1, keepdims=True)
               + jnp.exp(sink - m))
        inv = 1.0 / den
        o4 = (_dot((e_loc * inv).astype(BF16), vpad[pl.ds(r0, 3 * blk), :])
              + _dot((e_ctx * inv).astype(BF16), cv))
        for h in range(N_GROUP):
            o_ref[rows, h * HEAD_DIM:(h + 1) * HEAD_DIM] = o4[h * blk:(h + 1) * blk].astype(BF16)
        return carry

    lax.fori_loop(0, seq // blk, q_block, 0)


def _lat_window(z, cache_k, cache_v, layer, cos, sin, sink, nb, seq):
    past = cache_k.shape[2]
    q_spec = pl.BlockSpec((seq, GQ_W), lambda b, k: (b, OFF_QW // GQ_W + k))
    kv_spec = lambda off: pl.BlockSpec((seq, HEAD_DIM), lambda b, k: (b, off // HEAD_DIM + k))
    c_spec = pl.BlockSpec((None, None, past, HEAD_DIM), lambda b, k: (b, layer, 0, k))
    tab = pl.BlockSpec((seq, HEAD_DIM), lambda b, k: (0, 0))
    return pl.pallas_call(
        _lat_window_kernel,
        grid=(nb, N_KV),
        in_specs=[pl.BlockSpec(memory_space=pltpu.SMEM), q_spec, kv_spec(OFF_KW), kv_spec(OFF_VW),
                  c_spec, c_spec, tab, tab],
        out_specs=pl.BlockSpec((seq, GQ_W), lambda b, k: (b, k)),
        out_shape=jax.ShapeDtypeStruct((nb * seq, Q_W), BF16),
        scratch_shapes=[pltpu.VMEM((seq + 2 * WINDOW, HEAD_DIM), BF16),
                        pltpu.VMEM((seq + 2 * WINDOW, HEAD_DIM), BF16)],
        compiler_params=_cparams("parallel", "parallel"), name="lat_window",
    )(sink, z, z, z, cache_k, cache_v, cos, sin)


def _lru_kernel(x_ref, y_ref, cw_ref, cb_ref, lam_ref, wa_ref, ba_ref, wx_ref, bx_ref, h0_ref,
                o_ref, st_ref, a_s, b_s, h_s):
    seq, width = x_ref.shape
    n_tiles = seq // SUBLANE
    x = x_ref[...]
    row = lax.broadcasted_iota(jnp.int32, (seq, width), 0)
    xc = (cb_ref[...]
          + jnp.where(row >= 2, pltpu.roll(x, 2, axis=0), 0.0) * cw_ref[0:1, :]
          + jnp.where(row >= 1, pltpu.roll(x, 1, axis=0), 0.0) * cw_ref[1:2, :]
          + x * cw_ref[2:3, :]
          + jnp.where(row < seq - 1, pltpu.roll(x, seq - 1, axis=0), 0.0) * cw_ref[3:4, :])
    xcb = xc.astype(BF16)
    in_tile = row & (SUBLANE - 1)

    def block_diag(w_ref, d):
        return jnp.concatenate(
            [_dot(xcb[:, j * LRU_BW:(j + 1) * LRU_BW], w_ref[d, j]) for j in range(width // LRU_BW)], axis=1)

    for d in range(2):
        backward = d == 1
        r = jax.nn.sigmoid(block_diag(wa_ref, d) + ba_ref[d:d + 1, :])
        gate_i = jax.nn.sigmoid(block_diag(wx_ref, d) + bx_ref[d:d + 1, :])
        neg_lam = -lam_ref[d:d + 1, :]
        softplus = jnp.maximum(neg_lam, 0.0) + jnp.log1p(jnp.exp(-jnp.abs(neg_lam)))
        log_a = -LRU_C * r * softplus
        a = jnp.exp(log_a)
        th = jnp.tanh(log_a)
        b = jnp.sqrt(-2.0 * th / (1.0 - th)) * (gate_i * xc)
        for s in (1, 2, 4):
            shift = seq - s if backward else s
            take = in_tile < SUBLANE - s if backward else in_tile >= s
            b = jnp.where(take, a * pltpu.roll(b, shift, axis=0) + b, b)
            a = jnp.where(take, a * pltpu.roll(a, shift, axis=0), a)
        a_s[...] = a
        b_s[...] = b

        def tile_step(j, h, backward=backward):
            t = n_tiles - 1 - j if backward else j
            rows = pl.ds(pl.multiple_of(t * SUBLANE, SUBLANE), SUBLANE)
            h8 = a_s[rows, :] * h + b_s[rows, :]
            if backward:
                h_s[rows, :] = h_s[rows, :] + h8
                edge = h8[0:1, :]
            else:
                h_s[rows, :] = h8
                edge = h8[SUBLANE - 1:SUBLANE, :]
            return jnp.broadcast_to(edge, (SUBLANE, width))

        h_end = lax.fori_loop(0, n_tiles, tile_step, jnp.broadcast_to(h0_ref[d:d + 1, :], (SUBLANE, width)))
        st_ref[d:d + 1, :] = h_end[0:1, :]
    o_ref[...] = (h_s[...] * jax.nn.gelu(y_ref[...])).astype(BF16)


def _lru(z, h0, h0_layer, conv_w, conv_b, lam, wa, ba, wx, bx, nb, seq, cw=256):
    nc = LRU_W // cw
    nd = cw // LRU_BW
    vec2 = pl.BlockSpec((2, cw), lambda b, c: (0, c))
    wspec = pl.BlockSpec((2, nd, LRU_BW, LRU_BW), lambda b, c: (0, c, 0, 0))
    return pl.pallas_call(
        _lru_kernel,
        grid=(nb, nc),
        in_specs=[pl.BlockSpec((seq, cw), lambda b, c: (b, OFF_XL // cw + c)),
                  pl.BlockSpec((seq, cw), lambda b, c: (b, OFF_YL // cw + c)),
                  pl.BlockSpec((4, cw), lambda b, c: (0, c)),
                  pl.BlockSpec((1, cw), lambda b, c: (0, c)),
                  vec2, wspec, vec2, wspec, vec2,
                  pl.BlockSpec((None, None, 2, cw), lambda b, c: (b, h0_layer, 0, c))],
        out_specs=(pl.BlockSpec((seq, cw), lambda b, c: (b, c)),
                   pl.BlockSpec((None, 2, cw), lambda b, c: (b, 0, c))),
        out_shape=(jax.ShapeDtypeStruct((nb * seq, LRU_W), BF16), jax.ShapeDtypeStruct((nb, 2, LRU_W), F32)),
        scratch_shapes=[pltpu.VMEM((seq, cw), F32)] * 3,
        compiler_params=_cparams("parallel", "parallel"), name="lru",
    )(z, z, conv_w, conv_b.reshape(1, LRU_W), lam, wa, ba, wx, bx, h0)


def _merge_kernel(b0, b1, b2, b3, w_ref, g0, g1, g2, g3, o_ref):
    acc = None
    for n, (br, gr) in enumerate(((b0, g0), (b1, g1), (b2, g2), (b3, g3))):
        t = jax.nn.sigmoid(gr[...]) * _dot(br[...], w_ref[n])
        acc = t if acc is None else acc + t
    o_ref[...] = acc.astype(BF16)


def _merge(branches, w_br, z, tm=512, tn=512):
    m = z.shape[0]
    br = pl.BlockSpec((tm, BRANCH_W), lambda i, j: (i, 0))
    gate = lambda n: pl.BlockSpec((tm, tn), lambda i, j: (i, (OFF_GATE + n * D_MODEL) // tn + j))
    return pl.pallas_call(
        _merge_kernel,
        grid=(m // tm, D_MODEL // tn),
        in_specs=[br, br, br, br, pl.BlockSpec((N_BRANCH, BRANCH_W, tn), lambda i, j: (0, 0, j)),
                  gate(0), gate(1), gate(2), gate(3)],
        out_specs=pl.BlockSpec((tm, tn), lambda i, j: (i, j)),
        out_shape=jax.ShapeDtypeStruct((m, D_MODEL), BF16),
        compiler_params=_cparams("parallel", "arbitrary"), name="merge",
    )(*branches, w_br, z, z, z, z)


def _up_kernel(x_ref, wu_ref, wg_ref, cw_ref, cb_ref, o_ref, *, seq):
    x = x_ref[...]
    rows, tn = o_ref.shape
    pos = lax.broadcasted_iota(jnp.int32, (rows, tn), 0) & (seq - 1)
    first, last = pos == 0, pos == seq - 1

    def conv(v, w, b):
        prev = jnp.where(first, 0.0, pltpu.roll(v, 1, axis=0))
        nxt = jnp.where(last, 0.0, pltpu.roll(v, rows - 1, axis=0))
        return b + prev * w[0:1, :] + v * w[1:2, :] + nxt * w[2:3, :]

    u = conv(_dot(x, wu_ref[...]), cw_ref[0], cb_ref[0])
    g = conv(_dot(x, wg_ref[...]), cw_ref[1], cb_ref[1])
    o_ref[...] = (g * jax.nn.sigmoid(g) * u).astype(BF16)


def _up(h, w_up2, conv_w2, conv_b2, seq, tm=1024, tn=256):
    m = h.shape[0]
    half = lambda which: pl.BlockSpec((None, D_MODEL, tn), lambda i, j: (which, 0, j))
    return pl.pallas_call(
        functools.partial(_up_kernel, seq=seq),
        grid=(m // tm, D_FF_PAD // tn),
        in_specs=[pl.BlockSpec((tm, D_MODEL), lambda i, j: (i, 0)), half(0), half(1),
                  pl.BlockSpec((2, 3, tn), lambda i, j: (0, 0, j)),
                  pl.BlockSpec((2, 1, tn), lambda i, j: (0, 0, j))],
        out_specs=pl.BlockSpec((tm, tn), lambda i, j: (i, j)),
        out_shape=jax.ShapeDtypeStruct((m, D_FF_PAD), BF16),
        compiler_params=_cparams("parallel", "arbitrary"), name="ffn_up",
    )(h, w_up2, w_up2, conv_w2, conv_b2)


def _pad_ff(a):
    return jnp.pad(a, [(0, 0)] * (a.ndim - 1) + [(0, D_FF_PAD - D_FF)])


def _rope_tables(seq):
    t = jnp.arange(seq)
    half = HEAD_DIM // 2
    inv = ROPE_THETA ** (-jnp.arange(0, half, 2, dtype=F32) / half)
    ang_r = (t // GRID_W).astype(F32)[:, None] * inv[None, :]
    ang_c = (t % GRID_W).astype(F32)[:, None] * inv[None, :]
    cos = jnp.concatenate([jnp.cos(ang_r), jnp.cos(ang_r), jnp.cos(ang_c), jnp.cos(ang_c)], axis=1)
    sin = jnp.concatenate([-jnp.sin(ang_r), jnp.sin(ang_r), -jnp.sin(ang_c), jnp.sin(ang_c)], axis=1)
    return cos, sin


def kernel(x_prompt, x_sample, cache_gk, cache_gv, cache_wk, cache_wv, state_lru, c, c_ctx, w_ada, b_ada, w_in, b_in, pool_w, pool_scale, qn_g, kn_g, sink, lru_conv_w, lru_conv_b, lru_lambda, lru_wa, lru_ba, lru_wx, lru_bx, w_br, w_o, ln1_g, ln1_b, w_up, ffn_conv_w, ffn_conv_b, w_down, ln2_g, ln2_b):
    nb_c, seq_c, _ = x_prompt.shape
    nb_s, seq_s, _ = x_sample.shape
    past = cache_gk.shape[2]

    c_all = jnp.concatenate([c_ctx[None, :], c, jnp.zeros((ADA_ROWS - 1 - nb_s, D_MODEL), F32)], axis=0)
    ada = _ada(c_all, w_ada, b_ada)
    ada3 = [ada[l].reshape(ADA_ROWS * 6, 1, D_MODEL) for l in range(DEPTH)]
    cos, sin = _rope_tables(seq_s)
    caches = [a.reshape(nb_s, DEPTH, past, KV_W) for a in (cache_gk, cache_gv, cache_wk, cache_wv)]
    zero_state = jnp.zeros((nb_c, 1, 2, LRU_W), F32)

    groups = [dict(x=x_prompt.reshape(nb_c * seq_c, D_MODEL), nb=nb_c, seq=seq_c, cond=None),
              dict(x=x_sample.reshape(nb_s * seq_s, D_MODEL), nb=nb_s, seq=seq_s, cond=seq_s)]
    for g in groups:
        g["h"] = _lnmod(g["x"], ada3[0], g["cond"])
    new_cache = [[] for _ in range(5)]

    for l in range(DEPTH):
        w_in_l = w_in[l].astype(BF16)
        w_br_l = w_br[l].astype(BF16)
        w_o_l = w_o[l].astype(BF16)
        w_up_l = jnp.stack([_pad_ff(w_up[l][:, :D_FF]), _pad_ff(w_up[l][:, D_FF:])]).astype(BF16)
        w_down_l = jnp.pad(w_down[l], ((0, D_FF_PAD - D_FF), (0, 0))).astype(BF16)
        conv_w2 = jnp.stack([_pad_ff(ffn_conv_w[l][:, :D_FF]), _pad_ff(ffn_conv_w[l][:, D_FF:])])
        conv_b2 = jnp.stack([_pad_ff(ffn_conv_b[l][None, :D_FF]), _pad_ff(ffn_conv_b[l][None, D_FF:])])
        pool_w_l = pool_w[l].astype(BF16)
        wa_l = lru_wa[l].astype(BF16)
        wx_l = lru_wx[l].astype(BF16)
        for gi, g in enumerate(groups):
            nb, seq = g["nb"], g["seq"]
            z = _matmul(g["h"], w_in_l, b_in[l], 1024, 512, "in_proj")
            o_pool = _pool(z, pool_w_l, pool_scale[l], nb, seq)
            if gi == 0:
                o_g, o_w, gk, gv, wk, wv = _ctx_attn(z, sink[l], qn_g[l], kn_g[l], nb, seq)
                h0, h0_layer = zero_state, 0
            else:
                o_g = _lat_global(z, caches[0], caches[1], l, cos, sin, qn_g[l], kn_g[l], nb, seq)
                o_w = _lat_window(z, caches[2], caches[3], l, cos, sin, sink[l], nb, seq)
                h0, h0_layer = state_lru, l
            o_lru, st = _lru(z, h0, h0_layer, lru_conv_w[l], lru_conv_b[l], lru_lambda[l],
                             wa_l, lru_ba[l], wx_l, lru_bx[l], nb, seq)
            if gi == 0:
                for lst, a in zip(new_cache, (gk, gv, wk, wv, st)):
                    lst.append(a)
            merged = _merge((o_pool, o_g, o_w, o_lru), w_br_l, z)
            mix = _matmul(merged, w_o_l, None, 1024, 512, "out_proj")
            x1, h2 = _post(g["x"], mix, ada3[l], 2, ln1_g[l], ln1_b[l], g["cond"], nxt=(ada3[l], 3, 4))
            act = _up(h2, w_up_l, conv_w2, conv_b2, seq)
            ff = _matmul(act, w_down_l, None, 512, 256, "ffn_down")
            nxt = (ada3[l + 1], 0, 1) if l + 1 < DEPTH else None
            g["x"], g["h"] = _post(x1, ff, ada3[l], 5, ln2_g[l], ln2_b[l], g["cond"], nxt=nxt)

    y_prompt = groups[0]["x"].reshape(nb_c, seq_c, D_MODEL)
    y_sample = groups[1]["x"].reshape(nb_s, seq_s, D_MODEL)
    kv = [jnp.stack([a.reshape(nb_c, seq_c, N_KV, HEAD_DIM) for a in new_cache[i]], axis=1) for i in range(4)]
    new_lru = jnp.stack(new_cache[4], axis=1)
    return (y_prompt, y_sample, kv[0], kv[1], kv[2], kv[3], new_lru)
```

```python
import functools

import jax
import jax.numpy as jnp
from jax import lax
from jax.experimental import pallas as pl
from jax.experimental.pallas import tpu as pltpu

F32 = jnp.float32
BF16 = jnp.bfloat16

D_MODEL = 4096
DEPTH = 2
GRID_W = 64
BRANCH_W = D_MODEL // 4
N_BRANCH = 4
HEAD_DIM = 128
N_HEADS = BRANCH_W // HEAD_DIM
N_KV = N_HEADS // 4
N_GROUP = N_HEADS // N_KV
Q_W = N_HEADS * HEAD_DIM
KV_W = N_KV * HEAD_DIM
WINDOW = 128
ROPE_THETA = 10000.0
ATTN_SCALE = HEAD_DIM ** -0.5
NEG_INF = -1e30
POOL_WINDOWS = (2, 4, 8, 16)
POOL_GW = BRANCH_W // len(POOL_WINDOWS)
LRU_W = BRANCH_W
LRU_BW = 128
LRU_C = 8.0
D_FF = 11008
ALPHA = (2 * DEPTH) ** 0.25
IN_SIZES = (BRANCH_W, Q_W, KV_W, KV_W, Q_W, KV_W, KV_W, LRU_W, LRU_W, N_BRANCH * D_MODEL)
IN_W = sum(IN_SIZES)

OFF_POOL = 0
OFF_QG = OFF_POOL + BRANCH_W
OFF_KG = OFF_QG + Q_W
OFF_VG = OFF_KG + KV_W
OFF_QW = OFF_VG + KV_W
OFF_KW = OFF_QW + Q_W
OFF_VW = OFF_KW + KV_W
OFF_XL = OFF_VW + KV_W
OFF_YL = OFF_XL + LRU_W
OFF_GATE = OFF_YL + LRU_W

LANE = 128
SUBLANE = 8
ADA_ROWS = 16
GQ_W = N_GROUP * HEAD_DIM
VMEM_LIMIT = 56 << 20


def _cparams(*sem):
    return pltpu.CompilerParams(dimension_semantics=sem, vmem_limit_bytes=VMEM_LIMIT)


def _dot(a, b):
    return jnp.dot(a, b, preferred_element_type=F32)


def _dot_nt(a, b):
    return lax.dot_general(a, b, (((1,), (1,)), ((), ())), preferred_element_type=F32)


def _ln(x, eps=1e-6):
    mu = jnp.mean(x, axis=-1, keepdims=True)
    xc = x - mu
    var = jnp.mean(xc * xc, axis=-1, keepdims=True)
    return xc * lax.rsqrt(var + eps)


def _rms(x, g, eps=1e-6):
    return x * lax.rsqrt(jnp.mean(x * x, axis=-1, keepdims=True) + eps) * g


def _ada_kernel(c_ref, w_ref, b_ref, o_ref):
    c = c_ref[...]
    s = (c * jax.nn.sigmoid(c)).astype(BF16)
    o_ref[...] = _dot(s, w_ref[...].astype(BF16)) + b_ref[...]


def _ada(c_all, w_ada, b_ada, tn=512):
    n = w_ada.shape[-1]
    return pl.pallas_call(
        _ada_kernel,
        grid=(DEPTH, n // tn),
        in_specs=[pl.BlockSpec((ADA_ROWS, D_MODEL), lambda l, j: (0, 0)),
                  pl.BlockSpec((None, D_MODEL, tn), lambda l, j: (l, 0, j)),
                  pl.BlockSpec((None, 1, tn), lambda l, j: (l, 0, j))],
        out_specs=pl.BlockSpec((None, ADA_ROWS, tn), lambda l, j: (l, 0, j)),
        out_shape=jax.ShapeDtypeStruct((DEPTH, ADA_ROWS, n), F32),
        compiler_params=_cparams("parallel", "parallel"),
        name="ada",
    )(c_all, w_ada, b_ada.reshape(DEPTH, 1, n))


def _ada_spec(which, tm, rows_per_cond):
    if rows_per_cond is None:
        return pl.BlockSpec((None, 1, D_MODEL), lambda i: (which, 0, 0))
    return pl.BlockSpec((None, 1, D_MODEL), lambda i: ((1 + (i * tm) // rows_per_cond) * 6 + which, 0, 0))


def _lnmod_kernel(x_ref, sh_ref, sc_ref, h_ref):
    h_ref[...] = (_ln(x_ref[...]) * (1.0 + sc_ref[...]) + sh_ref[...]).astype(BF16)


def _lnmod(x, ada3, rows_per_cond, tm=256):
    m = x.shape[0]
    return pl.pallas_call(
        _lnmod_kernel,
        grid=(m // tm,),
        in_specs=[pl.BlockSpec((tm, D_MODEL), lambda i: (i, 0)),
                  _ada_spec(0, tm, rows_per_cond), _ada_spec(1, tm, rows_per_cond)],
        out_specs=pl.BlockSpec((tm, D_MODEL), lambda i: (i, 0)),
        out_shape=jax.ShapeDtypeStruct((m, D_MODEL), BF16),
        compiler_params=_cparams("parallel"),
        name="lnmod",
    )(x, ada3, ada3)


def _post_kernel(x_ref, y_ref, g_ref, lg_ref, lb_ref, sh_ref, sc_ref, xo_ref, h_ref):
    xn = _ln(ALPHA * x_ref[...] + g_ref[...] * y_ref[...]) * lg_ref[...] + lb_ref[...]
    xo_ref[...] = xn
    h_ref[...] = (_ln(xn) * (1.0 + sc_ref[...]) + sh_ref[...]).astype(BF16)


def _post_last_kernel(x_ref, y_ref, g_ref, lg_ref, lb_ref, xo_ref):
    xo_ref[...] = _ln(ALPHA * x_ref[...] + g_ref[...] * y_ref[...]) * lg_ref[...] + lb_ref[...]


def _post(x, y, ada3, gate_idx, ln_g, ln_b, rows_per_cond, nxt=None, tm=256):
    m = x.shape[0]
    row = pl.BlockSpec((tm, D_MODEL), lambda i: (i, 0))
    vec = pl.BlockSpec((1, D_MODEL), lambda i: (0, 0))
    in_specs = [row, row, _ada_spec(gate_idx, tm, rows_per_cond), vec, vec]
    args = [x, y, ada3, ln_g.reshape(1, D_MODEL), ln_b.reshape(1, D_MODEL)]
    if nxt is None:
        return pl.pallas_call(
            _post_last_kernel, grid=(m // tm,), in_specs=in_specs, out_specs=row,
            out_shape=jax.ShapeDtypeStruct((m, D_MODEL), F32),
            compiler_params=_cparams("parallel"), name="post_last",
        )(*args), None
    nxt_ada3, sh_idx, sc_idx = nxt
    in_specs += [_ada_spec(sh_idx, tm, rows_per_cond), _ada_spec(sc_idx, tm, rows_per_cond)]
    args += [nxt_ada3, nxt_ada3]
    return pl.pallas_call(
        _post_kernel, grid=(m // tm,), in_specs=in_specs, out_specs=(row, row),
        out_shape=(jax.ShapeDtypeStruct((m, D_MODEL), F32), jax.ShapeDtypeStruct((m, D_MODEL), BF16)),
        compiler_params=_cparams("parallel"), name="post",
    )(*args)


def _mm_bias_kernel(x_ref, w_ref, b_ref, o_ref):
    o_ref[...] = (_dot(x_ref[...], w_ref[...].astype(BF16)) + b_ref[...]).astype(o_ref.dtype)


def _mm_kernel(x_ref, w_ref, o_ref):
    o_ref[...] = _dot(x_ref[...], w_ref[...].astype(BF16)).astype(o_ref.dtype)


def _matmul(x, w, b, tm, tn, name, layer=None, single_buffer_x=False):
    m, k = x.shape
    n = w.shape[-1]
    tm = min(tm, m)
    x_mode = dict(pipeline_mode=pl.Buffered(1)) if single_buffer_x else {}
    if layer is None:
        w_spec = pl.BlockSpec((k, tn), lambda i, j: (0, j))
    else:
        w_spec = pl.BlockSpec((None, k, tn), lambda i, j: (layer, 0, j))
    in_specs = [pl.BlockSpec((tm, k), lambda i, j: (i, 0), **x_mode), w_spec]
    args = [x, w]
    body = _mm_kernel
    if b is not None:
        in_specs.append(pl.BlockSpec((1, tn), lambda i, j: (0, j)))
        args.append(b.reshape(1, n))
        body = _mm_bias_kernel
    return pl.pallas_call(
        body, grid=(m // tm, n // tn), in_specs=in_specs,
        out_specs=pl.BlockSpec((tm, tn), lambda i, j: (i, j)),
        out_shape=jax.ShapeDtypeStruct((m, n), F32),
        compiler_params=_cparams("parallel", "arbitrary"), name=name,
    )(*args)


def _pool_kernel(u_ref, w_ref, sc_ref, o_ref):
    seq = u_ref.shape[0]
    t = lax.broadcasted_iota(jnp.int32, (seq, seq), 0)
    s = lax.broadcasted_iota(jnp.int32, (seq, seq), 1)
    d = s - t
    tcol = lax.broadcasted_iota(jnp.int32, (seq, 1), 0)
    for gi, win in enumerate(POOL_WINDOWS):
        half = win // 2
        cols = slice(gi * POOL_GW, (gi + 1) * POOL_GW)
        band = jnp.where((d >= -half) & (d < half), 1.0, 0.0).astype(BF16)
        cnt = jnp.minimum(tcol + half, seq) - jnp.maximum(tcol - half, 0)
        u = u_ref[:, cols]
        u_hi = u.astype(BF16)
        u_lo = (u - u_hi.astype(F32)).astype(BF16)
        mean = (_dot(band, u_hi) + _dot(band, u_lo)) / cnt.astype(F32)
        y = _dot((mean - u).astype(BF16), w_ref[gi])
        o_ref[:, cols] = (y * sc_ref[:, cols]).astype(BF16)


def _pool(z, pool_w, pool_scale, nb, seq):
    return pl.pallas_call(
        _pool_kernel,
        grid=(nb,),
        in_specs=[pl.BlockSpec((seq, BRANCH_W), lambda b: (b, OFF_POOL // BRANCH_W)),
                  pl.BlockSpec((len(POOL_WINDOWS), POOL_GW, POOL_GW), lambda b: (0, 0, 0)),
                  pl.BlockSpec((1, BRANCH_W), lambda b: (0, 0))],
        out_specs=pl.BlockSpec((seq, BRANCH_W), lambda b: (b, 0)),
        out_shape=jax.ShapeDtypeStruct((nb * seq, BRANCH_W), BF16),
        compiler_params=_cparams("parallel"), name="pool",
    )(z, pool_w, pool_scale.reshape(1, BRANCH_W))


def _softmax_attend(q, k, v, sink):
    s = _dot_nt(q, k) * ATTN_SCALE
    m = jnp.max(s, axis=-1, keepdims=True)
    if sink is not None:
        m = jnp.maximum(m, sink)
    e = jnp.exp(s - m)
    den = jnp.sum(e, axis=-1, keepdims=True)
    if sink is not None:
        den = den + jnp.exp(sink - m)
    return _dot((e * (1.0 / den)).astype(BF16), v)


def _ctx_attn_kernel(sink_ref, qg_ref, kg_ref, vg_ref, qw_ref, kw_ref, vw_ref, qn_ref, kn_ref,
                     og_ref, ow_ref, gk_ref, gv_ref, wk_ref, wv_ref):
    kvh = pl.program_id(1)
    kg = _rms(kg_ref[...], kn_ref[...])
    gk_ref[...] = kg
    gv_ref[...] = vg_ref[...]
    wk_ref[...] = kw_ref[...]
    wv_ref[...] = vw_ref[...]
    kg_b, vg_b = kg.astype(BF16), vg_ref[...].astype(BF16)
    kw_b, vw_b = kw_ref[...].astype(BF16), vw_ref[...].astype(BF16)
    for h in range(N_GROUP):
        cols = slice(h * HEAD_DIM, (h + 1) * HEAD_DIM)
        qg = _rms(qg_ref[:, cols], qn_ref[...]).astype(BF16)
        og_ref[:, cols] = _softmax_attend(qg, kg_b, vg_b, None).astype(BF16)
        sink = sink_ref[kvh * N_GROUP + h]
        ow_ref[:, cols] = _softmax_attend(qw_ref[:, cols].astype(BF16), kw_b, vw_b, sink).astype(BF16)


def _ctx_attn(z, sink, qn_g, kn_g, nb, seq):
    q_spec = lambda off: pl.BlockSpec((seq, GQ_W), lambda b, k: (b, off // GQ_W + k))
    kv_spec = lambda off: pl.BlockSpec((seq, HEAD_DIM), lambda b, k: (b, off // HEAD_DIM + k))
    vec = pl.BlockSpec((1, HEAD_DIM), lambda b, k: (0, 0))
    o_spec = pl.BlockSpec((seq, GQ_W), lambda b, k: (b, k))
    c_spec = pl.BlockSpec((seq, HEAD_DIM), lambda b, k: (b, k))
    m = nb * seq
    return pl.pallas_call(
        _ctx_attn_kernel,
        grid=(nb, N_KV),
        in_specs=[pl.BlockSpec(memory_space=pltpu.SMEM),
                  q_spec(OFF_QG), kv_spec(OFF_KG), kv_spec(OFF_VG),
                  q_spec(OFF_QW), kv_spec(OFF_KW), kv_spec(OFF_VW), vec, vec],
        out_specs=(o_spec, o_spec, c_spec, c_spec, c_spec, c_spec),
        out_shape=(jax.ShapeDtypeStruct((m, Q_W), BF16), jax.ShapeDtypeStruct((m, Q_W), BF16))
        + tuple(jax.ShapeDtypeStruct((m, KV_W), F32) for _ in range(4)),
        compiler_params=_cparams("parallel", "parallel"), name="ctx_attn",
    )(sink, z, z, z, z, z, z, qn_g.reshape(1, HEAD_DIM), kn_g.reshape(1, HEAD_DIM))


def _rope(x, cos, sin, low_half):
    partner = jnp.where(low_half, pltpu.roll(x, HEAD_DIM - 32, axis=1), pltpu.roll(x, 32, axis=1))
    return x * cos + partner * sin


def _low_half(rows):
    lane = lax.broadcasted_iota(jnp.int32, (rows, HEAD_DIM), 1)
    return (lane & 63) < 32


def _lat_global_kernel(q_ref, k_ref, v_ref, ck_ref, cv_ref, cos_ref, sin_ref, qn_ref, kn_ref,
                       o_ref, kall, vall, *, tq):
    seq = q_ref.shape[0]
    past = ck_ref.shape[0]
    kall[0:past, :] = ck_ref[...].astype(BF16)
    vall[0:past, :] = cv_ref[...].astype(BF16)
    kn = _rope(_rms(k_ref[...], kn_ref[...]), cos_ref[...], sin_ref[...], _low_half(seq))
    kall[past:past + seq, :] = kn.astype(BF16)
    vall[past:past + seq, :] = v_ref[...].astype(BF16)
    low = _low_half(tq)

    def q_block(qb, carry):
        rows = pl.ds(pl.multiple_of(qb * tq, tq), tq)
        cos, sin = cos_ref[rows, :], sin_ref[rows, :]
        for h in range(N_GROUP):
            cols = slice(h * HEAD_DIM, (h + 1) * HEAD_DIM)
            q = _rope(_rms(q_ref[rows, cols], qn_ref[...]), cos, sin, low).astype(BF16)
            o_ref[rows, cols] = _softmax_attend(q, kall[...], vall[...], None).astype(BF16)
        return carry

    lax.fori_loop(0, seq // tq, q_block, 0)


def _lat_global(z, cache_k, cache_v, layer, cos, sin, qn_g, kn_g, nb, seq, tq=256):
    past = cache_k.shape[2]
    q_spec = pl.BlockSpec((seq, GQ_W), lambda b, k: (b, OFF_QG // GQ_W + k))
    kv_spec = lambda off: pl.BlockSpec((seq, HEAD_DIM), lambda b, k: (b, off // HEAD_DIM + k))
    c_spec = pl.BlockSpec((None, None, past, HEAD_DIM), lambda b, k: (b, layer, 0, k))
    tab = pl.BlockSpec((seq, HEAD_DIM), lambda b, k: (0, 0))
    vec = pl.BlockSpec((1, HEAD_DIM), lambda b, k: (0, 0))
    return pl.pallas_call(
        functools.partial(_lat_global_kernel, tq=tq),
        grid=(nb, N_KV),
        in_specs=[q_spec, kv_spec(OFF_KG), kv_spec(OFF_VG), c_spec, c_spec, tab, tab, vec, vec],
        out_specs=pl.BlockSpec((seq, GQ_W), lambda b, k: (b, k)),
        out_shape=jax.ShapeDtypeStruct((nb * seq, Q_W), BF16),
        scratch_shapes=[pltpu.VMEM((past + seq, HEAD_DIM), BF16), pltpu.VMEM((past + seq, HEAD_DIM), BF16)],
        compiler_params=_cparams("parallel", "parallel"), name="lat_global",
    )(z, z, z, cache_k, cache_v, cos, sin, qn_g.reshape(1, HEAD_DIM), kn_g.reshape(1, HEAD_DIM))


def _lat_window_kernel(sink_ref, q_ref, k_ref, v_ref, ck_ref, cv_ref, cos_ref, sin_ref,
                       o_ref, kpad, vpad):
    seq = q_ref.shape[0]
    blk = WINDOW
    kvh = pl.program_id(1)
    zeros = jnp.zeros((blk, HEAD_DIM), BF16)
    kpad[0:blk, :] = zeros
    vpad[0:blk, :] = zeros
    kpad[blk + seq:2 * blk + seq, :] = zeros
    vpad[blk + seq:2 * blk + seq, :] = zeros
    kpad[blk:blk + seq, :] = _rope(k_ref[...], cos_ref[...], sin_ref[...], _low_half(seq)).astype(BF16)
    vpad[blk:blk + seq, :] = v_ref[...].astype(BF16)
    ck = ck_ref[...].astype(BF16)
    cv = cv_ref[...].astype(BF16)
    low = _low_half(blk)
    rows4 = N_GROUP * blk
    head = lax.broadcasted_iota(jnp.int32, (rows4, 1), 0) // blk
    sink = jnp.zeros((rows4, 1), F32)
    for h in range(N_GROUP):
        sink = jnp.where(head == h, sink_ref[kvh * N_GROUP + h], sink)
    qi = lax.broadcasted_iota(jnp.int32, (rows4, 3 * blk), 0) & (blk - 1)
    kj = lax.broadcasted_iota(jnp.int32, (rows4, 3 * blk), 1)
    in_band = jnp.abs(qi + blk - kj) <= WINDOW

    def q_block(n, carry):
        r0 = pl.multiple_of(n * blk, blk)
        rows = pl.ds(r0, blk)
        cos, sin = cos_ref[rows, :], sin_ref[rows, :]
        q4 = jnp.concatenate(
            [_rope(q_ref[rows, h * HEAD_DIM:(h + 1) * HEAD_DIM], cos, sin, low) for h in range(N_GROUP)],
            axis=0).astype(BF16)
        tk = (n - 1) * blk + kj
        valid = in_band & (tk >= 0) & (tk < seq)
        s_loc = jnp.where(valid, _dot_nt(q4, kpad[pl.ds(r0, 3 * blk), :]) * ATTN_SCALE, NEG_INF)
        s_ctx = _dot_nt(q4, ck) * ATTN_SCALE
        m = jnp.maximum(jnp.maximum(jnp.max(s_loc, axis=-1, keepdims=True),
                                    jnp.max(s_ctx, axis=-1, keepdims=True)), sink)
        e_loc = jnp.exp(s_loc - m)
        e_ctx = jnp.exp(s_ctx - m)
        den = (jnp.sum(e_loc, axis=-1, keepdims=True) + jnp.sum(e_ctx, axis=-1, keepdims=True)
               + jnp.exp(sink - m))
        inv = 1.0 / den
        o4 = (_dot((e_loc * inv).astype(BF16), vpad[pl.ds(r0, 3 * blk), :])
              + _dot((e_ctx * inv).astype(BF16), cv))
        for h in range(N_GROUP):
            o_ref[rows, h * HEAD_DIM:(h + 1) * HEAD_DIM] = o4[h * blk:(h + 1) * blk].astype(BF16)
        return carry

    lax.fori_loop(0, seq // blk, q_block, 0)


def _lat_window(z, cache_k, cache_v, layer, cos, sin, sink, nb, seq):
    past = cache_k.shape[2]
    q_spec = pl.BlockSpec((seq, GQ_W), lambda b, k: (b, OFF_QW // GQ_W + k))
    kv_spec = lambda off: pl.BlockSpec((seq, HEAD_DIM), lambda b, k: (b, off // HEAD_DIM + k))
    c_spec = pl.BlockSpec((None, None, past, HEAD_DIM), lambda b, k: (b, layer, 0, k))
    tab = pl.BlockSpec((seq, HEAD_DIM), lambda b, k: (0, 0))
    return pl.pallas_call(
        _lat_window_kernel,
        grid=(nb, N_KV),
        in_specs=[pl.BlockSpec(memory_space=pltpu.SMEM), q_spec, kv_spec(OFF_KW), kv_spec(OFF_VW),
                  c_spec, c_spec, tab, tab],
        out_specs=pl.BlockSpec((seq, GQ_W), lambda b, k: (b, k)),
        out_shape=jax.ShapeDtypeStruct((nb * seq, Q_W), BF16),
        scratch_shapes=[pltpu.VMEM((seq + 2 * WINDOW, HEAD_DIM), BF16),
                        pltpu.VMEM((seq + 2 * WINDOW, HEAD_DIM), BF16)],
        compiler_params=_cparams("parallel", "parallel"), name="lat_window",
    )(sink, z, z, z, cache_k, cache_v, cos, sin)


def _lru_kernel(x_ref, y_ref, cw_ref, cb_ref, lam_ref, wa_ref, ba_ref, wx_ref, bx_ref, h0_ref,
                o_ref, st_ref, a_s, b_s, h_s):
    seq, width = x_ref.shape
    n_tiles = seq // SUBLANE
    x = x_ref[...]
    row = lax.broadcasted_iota(jnp.int32, (seq, width), 0)
    xc = (cb_ref[...]
          + jnp.where(row >= 2, pltpu.roll(x, 2, axis=0), 0.0) * cw_ref[0:1, :]
          + jnp.where(row >= 1, pltpu.roll(x, 1, axis=0), 0.0) * cw_ref[1:2, :]
          + x * cw_ref[2:3, :]
          + jnp.where(row < seq - 1, pltpu.roll(x, seq - 1, axis=0), 0.0) * cw_ref[3:4, :])
    xcb = xc.astype(BF16)
    in_tile = row & (SUBLANE - 1)

    def block_diag(w_ref, d):
        return jnp.concatenate(
            [_dot(xcb[:, j * LRU_BW:(j + 1) * LRU_BW], w_ref[d, j]) for j in range(width // LRU_BW)], axis=1)

    for d in range(2):
        backward = d == 1
        r = jax.nn.sigmoid(block_diag(wa_ref, d) + ba_ref[d:d + 1, :])
        gate_i = jax.nn.sigmoid(block_diag(wx_ref, d) + bx_ref[d:d + 1, :])
        neg_lam = -lam_ref[d:d + 1, :]
        softplus = jnp.maximum(neg_lam, 0.0) + jnp.log1p(jnp.exp(-jnp.abs(neg_lam)))
        log_a = -LRU_C * r * softplus
        a = jnp.exp(log_a)
        th = jnp.tanh(log_a)
        b = jnp.sqrt(-2.0 * th / (1.0 - th)) * (gate_i * xc)
        for s in (1, 2, 4):
            shift = seq - s if backward else s
            take = in_tile < SUBLANE - s if backward else in_tile >= s
            b = jnp.where(take, a * pltpu.roll(b, shift, axis=0) + b, b)
            a = jnp.where(take, a * pltpu.roll(a, shift, axis=0), a)
        a_s[...] = a
        b_s[...] = b

        def tile_step(j, h, backward=backward):
            t = n_tiles - 1 - j if backward else j
            rows = pl.ds(pl.multiple_of(t * SUBLANE, SUBLANE), SUBLANE)
            h8 = a_s[rows, :] * h + b_s[rows, :]
            if backward:
                h_s[rows, :] = h_s[rows, :] + h8
                edge = h8[0:1, :]
            else:
                h_s[rows, :] = h8
                edge = h8[SUBLANE - 1:SUBLANE, :]
            return jnp.broadcast_to(edge, (SUBLANE, width))

        h_end = lax.fori_loop(0, n_tiles, tile_step, jnp.broadcast_to(h0_ref[d:d + 1, :], (SUBLANE, width)))
        st_ref[d:d + 1, :] = h_end[0:1, :]
    o_ref[...] = (h_s[...] * jax.nn.gelu(y_ref[...])).astype(BF16)


def _lru(z, h0, h0_layer, conv_w, conv_b, lam, wa, ba, wx, bx, nb, seq, cw=256):
    nc = LRU_W // cw
    nd = cw // LRU_BW
    vec2 = pl.BlockSpec((2, cw), lambda b, c: (0, c))
    wspec = pl.BlockSpec((2, nd, LRU_BW, LRU_BW), lambda b, c: (0, c, 0, 0))
    return pl.pallas_call(
        _lru_kernel,
        grid=(nb, nc),
        in_specs=[pl.BlockSpec((seq, cw), lambda b, c: (b, OFF_XL // cw + c)),
                  pl.BlockSpec((seq, cw), lambda b, c: (b, OFF_YL // cw + c)),
                  pl.BlockSpec((4, cw), lambda b, c: (0, c)),
                  pl.BlockSpec((1, cw), lambda b, c: (0, c)),
                  vec2, wspec, vec2, wspec, vec2,
                  pl.BlockSpec((None, None, 2, cw), lambda b, c: (b, h0_layer, 0, c))],
        out_specs=(pl.BlockSpec((seq, cw), lambda b, c: (b, c)),
                   pl.BlockSpec((None, 2, cw), lambda b, c: (b, 0, c))),
        out_shape=(jax.ShapeDtypeStruct((nb * seq, LRU_W), BF16), jax.ShapeDtypeStruct((nb, 2, LRU_W), F32)),
        scratch_shapes=[pltpu.VMEM((seq, cw), F32)] * 3,
        compiler_params=_cparams("parallel", "parallel"), name="lru",
    )(z, z, conv_w, conv_b.reshape(1, LRU_W), lam, wa, ba, wx, bx, h0)


def _merge_kernel(b0, b1, b2, b3, w_ref, g0, g1, g2, g3, o_ref):
    acc = None
    for n, (br, gr) in enumerate(((b0, g0), (b1, g1), (b2, g2), (b3, g3))):
        t = jax.nn.sigmoid(gr[...]) * _dot(br[...], w_ref[n])
        acc = t if acc is None else acc + t
    o_ref[...] = acc.astype(BF16)


def _merge(branches, w_br, z, tm=1024, tn=256):
    m = z.shape[0]
    br = pl.BlockSpec((tm, BRANCH_W), lambda i, j: (i, 0))
    gate = lambda n: pl.BlockSpec((tm, tn), lambda i, j: (i, (OFF_GATE + n * D_MODEL) // tn + j))
    return pl.pallas_call(
        _merge_kernel,
        grid=(m // tm, D_MODEL // tn),
        in_specs=[br, br, br, br, pl.BlockSpec((N_BRANCH, BRANCH_W, tn), lambda i, j: (0, 0, j)),
                  gate(0), gate(1), gate(2), gate(3)],
        out_specs=pl.BlockSpec((tm, tn), lambda i, j: (i, j)),
        out_shape=jax.ShapeDtypeStruct((m, D_MODEL), BF16),
        compiler_params=_cparams("parallel", "arbitrary"), name="merge",
    )(*branches, w_br, z, z, z, z)


def _up_kernel(x_ref, wu_ref, wg_ref, cwu_ref, cwg_ref, cbu_ref, cbg_ref, o_ref, *, seq):
    rows, tn = o_ref.shape
    pos = lax.broadcasted_iota(jnp.int32, (rows, tn), 0) & (seq - 1)
    first, last = pos == 0, pos == seq - 1

    def conv(v, w_ref, b_ref):
        prev = jnp.where(first, 0.0, pltpu.roll(v, 1, axis=0))
        nxt = jnp.where(last, 0.0, pltpu.roll(v, rows - 1, axis=0))
        return b_ref[...] + prev * w_ref[0:1, :] + v * w_ref[1:2, :] + nxt * w_ref[2:3, :]

    x = x_ref[...]
    u = conv(_dot(x, wu_ref[...].astype(BF16)), cwu_ref, cbu_ref)
    g = conv(_dot(x, wg_ref[...].astype(BF16)), cwg_ref, cbg_ref)
    o_ref[...] = (g * jax.nn.sigmoid(g) * u).astype(BF16)


def _up(h, w_up, conv_w, conv_b, layer, seq, tm=1024, tn=256):
    m = h.shape[0]
    nj = D_FF // tn
    w_spec = lambda off: pl.BlockSpec((None, D_MODEL, tn), lambda i, j: (layer, 0, off + j))
    cw_spec = lambda off: pl.BlockSpec((None, 3, tn), lambda i, j: (layer, 0, off + j))
    cb_spec = lambda off: pl.BlockSpec((None, 1, tn), lambda i, j: (layer, 0, off + j))
    conv_b3 = conv_b.reshape(DEPTH, 1, 2 * D_FF)
    return pl.pallas_call(
        functools.partial(_up_kernel, seq=seq),
        grid=(m // tm, nj),
        in_specs=[pl.BlockSpec((tm, D_MODEL), lambda i, j: (i, 0)), w_spec(0), w_spec(nj),
                  cw_spec(0), cw_spec(nj), cb_spec(0), cb_spec(nj)],
        out_specs=pl.BlockSpec((tm, tn), lambda i, j: (i, j)),
        out_shape=jax.ShapeDtypeStruct((m, D_FF), BF16),
        compiler_params=_cparams("parallel", "arbitrary"), name="ffn_up",
    )(h, w_up, w_up, conv_w, conv_w, conv_b3, conv_b3)


def _rope_tables(seq):
    t = jnp.arange(seq)
    half = HEAD_DIM // 2
    inv = ROPE_THETA ** (-jnp.arange(0, half, 2, dtype=F32) / half)
    ang_r = (t // GRID_W).astype(F32)[:, None] * inv[None, :]
    ang_c = (t % GRID_W).astype(F32)[:, None] * inv[None, :]
    cos = jnp.concatenate([jnp.cos(ang_r), jnp.cos(ang_r), jnp.cos(ang_c), jnp.cos(ang_c)], axis=1)
    sin = jnp.concatenate([-jnp.sin(ang_r), jnp.sin(ang_r), -jnp.sin(ang_c), jnp.sin(ang_c)], axis=1)
    return cos, sin


def kernel(x_prompt, x_sample, cache_gk, cache_gv, cache_wk, cache_wv, state_lru, c, c_ctx, w_ada, b_ada, w_in, b_in, pool_w, pool_scale, qn_g, kn_g, sink, lru_conv_w, lru_conv_b, lru_lambda, lru_wa, lru_ba, lru_wx, lru_bx, w_br, w_o, ln1_g, ln1_b, w_up, ffn_conv_w, ffn_conv_b, w_down, ln2_g, ln2_b):
    nb_c, seq_c, _ = x_prompt.shape
    nb_s, seq_s, _ = x_sample.shape
    past = cache_gk.shape[2]

    c_all = jnp.concatenate([c_ctx[None, :], c, jnp.zeros((ADA_ROWS - 1 - nb_s, D_MODEL), F32)], axis=0)
    ada = _ada(c_all, w_ada, b_ada)
    ada3 = [ada[l].reshape(ADA_ROWS * 6, 1, D_MODEL) for l in range(DEPTH)]
    cos, sin = _rope_tables(seq_s)
    caches = [a.reshape(nb_s, DEPTH, past, KV_W) for a in (cache_gk, cache_gv, cache_wk, cache_wv)]
    zero_state = jnp.zeros((nb_c, 1, 2, LRU_W), F32)

    groups = [dict(x=x_prompt.reshape(nb_c * seq_c, D_MODEL), nb=nb_c, seq=seq_c, cond=None),
              dict(x=x_sample.reshape(nb_s * seq_s, D_MODEL), nb=nb_s, seq=seq_s, cond=seq_s)]
    for g in groups:
        g["h"] = _lnmod(g["x"], ada3[0], g["cond"])
    new_cache = [[] for _ in range(5)]

    for l in range(DEPTH):
        w_br_l = w_br[l].astype(BF16)
        w_down_l = w_down[l].astype(BF16)
        pool_w_l = pool_w[l].astype(BF16)
        wa_l = lru_wa[l].astype(BF16)
        wx_l = lru_wx[l].astype(BF16)
        for gi, g in enumerate(groups):
            nb, seq = g["nb"], g["seq"]
            z = _matmul(g["h"], w_in, b_in[l], 2048, 512, "in_proj", layer=l, single_buffer_x=True)
            o_pool = _pool(z, pool_w_l, pool_scale[l], nb, seq)
            if gi == 0:
                o_g, o_w, gk, gv, wk, wv = _ctx_attn(z, sink[l], qn_g[l], kn_g[l], nb, seq)
                h0, h0_layer = zero_state, 0
            else:
                o_g = _lat_global(z, caches[0], caches[1], l, cos, sin, qn_g[l], kn_g[l], nb, seq)
                o_w = _lat_window(z, caches[2], caches[3], l, cos, sin, sink[l], nb, seq)
                h0, h0_layer = state_lru, l
            o_lru, st = _lru(z, h0, h0_layer, lru_conv_w[l], lru_conv_b[l], lru_lambda[l],
                             wa_l, lru_ba[l], wx_l, lru_bx[l], nb, seq)
            if gi == 0:
                for lst, a in zip(new_cache, (gk, gv, wk, wv, st)):
                    lst.append(a)
            merged = _merge((o_pool, o_g, o_w, o_lru), w_br_l, z)
            mix = _matmul(merged, w_o, None, 2048, 512, "out_proj", layer=l, single_buffer_x=True)
            x1, h2 = _post(g["x"], mix, ada3[l], 2, ln1_g[l], ln1_b[l], g["cond"], nxt=(ada3[l], 3, 4))
            act = _up(h2, w_up, ffn_conv_w, ffn_conv_b, l, seq)
            ff = _matmul(act, w_down_l, None, 512, 512, "ffn_down")
            nxt = (ada3[l + 1], 0, 1) if l + 1 < DEPTH else None
            g["x"], g["h"] = _post(x1, ff, ada3[l], 5, ln2_g[l], ln2_b[l], g["cond"], nxt=nxt)

    y_prompt = groups[0]["x"].reshape(nb_c, seq_c, D_MODEL)
    y_sample = groups[1]["x"].reshape(nb_s, seq_s, D_MODEL)
    kv = [jnp.stack([a.reshape(nb_c, seq_c, N_KV, HEAD_DIM) for a in new_cache[i]], axis=1) for i in range(4)]
    new_lru = jnp.stack(new_cache[4], axis=1)
    return (y_prompt, y_sample, kv[0], kv[1], kv[2], kv[3], new_lru)
```

```python
import functools

import jax
import jax.numpy as jnp
from jax import lax
from jax.experimental import pallas as pl
from jax.experimental.pallas import tpu as pltpu

F32 = jnp.float32
BF16 = jnp.bfloat16

D_MODEL = 4096
DEPTH = 2
GRID_W = 64
BRANCH_W = D_MODEL // 4
N_BRANCH = 4
HEAD_DIM = 128
N_HEADS = BRANCH_W // HEAD_DIM
N_KV = N_HEADS // 4
N_GROUP = N_HEADS // N_KV
Q_W = N_HEADS * HEAD_DIM
KV_W = N_KV * HEAD_DIM
WINDOW = 128
ROPE_THETA = 10000.0
ATTN_SCALE = HEAD_DIM ** -0.5
NEG_INF = -1e30
POOL_WINDOWS = (2, 4, 8, 16)
POOL_GW = BRANCH_W // len(POOL_WINDOWS)
LRU_W = BRANCH_W
LRU_BW = 128
LRU_C = 8.0
D_FF = 11008
ALPHA = (2 * DEPTH) ** 0.25
IN_SIZES = (BRANCH_W, Q_W, KV_W, KV_W, Q_W, KV_W, KV_W, LRU_W, LRU_W, N_BRANCH * D_MODEL)
IN_W = sum(IN_SIZES)

OFF_POOL = 0
OFF_QG = OFF_POOL + BRANCH_W
OFF_KG = OFF_QG + Q_W
OFF_VG = OFF_KG + KV_W
OFF_QW = OFF_VG + KV_W
OFF_KW = OFF_QW + Q_W
OFF_VW = OFF_KW + KV_W
OFF_XL = OFF_VW + KV_W
OFF_YL = OFF_XL + LRU_W
OFF_GATE = OFF_YL + LRU_W

LANE = 128
SUBLANE = 8
ADA_ROWS = 16
GQ_W = N_GROUP * HEAD_DIM
VMEM_LIMIT = 56 << 20


def _cparams(*sem):
    return pltpu.CompilerParams(dimension_semantics=sem, vmem_limit_bytes=VMEM_LIMIT)


def _dot(a, b):
    return jnp.dot(a, b, preferred_element_type=F32)


def _dot_nt(a, b):
    return lax.dot_general(a, b, (((1,), (1,)), ((), ())), preferred_element_type=F32)


def _ln(x, eps=1e-6):
    mu = jnp.mean(x, axis=-1, keepdims=True)
    xc = x - mu
    var = jnp.mean(xc * xc, axis=-1, keepdims=True)
    return xc * lax.rsqrt(var + eps)


def _rms(x, g, eps=1e-6):
    return x * lax.rsqrt(jnp.mean(x * x, axis=-1, keepdims=True) + eps) * g


def _ada_kernel(c_ref, w_ref, b_ref, o_ref):
    c = c_ref[...]
    s = (c * jax.nn.sigmoid(c)).astype(BF16)
    o_ref[...] = _dot(s, w_ref[...].astype(BF16)) + b_ref[...]


def _ada(c_all, w_ada, b_ada, tn=512):
    n = w_ada.shape[-1]
    return pl.pallas_call(
        _ada_kernel,
        grid=(DEPTH, n // tn),
        in_specs=[pl.BlockSpec((ADA_ROWS, D_MODEL), lambda l, j: (0, 0)),
                  pl.BlockSpec((None, D_MODEL, tn), lambda l, j: (l, 0, j)),
                  pl.BlockSpec((None, 1, tn), lambda l, j: (l, 0, j))],
        out_specs=pl.BlockSpec((None, ADA_ROWS, tn), lambda l, j: (l, 0, j)),
        out_shape=jax.ShapeDtypeStruct((DEPTH, ADA_ROWS, n), F32),
        compiler_params=_cparams("parallel", "parallel"),
        name="ada",
    )(c_all, w_ada, b_ada.reshape(DEPTH, 1, n))


def _ada_spec(which, tm, rows_per_cond):
    if rows_per_cond is None:
        return pl.BlockSpec((None, 1, D_MODEL), lambda i: (which, 0, 0))
    return pl.BlockSpec((None, 1, D_MODEL), lambda i: ((1 + (i * tm) // rows_per_cond) * 6 + which, 0, 0))


def _lnmod_kernel(x_ref, sh_ref, sc_ref, h_ref):
    h_ref[...] = (_ln(x_ref[...]) * (1.0 + sc_ref[...]) + sh_ref[...]).astype(BF16)


def _lnmod(x, ada3, rows_per_cond, tm=256):
    m = x.shape[0]
    return pl.pallas_call(
        _lnmod_kernel,
        grid=(m // tm,),
        in_specs=[pl.BlockSpec((tm, D_MODEL), lambda i: (i, 0)),
                  _ada_spec(0, tm, rows_per_cond), _ada_spec(1, tm, rows_per_cond)],
        out_specs=pl.BlockSpec((tm, D_MODEL), lambda i: (i, 0)),
        out_shape=jax.ShapeDtypeStruct((m, D_MODEL), BF16),
        compiler_params=_cparams("parallel"),
        name="lnmod",
    )(x, ada3, ada3)


def _post_kernel(x_ref, y_ref, g_ref, lg_ref, lb_ref, sh_ref, sc_ref, xo_ref, h_ref, *, y_interleaved, h_interleaved):
    y = y_ref[...]
    if y_interleaved:
        y = jnp.swapaxes(y, 0, 1)
    xn = _ln(ALPHA * x_ref[...] + g_ref[...] * y) * lg_ref[...] + lb_ref[...]
    xo_ref[...] = xn
    if h_ref is not None:
        h = _ln(xn) * (1.0 + sc_ref[...]) + sh_ref[...]
        if h_interleaved:
            h = jnp.swapaxes(h, 0, 1)
        h_ref[...] = h.astype(BF16)


def _post_last_kernel(x_ref, y_ref, g_ref, lg_ref, lb_ref, xo_ref, *, y_interleaved):
    _post_kernel(x_ref, y_ref, g_ref, lg_ref, lb_ref, None, None, xo_ref, None,
                 y_interleaved=y_interleaved, h_interleaved=False)


POST_ROWS = 16


def _post(x, y, ada3, gate_idx, ln_g, ln_b, nb, seq, latent, nxt=None, y_interleaved=False, h_interleaved=False):
    nt = seq // SUBLANE
    nat_shape, int_shape = (nb, SUBLANE, nt, D_MODEL), (nb, nt, SUBLANE, D_MODEL)
    nat = pl.BlockSpec((None, SUBLANE, POST_ROWS, D_MODEL), lambda q, j: (q, 0, j, 0))
    itl = pl.BlockSpec((None, POST_ROWS, SUBLANE, D_MODEL), lambda q, j: (q, j, 0, 0))
    vec = pl.BlockSpec((1, D_MODEL), lambda q, j: (0, 0))
    ada = lambda which: pl.BlockSpec((None, 1, D_MODEL),
                                     lambda q, j: (((1 + q) * 6 if latent else 0) + which, 0, 0))
    in_specs = [nat, itl if y_interleaved else nat, ada(gate_idx), vec, vec]
    args = [x.reshape(nat_shape), y.reshape(int_shape if y_interleaved else nat_shape), ada3,
            ln_g.reshape(1, D_MODEL), ln_b.reshape(1, D_MODEL)]
    grid = (nb, nt // POST_ROWS)
    if nxt is None:
        xo = pl.pallas_call(
            functools.partial(_post_last_kernel, y_interleaved=y_interleaved),
            grid=grid, in_specs=in_specs, out_specs=nat,
            out_shape=jax.ShapeDtypeStruct(nat_shape, F32),
            compiler_params=_cparams("parallel", "parallel"), name="post_last",
        )(*args)
        return xo.reshape(nb * seq, D_MODEL), None
    nxt_ada3, sh_idx, sc_idx = nxt
    in_specs += [ada(sh_idx), ada(sc_idx)]
    args += [nxt_ada3, nxt_ada3]
    xo, h = pl.pallas_call(
        functools.partial(_post_kernel, y_interleaved=y_interleaved, h_interleaved=h_interleaved),
        grid=grid, in_specs=in_specs, out_specs=(nat, itl if h_interleaved else nat),
        out_shape=(jax.ShapeDtypeStruct(nat_shape, F32),
                   jax.ShapeDtypeStruct(int_shape if h_interleaved else nat_shape, BF16)),
        compiler_params=_cparams("parallel", "parallel"), name="post",
    )(*args)
    return xo.reshape(nb * seq, D_MODEL), h.reshape(nb * seq, D_MODEL)


def _mm_bias_kernel(x_ref, w_ref, b_ref, o_ref):
    o_ref[...] = (_dot(x_ref[...], w_ref[...].astype(BF16)) + b_ref[...]).astype(o_ref.dtype)


def _mm_kernel(x_ref, w_ref, o_ref):
    o_ref[...] = _dot(x_ref[...], w_ref[...].astype(BF16)).astype(o_ref.dtype)


def _matmul(x, w, b, tm, tn, name, layer=None, single_buffer_x=False):
    m, k = x.shape
    n = w.shape[-1]
    tm = min(tm, m)
    x_mode = dict(pipeline_mode=pl.Buffered(1)) if single_buffer_x else {}
    if layer is None:
        w_spec = pl.BlockSpec((k, tn), lambda i, j: (0, j))
    else:
        w_spec = pl.BlockSpec((None, k, tn), lambda i, j: (layer, 0, j))
    in_specs = [pl.BlockSpec((tm, k), lambda i, j: (i, 0), **x_mode), w_spec]
    args = [x, w]
    body = _mm_kernel
    if b is not None:
        in_specs.append(pl.BlockSpec((1, tn), lambda i, j: (0, j)))
        args.append(b.reshape(1, n))
        body = _mm_bias_kernel
    return pl.pallas_call(
        body, grid=(m // tm, n // tn), in_specs=in_specs,
        out_specs=pl.BlockSpec((tm, tn), lambda i, j: (i, j)),
        out_shape=jax.ShapeDtypeStruct((m, n), F32),
        compiler_params=_cparams("parallel", "arbitrary"), name=name,
    )(*args)


def _pool_kernel(u_ref, w_ref, sc_ref, o_ref):
    seq = u_ref.shape[0]
    t = lax.broadcasted_iota(jnp.int32, (seq, seq), 0)
    s = lax.broadcasted_iota(jnp.int32, (seq, seq), 1)
    d = s - t
    tcol = lax.broadcasted_iota(jnp.int32, (seq, 1), 0)
    for gi, win in enumerate(POOL_WINDOWS):
        half = win // 2
        cols = slice(gi * POOL_GW, (gi + 1) * POOL_GW)
        band = jnp.where((d >= -half) & (d < half), 1.0, 0.0).astype(BF16)
        cnt = jnp.minimum(tcol + half, seq) - jnp.maximum(tcol - half, 0)
        u = u_ref[:, cols]
        u_hi = u.astype(BF16)
        u_lo = (u - u_hi.astype(F32)).astype(BF16)
        mean = (_dot(band, u_hi) + _dot(band, u_lo)) / cnt.astype(F32)
        y = _dot((mean - u).astype(BF16), w_ref[gi])
        o_ref[:, cols] = (y * sc_ref[:, cols]).astype(BF16)


def _pool(z, pool_w, pool_scale, nb, seq):
    return pl.pallas_call(
        _pool_kernel,
        grid=(nb,),
        in_specs=[pl.BlockSpec((seq, BRANCH_W), lambda b: (b, OFF_POOL // BRANCH_W)),
                  pl.BlockSpec((len(POOL_WINDOWS), POOL_GW, POOL_GW), lambda b: (0, 0, 0)),
                  pl.BlockSpec((1, BRANCH_W), lambda b: (0, 0))],
        out_specs=pl.BlockSpec((seq, BRANCH_W), lambda b: (b, 0)),
        out_shape=jax.ShapeDtypeStruct((nb * seq, BRANCH_W), BF16),
        compiler_params=_cparams("parallel"), name="pool",
    )(z, pool_w, pool_scale.reshape(1, BRANCH_W))


def _softmax_attend(q, k, v, sink):
    s = _dot_nt(q, k) * ATTN_SCALE
    m = jnp.max(s, axis=-1, keepdims=True)
    if sink is not None:
        m = jnp.maximum(m, sink)
    e = jnp.exp(s - m)
    den = jnp.sum(e, axis=-1, keepdims=True)
    if sink is not None:
        den = den + jnp.exp(sink - m)
    return _dot(e.astype(BF16), v) * (1.0 / den)


def _ctx_attn_kernel(sink_ref, qg_ref, kg_ref, vg_ref, qw_ref, kw_ref, vw_ref, qn_ref, kn_ref,
                     og_ref, ow_ref, gk_ref, gv_ref, wk_ref, wv_ref):
    kvh = pl.program_id(1)
    kg = _rms(kg_ref[...], kn_ref[...])
    gk_ref[...] = kg
    gv_ref[...] = vg_ref[...]
    wk_ref[...] = kw_ref[...]
    wv_ref[...] = vw_ref[...]
    kg_b, vg_b = kg.astype(BF16), vg_ref[...].astype(BF16)
    kw_b, vw_b = kw_ref[...].astype(BF16), vw_ref[...].astype(BF16)
    for h in range(N_GROUP):
        cols = slice(h * HEAD_DIM, (h + 1) * HEAD_DIM)
        qg = _rms(qg_ref[:, cols], qn_ref[...]).astype(BF16)
        og_ref[:, cols] = _softmax_attend(qg, kg_b, vg_b, None).astype(BF16)
        sink = sink_ref[kvh * N_GROUP + h]
        ow_ref[:, cols] = _softmax_attend(qw_ref[:, cols].astype(BF16), kw_b, vw_b, sink).astype(BF16)


def _ctx_attn(z, sink, qn_g, kn_g, nb, seq):
    q_spec = lambda off: pl.BlockSpec((seq, GQ_W), lambda b, k: (b, off // GQ_W + k))
    kv_spec = lambda off: pl.BlockSpec((seq, HEAD_DIM), lambda b, k: (b, off // HEAD_DIM + k))
    vec = pl.BlockSpec((1, HEAD_DIM), lambda b, k: (0, 0))
    o_spec = pl.BlockSpec((seq, GQ_W), lambda b, k: (b, k))
    c_spec = pl.BlockSpec((seq, HEAD_DIM), lambda b, k: (b, k))
    m = nb * seq
    return pl.pallas_call(
        _ctx_attn_kernel,
        grid=(nb, N_KV),
        in_specs=[pl.BlockSpec(memory_space=pltpu.SMEM),
                  q_spec(OFF_QG), kv_spec(OFF_KG), kv_spec(OFF_VG),
                  q_spec(OFF_QW), kv_spec(OFF_KW), kv_spec(OFF_VW), vec, vec],
        out_specs=(o_spec, o_spec, c_spec, c_spec, c_spec, c_spec),
        out_shape=(jax.ShapeDtypeStruct((m, Q_W), BF16), jax.ShapeDtypeStruct((m, Q_W), BF16))
        + tuple(jax.ShapeDtypeStruct((m, KV_W), F32) for _ in range(4)),
        compiler_params=_cparams("parallel", "parallel"), name="ctx_attn",
    )(sink, z, z, z, z, z, z, qn_g.reshape(1, HEAD_DIM), kn_g.reshape(1, HEAD_DIM))


def _rope(x, cos, sin, low_half):
    partner = jnp.where(low_half, pltpu.roll(x, HEAD_DIM - 32, axis=1), pltpu.roll(x, 32, axis=1))
    return x * cos + partner * sin


def _low_half(rows):
    lane = lax.broadcasted_iota(jnp.int32, (rows, HEAD_DIM), 1)
    return (lane & 63) < 32


def _lat_global_kernel(q_ref, k_ref, v_ref, ck_ref, cv_ref, cos_ref, sin_ref, qn_ref, kn_ref,
                       o_ref, kall, vall, *, tq):
    seq = q_ref.shape[0]
    past = ck_ref.shape[0]
    kall[0:past, :] = ck_ref[...].astype(BF16)
    vall[0:past, :] = cv_ref[...].astype(BF16)
    kn = _rope(_rms(k_ref[...], kn_ref[...]), cos_ref[...], sin_ref[...], _low_half(seq))
    kall[past:past + seq, :] = kn.astype(BF16)
    vall[past:past + seq, :] = v_ref[...].astype(BF16)
    low = _low_half(tq)

    def q_block(qb, carry):
        rows = pl.ds(pl.multiple_of(qb * tq, tq), tq)
        cos, sin = cos_ref[rows, :], sin_ref[rows, :]
        for h in range(N_GROUP):
            cols = slice(h * HEAD_DIM, (h + 1) * HEAD_DIM)
            q = _rope(_rms(q_ref[rows, cols], qn_ref[...]), cos, sin, low).astype(BF16)
            o_ref[rows, cols] = _softmax_attend(q, kall[...], vall[...], None).astype(BF16)
        return carry

    lax.fori_loop(0, seq // tq, q_block, 0)


def _lat_global(z, cache_k, cache_v, layer, cos, sin, qn_g, kn_g, nb, seq, tq=256):
    past = cache_k.shape[2]
    q_spec = pl.BlockSpec((seq, GQ_W), lambda b, k: (b, OFF_QG // GQ_W + k))
    kv_spec = lambda off: pl.BlockSpec((seq, HEAD_DIM), lambda b, k: (b, off // HEAD_DIM + k))
    c_spec = pl.BlockSpec((None, None, past, HEAD_DIM), lambda b, k: (b, layer, 0, k))
    tab = pl.BlockSpec((seq, HEAD_DIM), lambda b, k: (0, 0))
    vec = pl.BlockSpec((1, HEAD_DIM), lambda b, k: (0, 0))
    return pl.pallas_call(
        functools.partial(_lat_global_kernel, tq=tq),
        grid=(nb, N_KV),
        in_specs=[q_spec, kv_spec(OFF_KG), kv_spec(OFF_VG), c_spec, c_spec, tab, tab, vec, vec],
        out_specs=pl.BlockSpec((seq, GQ_W), lambda b, k: (b, k)),
        out_shape=jax.ShapeDtypeStruct((nb * seq, Q_W), BF16),
        scratch_shapes=[pltpu.VMEM((past + seq, HEAD_DIM), BF16), pltpu.VMEM((past + seq, HEAD_DIM), BF16)],
        compiler_params=_cparams("parallel", "parallel"), name="lat_global",
    )(z, z, z, cache_k, cache_v, cos, sin, qn_g.reshape(1, HEAD_DIM), kn_g.reshape(1, HEAD_DIM))


def _lat_window_kernel(sink_ref, q_ref, k_ref, v_ref, ck_ref, cv_ref, cos_ref, sin_ref,
                       o_ref, kpad, vpad):
    seq = q_ref.shape[0]
    blk = WINDOW
    kvh = pl.program_id(1)
    zeros = jnp.zeros((blk, HEAD_DIM), BF16)
    kpad[0:blk, :] = zeros
    vpad[0:blk, :] = zeros
    kpad[blk + seq:2 * blk + seq, :] = zeros
    vpad[blk + seq:2 * blk + seq, :] = zeros
    kpad[blk:blk + seq, :] = _rope(k_ref[...], cos_ref[...], sin_ref[...], _low_half(seq)).astype(BF16)
    vpad[blk:blk + seq, :] = v_ref[...].astype(BF16)
    ck = ck_ref[...].astype(BF16)
    cv = cv_ref[...].astype(BF16)
    low = _low_half(blk)
    rows4 = N_GROUP * blk
    head = lax.broadcasted_iota(jnp.int32, (rows4, 1), 0) // blk
    sink = jnp.zeros((rows4, 1), F32)
    for h in range(N_GROUP):
        sink = jnp.where(head == h, sink_ref[kvh * N_GROUP + h], sink)
    qi = lax.broadcasted_iota(jnp.int32, (rows4, 3 * blk), 0) & (blk - 1)
    kj = lax.broadcasted_iota(jnp.int32, (rows4, 3 * blk), 1)
    in_band = jnp.abs(qi + blk - kj) <= WINDOW

    def q_block(n, carry):
        r0 = pl.multiple_of(n * blk, blk)
        rows = pl.ds(r0, blk)
        cos, sin = cos_ref[rows, :], sin_ref[rows, :]
        q4 = jnp.concatenate(
            [_rope(q_ref[rows, h * HEAD_DIM:(h + 1) * HEAD_DIM], cos, sin, low) for h in range(N_GROUP)],
            axis=0).astype(BF16)
        tk = (n - 1) * blk + kj
        valid = in_band & (tk >= 0) & (tk < seq)
        s_loc = jnp.where(valid, _dot_nt(q4, kpad[pl.ds(r0, 3 * blk), :]) * ATTN_SCALE, NEG_INF)
        s_ctx = _dot_nt(q4, ck) * ATTN_SCALE
        m = jnp.maximum(jnp.maximum(jnp.max(s_loc, axis=-1, keepdims=True),
                                    jnp.max(s_ctx, axis=-1, keepdims=True)), sink)
        e_loc = jnp.exp(s_loc - m)
        e_ctx = jnp.exp(s_ctx - m)
        den = (jnp.sum(e_loc, axis=-1, keepdims=True) + jnp.sum(e_ctx, axis=-1, keepdims=True)
               + jnp.exp(sink - m))
        o4 = (_dot(e_loc.astype(BF16), vpad[pl.ds(r0, 3 * blk), :]) + _dot(e_ctx.astype(BF16), cv)) * (1.0 / den)
        for h in range(N_GROUP):
            o_ref[rows, h * HEAD_DIM:(h + 1) * HEAD_DIM] = o4[h * blk:(h + 1) * blk].astype(BF16)
        return carry

    lax.fori_loop(0, seq // blk, q_block, 0)


def _lat_window(z, cache_k, cache_v, layer, cos, sin, sink, nb, seq):
    past = cache_k.shape[2]
    q_spec = pl.BlockSpec((seq, GQ_W), lambda b, k: (b, OFF_QW // GQ_W + k))
    kv_spec = lambda off: pl.BlockSpec((seq, HEAD_DIM), lambda b, k: (b, off // HEAD_DIM + k))
    c_spec = pl.BlockSpec((None, None, past, HEAD_DIM), lambda b, k: (b, layer, 0, k))
    tab = pl.BlockSpec((seq, HEAD_DIM), lambda b, k: (0, 0))
    return pl.pallas_call(
        _lat_window_kernel,
        grid=(nb, N_KV),
        in_specs=[pl.BlockSpec(memory_space=pltpu.SMEM), q_spec, kv_spec(OFF_KW), kv_spec(OFF_VW),
                  c_spec, c_spec, tab, tab],
        out_specs=pl.BlockSpec((seq, GQ_W), lambda b, k: (b, k)),
        out_shape=jax.ShapeDtypeStruct((nb * seq, Q_W), BF16),
        scratch_shapes=[pltpu.VMEM((seq + 2 * WINDOW, HEAD_DIM), BF16),
                        pltpu.VMEM((seq + 2 * WINDOW, HEAD_DIM), BF16)],
        compiler_params=_cparams("parallel", "parallel"), name="lat_window",
    )(sink, z, z, z, cache_k, cache_v, cos, sin)


def _lru_kernel(x_ref, y_ref, cw_ref, cb_ref, lam_ref, wa_ref, ba_ref, wx_ref, bx_ref, h0_ref,
                o_ref, st_ref, a_s, b_s, h_s):
    seq, width = x_ref.shape
    n_tiles = seq // SUBLANE
    x = x_ref[...]
    row = lax.broadcasted_iota(jnp.int32, (seq, width), 0)
    xc = (cb_ref[...]
          + jnp.where(row >= 2, pltpu.roll(x, 2, axis=0), 0.0) * cw_ref[0:1, :]
          + jnp.where(row >= 1, pltpu.roll(x, 1, axis=0), 0.0) * cw_ref[1:2, :]
          + x * cw_ref[2:3, :]
          + jnp.where(row < seq - 1, pltpu.roll(x, seq - 1, axis=0), 0.0) * cw_ref[3:4, :])
    xcb = xc.astype(BF16)
    in_tile = row & (SUBLANE - 1)

    def block_diag(w_ref, d):
        return jnp.concatenate(
            [_dot(xcb[:, j * LRU_BW:(j + 1) * LRU_BW], w_ref[d, j]) for j in range(width // LRU_BW)], axis=1)

    for d in range(2):
        backward = d == 1
        r = jax.nn.sigmoid(block_diag(wa_ref, d) + ba_ref[d:d + 1, :])
        gate_i = jax.nn.sigmoid(block_diag(wx_ref, d) + bx_ref[d:d + 1, :])
        neg_lam = -lam_ref[d:d + 1, :]
        softplus = jnp.maximum(neg_lam, 0.0) + jnp.log1p(jnp.exp(-jnp.abs(neg_lam)))
        log_a = -LRU_C * r * softplus
        a = jnp.exp(log_a)
        th = jnp.tanh(log_a)
        b = jnp.sqrt(-2.0 * th / (1.0 - th)) * (gate_i * xc)
        for s in (1, 2, 4):
            shift = seq - s if backward else s
            take = in_tile < SUBLANE - s if backward else in_tile >= s
            b = jnp.where(take, a * pltpu.roll(b, shift, axis=0) + b, b)
            a = jnp.where(take, a * pltpu.roll(a, shift, axis=0), a)
        a_s[...] = a
        b_s[...] = b

        def tile_step(j, h, backward=backward):
            t = n_tiles - 1 - j if backward else j
            rows = pl.ds(pl.multiple_of(t * SUBLANE, SUBLANE), SUBLANE)
            h8 = a_s[rows, :] * h + b_s[rows, :]
            if backward:
                h_s[rows, :] = h_s[rows, :] + h8
                edge = h8[0:1, :]
            else:
                h_s[rows, :] = h8
                edge = h8[SUBLANE - 1:SUBLANE, :]
            return jnp.broadcast_to(edge, (SUBLANE, width))

        h_end = lax.fori_loop(0, n_tiles, tile_step, jnp.broadcast_to(h0_ref[d:d + 1, :], (SUBLANE, width)))
        st_ref[d:d + 1, :] = h_end[0:1, :]
    o_ref[...] = (h_s[...] * jax.nn.gelu(y_ref[...])).astype(BF16)


def _lru(z, h0, h0_layer, conv_w, conv_b, lam, wa, ba, wx, bx, nb, seq, cw=256):
    nc = LRU_W // cw
    nd = cw // LRU_BW
    vec2 = pl.BlockSpec((2, cw), lambda b, c: (0, c))
    wspec = pl.BlockSpec((2, nd, LRU_BW, LRU_BW), lambda b, c: (0, c, 0, 0))
    return pl.pallas_call(
        _lru_kernel,
        grid=(nb, nc),
        in_specs=[pl.BlockSpec((seq, cw), lambda b, c: (b, OFF_XL // cw + c)),
                  pl.BlockSpec((seq, cw), lambda b, c: (b, OFF_YL // cw + c)),
                  pl.BlockSpec((4, cw), lambda b, c: (0, c)),
                  pl.BlockSpec((1, cw), lambda b, c: (0, c)),
                  vec2, wspec, vec2, wspec, vec2,
                  pl.BlockSpec((None, None, 2, cw), lambda b, c: (b, h0_layer, 0, c))],
        out_specs=(pl.BlockSpec((seq, cw), lambda b, c: (b, c)),
                   pl.BlockSpec((None, 2, cw), lambda b, c: (b, 0, c))),
        out_shape=(jax.ShapeDtypeStruct((nb * seq, LRU_W), BF16), jax.ShapeDtypeStruct((nb, 2, LRU_W), F32)),
        scratch_shapes=[pltpu.VMEM((seq, cw), F32)] * 3,
        compiler_params=_cparams("parallel", "parallel"), name="lru",
    )(z, z, conv_w, conv_b.reshape(1, LRU_W), lam, wa, ba, wx, bx, h0)


def _merge_kernel(b0, b1, b2, b3, w_ref, g0, g1, g2, g3, o_ref):
    acc = None
    for n, (br, gr) in enumerate(((b0, g0), (b1, g1), (b2, g2), (b3, g3))):
        t = jax.nn.sigmoid(gr[...]) * _dot(br[...], w_ref[n])
        acc = t if acc is None else acc + t
    o_ref[...] = acc.astype(BF16)


def _merge(branches, w_br, z, tm=1024, tn=256):
    m = z.shape[0]
    br = pl.BlockSpec((tm, BRANCH_W), lambda i, j: (i, 0))
    gate = lambda n: pl.BlockSpec((tm, tn), lambda i, j: (i, (OFF_GATE + n * D_MODEL) // tn + j))
    return pl.pallas_call(
        _merge_kernel,
        grid=(m // tm, D_MODEL // tn),
        in_specs=[br, br, br, br, pl.BlockSpec((N_BRANCH, BRANCH_W, tn), lambda i, j: (0, 0, j)),
                  gate(0), gate(1), gate(2), gate(3)],
        out_specs=pl.BlockSpec((tm, tn), lambda i, j: (i, j)),
        out_shape=jax.ShapeDtypeStruct((m, D_MODEL), BF16),
        compiler_params=_cparams("parallel", "arbitrary"), name="merge",
    )(*branches, w_br, z, z, z, z)


def _up_kernel(x_ref, wu_ref, wg_ref, cwu_ref, cwg_ref, cbu_ref, cbg_ref, o_ref, *, seq):
    rows, tn = o_ref.shape
    sub = lax.broadcasted_iota(jnp.int32, (SUBLANE, tn), 0)

    def conv(v, w_ref, b_ref):
        prev, nxt = [], []
        for q in range(rows // seq):
            vq = v[q * seq:(q + 1) * seq]
            head, tail = vq[:SUBLANE], vq[seq - SUBLANE:]
            prev += [jnp.where(sub >= 1, pltpu.roll(tail, 1, axis=0), 0.0), vq[:seq - SUBLANE]]
            nxt += [vq[SUBLANE:], jnp.where(sub < SUBLANE - 1, pltpu.roll(head, SUBLANE - 1, axis=0), 0.0)]
        prev, nxt = jnp.concatenate(prev, axis=0), jnp.concatenate(nxt, axis=0)
        return b_ref[...] + prev * w_ref[0:1, :] + v * w_ref[1:2, :] + nxt * w_ref[2:3, :]

    x = x_ref[...]
    u = conv(_dot(x, wu_ref[...].astype(BF16)), cwu_ref, cbu_ref)
    g = conv(_dot(x, wg_ref[...].astype(BF16)), cwg_ref, cbg_ref)
    o_ref[...] = (g * jax.nn.sigmoid(g) * u).astype(BF16)


def _up(h, w_up, conv_w, conv_b, layer, seq, tm=1024, tn=256):
    m = h.shape[0]
    nj = D_FF // tn
    w_spec = lambda off: pl.BlockSpec((None, D_MODEL, tn), lambda i, j: (layer, 0, off + j))
    cw_spec = lambda off: pl.BlockSpec((None, 3, tn), lambda i, j: (layer, 0, off + j))
    cb_spec = lambda off: pl.BlockSpec((None, 1, tn), lambda i, j: (layer, 0, off + j))
    conv_b3 = conv_b.reshape(DEPTH, 1, 2 * D_FF)
    return pl.pallas_call(
        functools.partial(_up_kernel, seq=seq),
        grid=(m // tm, nj),
        in_specs=[pl.BlockSpec((tm, D_MODEL), lambda i, j: (i, 0)), w_spec(0), w_spec(nj),
                  cw_spec(0), cw_spec(nj), cb_spec(0), cb_spec(nj)],
        out_specs=pl.BlockSpec((tm, tn), lambda i, j: (i, j)),
        out_shape=jax.ShapeDtypeStruct((m, D_FF), BF16),
        compiler_params=_cparams("parallel", "arbitrary"), name="ffn_up",
    )(h, w_up, w_up, conv_w, conv_w, conv_b3, conv_b3)


def _rope_tables(seq):
    t = jnp.arange(seq)
    half = HEAD_DIM // 2
    inv = ROPE_THETA ** (-jnp.arange(0, half, 2, dtype=F32) / half)
    ang_r = (t // GRID_W).astype(F32)[:, None] * inv[None, :]
    ang_c = (t % GRID_W).astype(F32)[:, None] * inv[None, :]
    cos = jnp.concatenate([jnp.cos(ang_r), jnp.cos(ang_r), jnp.cos(ang_c), jnp.cos(ang_c)], axis=1)
    sin = jnp.concatenate([-jnp.sin(ang_r), jnp.sin(ang_r), -jnp.sin(ang_c), jnp.sin(ang_c)], axis=1)
    return cos, sin


def kernel(x_prompt, x_sample, cache_gk, cache_gv, cache_wk, cache_wv, state_lru, c, c_ctx, w_ada, b_ada, w_in, b_in, pool_w, pool_scale, qn_g, kn_g, sink, lru_conv_w, lru_conv_b, lru_lambda, lru_wa, lru_ba, lru_wx, lru_bx, w_br, w_o, ln1_g, ln1_b, w_up, ffn_conv_w, ffn_conv_b, w_down, ln2_g, ln2_b):
    nb_c, seq_c, _ = x_prompt.shape
    nb_s, seq_s, _ = x_sample.shape
    past = cache_gk.shape[2]

    c_all = jnp.concatenate([c_ctx[None, :], c, jnp.zeros((ADA_ROWS - 1 - nb_s, D_MODEL), F32)], axis=0)
    ada = _ada(c_all, w_ada, b_ada)
    ada3 = [ada[l].reshape(ADA_ROWS * 6, 1, D_MODEL) for l in range(DEPTH)]
    cos, sin = _rope_tables(seq_s)
    caches = [a.reshape(nb_s, DEPTH, past, KV_W) for a in (cache_gk, cache_gv, cache_wk, cache_wv)]
    zero_state = jnp.zeros((nb_c, 1, 2, LRU_W), F32)

    groups = [dict(x=x_prompt.reshape(nb_c * seq_c, D_MODEL), nb=nb_c, seq=seq_c, cond=None),
              dict(x=x_sample.reshape(nb_s * seq_s, D_MODEL), nb=nb_s, seq=seq_s, cond=seq_s)]
    for g in groups:
        g["h"] = _lnmod(g["x"], ada3[0], g["cond"])
    new_cache = [[] for _ in range(5)]

    for l in range(DEPTH):
        w_br_l = w_br[l].astype(BF16)
        w_down_l = w_down[l].astype(BF16)
        pool_w_l = pool_w[l].astype(BF16)
        wa_l = lru_wa[l].astype(BF16)
        wx_l = lru_wx[l].astype(BF16)
        for gi, g in enumerate(groups):
            nb, seq = g["nb"], g["seq"]
            z = _matmul(g["h"], w_in, b_in[l], 2048, 512, "in_proj", layer=l, single_buffer_x=True)
            o_pool = _pool(z, pool_w_l, pool_scale[l], nb, seq)
            if gi == 0:
                o_g, o_w, gk, gv, wk, wv = _ctx_attn(z, sink[l], qn_g[l], kn_g[l], nb, seq)
                h0, h0_layer = zero_state, 0
            else:
                o_g = _lat_global(z, caches[0], caches[1], l, cos, sin, qn_g[l], kn_g[l], nb, seq)
                o_w = _lat_window(z, caches[2], caches[3], l, cos, sin, sink[l], nb, seq)
                h0, h0_layer = state_lru, l
            o_lru, st = _lru(z, h0, h0_layer, lru_conv_w[l], lru_conv_b[l], lru_lambda[l],
                             wa_l, lru_ba[l], wx_l, lru_bx[l], nb, seq)
            if gi == 0:
                for lst, a in zip(new_cache, (gk, gv, wk, wv, st)):
                    lst.append(a)
            merged = _merge((o_pool, o_g, o_w, o_lru), w_br_l, z)
            mix = _matmul(merged, w_o, None, 2048, 512, "out_proj", layer=l, single_buffer_x=True)
            latent = gi == 1
            x1, h2 = _post(g["x"], mix, ada3[l], 2, ln1_g[l], ln1_b[l], nb, seq, latent,
                           nxt=(ada3[l], 3, 4), h_interleaved=True)
            act = _up(h2, w_up, ffn_conv_w, ffn_conv_b, l, seq)
            ff = _matmul(act, w_down_l, None, 512, 512, "ffn_down")
            nxt = (ada3[l + 1], 0, 1) if l + 1 < DEPTH else None
            g["x"], g["h"] = _post(x1, ff, ada3[l], 5, ln2_g[l], ln2_b[l], nb, seq, latent,
                                   nxt=nxt, y_interleaved=True)

    y_prompt = groups[0]["x"].reshape(nb_c, seq_c, D_MODEL)
    y_sample = groups[1]["x"].reshape(nb_s, seq_s, D_MODEL)
    kv = [jnp.stack([a.reshape(nb_c, seq_c, N_KV, HEAD_DIM) for a in new_cache[i]], axis=1) for i in range(4)]
    new_lru = jnp.stack(new_cache[4], axis=1)
    return (y_prompt, y_sample, kv[0], kv[1], kv[2], kv[3], new_lru)
```

```python
import functools

import jax
import jax.numpy as jnp
from jax import lax
from jax.experimental import pallas as pl
from jax.experimental.pallas import tpu as pltpu

F32 = jnp.float32
BF16 = jnp.bfloat16

D_MODEL = 4096
DEPTH = 2
GRID_W = 64
BRANCH_W = D_MODEL // 4
N_BRANCH = 4
HEAD_DIM = 128
N_HEADS = BRANCH_W // HEAD_DIM
N_KV = N_HEADS // 4
N_GROUP = N_HEADS // N_KV
Q_W = N_HEADS * HEAD_DIM
KV_W = N_KV * HEAD_DIM
WINDOW = 128
ROPE_THETA = 10000.0
ATTN_SCALE = HEAD_DIM ** -0.5
NEG_INF = -1e30
POOL_WINDOWS = (2, 4, 8, 16)
POOL_GW = BRANCH_W // len(POOL_WINDOWS)
LRU_W = BRANCH_W
LRU_BW = 128
LRU_C = 8.0
D_FF = 11008
ALPHA = (2 * DEPTH) ** 0.25
IN_SIZES = (BRANCH_W, Q_W, KV_W, KV_W, Q_W, KV_W, KV_W, LRU_W, LRU_W, N_BRANCH * D_MODEL)
IN_W = sum(IN_SIZES)

OFF_POOL = 0
OFF_QG = OFF_POOL + BRANCH_W
OFF_KG = OFF_QG + Q_W
OFF_VG = OFF_KG + KV_W
OFF_QW = OFF_VG + KV_W
OFF_KW = OFF_QW + Q_W
OFF_VW = OFF_KW + KV_W
OFF_XL = OFF_VW + KV_W
OFF_YL = OFF_XL + LRU_W
OFF_GATE = OFF_YL + LRU_W

LANE = 128
SUBLANE = 8
ADA_ROWS = 16
GQ_W = N_GROUP * HEAD_DIM
VMEM_LIMIT = 56 << 20


def _cparams(*sem):
    return pltpu.CompilerParams(dimension_semantics=sem, vmem_limit_bytes=VMEM_LIMIT)


def _dot(a, b):
    return jnp.dot(a, b, preferred_element_type=F32)


def _dot_nt(a, b):
    return lax.dot_general(a, b, (((1,), (1,)), ((), ())), preferred_element_type=F32)


def _ln(x, eps=1e-6):
    mu = jnp.mean(x, axis=-1, keepdims=True)
    xc = x - mu
    var = jnp.mean(xc * xc, axis=-1, keepdims=True)
    return xc * lax.rsqrt(var + eps)


def _rms(x, g, eps=1e-6):
    return x * lax.rsqrt(jnp.mean(x * x, axis=-1, keepdims=True) + eps) * g


def _ada_kernel(c_ref, w_ref, b_ref, o_ref):
    c = c_ref[...]
    s = (c * jax.nn.sigmoid(c)).astype(BF16)
    o_ref[...] = _dot(s, w_ref[...].astype(BF16)) + b_ref[...]


def _ada(c_all, w_ada, b_ada, tn=512):
    n = w_ada.shape[-1]
    return pl.pallas_call(
        _ada_kernel,
        grid=(DEPTH, n // tn),
        in_specs=[pl.BlockSpec((ADA_ROWS, D_MODEL), lambda l, j: (0, 0)),
                  pl.BlockSpec((None, D_MODEL, tn), lambda l, j: (l, 0, j)),
                  pl.BlockSpec((None, 1, tn), lambda l, j: (l, 0, j))],
        out_specs=pl.BlockSpec((None, ADA_ROWS, tn), lambda l, j: (l, 0, j)),
        out_shape=jax.ShapeDtypeStruct((DEPTH, ADA_ROWS, n), F32),
        compiler_params=_cparams("parallel", "parallel"),
        name="ada",
    )(c_all, w_ada, b_ada.reshape(DEPTH, 1, n))


def _ada_spec(which, tm, rows_per_cond):
    if rows_per_cond is None:
        return pl.BlockSpec((None, 1, D_MODEL), lambda i: (which, 0, 0))
    return pl.BlockSpec((None, 1, D_MODEL), lambda i: ((1 + (i * tm) // rows_per_cond) * 6 + which, 0, 0))


def _lnmod_kernel(x_ref, sh_ref, sc_ref, h_ref):
    h_ref[...] = (_ln(x_ref[...]) * (1.0 + sc_ref[...]) + sh_ref[...]).astype(BF16)


def _lnmod(x, ada3, rows_per_cond, tm=256):
    m = x.shape[0]
    return pl.pallas_call(
        _lnmod_kernel,
        grid=(m // tm,),
        in_specs=[pl.BlockSpec((tm, D_MODEL), lambda i: (i, 0)),
                  _ada_spec(0, tm, rows_per_cond), _ada_spec(1, tm, rows_per_cond)],
        out_specs=pl.BlockSpec((tm, D_MODEL), lambda i: (i, 0)),
        out_shape=jax.ShapeDtypeStruct((m, D_MODEL), BF16),
        compiler_params=_cparams("parallel"),
        name="lnmod",
    )(x, ada3, ada3)


def _post_kernel(x_ref, y_ref, g_ref, lg_ref, lb_ref, sh_ref, sc_ref, xo_ref, h_ref, *, y_interleaved, h_interleaved):
    y = y_ref[...]
    if y_interleaved:
        y = jnp.swapaxes(y, 0, 1)
    xn = _ln(ALPHA * x_ref[...] + g_ref[...] * y) * lg_ref[...] + lb_ref[...]
    xo_ref[...] = xn
    if h_ref is not None:
        h = _ln(xn) * (1.0 + sc_ref[...]) + sh_ref[...]
        if h_interleaved:
            h = jnp.swapaxes(h, 0, 1)
        h_ref[...] = h.astype(BF16)


def _post_last_kernel(x_ref, y_ref, g_ref, lg_ref, lb_ref, xo_ref, *, y_interleaved):
    _post_kernel(x_ref, y_ref, g_ref, lg_ref, lb_ref, None, None, xo_ref, None,
                 y_interleaved=y_interleaved, h_interleaved=False)


POST_ROWS = 32


def _post(x, y, ada3, gate_idx, ln_g, ln_b, nb, seq, latent, nxt=None, y_interleaved=False, h_interleaved=False):
    nt = seq // SUBLANE
    nat_shape, int_shape = (nb, SUBLANE, nt, D_MODEL), (nb, nt, SUBLANE, D_MODEL)
    nat = pl.BlockSpec((None, SUBLANE, POST_ROWS, D_MODEL), lambda q, j: (q, 0, j, 0))
    itl = pl.BlockSpec((None, POST_ROWS, SUBLANE, D_MODEL), lambda q, j: (q, j, 0, 0))
    vec = pl.BlockSpec((1, D_MODEL), lambda q, j: (0, 0))
    ada = lambda which: pl.BlockSpec((None, 1, D_MODEL),
                                     lambda q, j: (((1 + q) * 6 if latent else 0) + which, 0, 0))
    in_specs = [nat, itl if y_interleaved else nat, ada(gate_idx), vec, vec]
    args = [x.reshape(nat_shape), y.reshape(int_shape if y_interleaved else nat_shape), ada3,
            ln_g.reshape(1, D_MODEL), ln_b.reshape(1, D_MODEL)]
    grid = (nb, nt // POST_ROWS)
    if nxt is None:
        xo = pl.pallas_call(
            functools.partial(_post_last_kernel, y_interleaved=y_interleaved),
            grid=grid, in_specs=in_specs, out_specs=nat,
            out_shape=jax.ShapeDtypeStruct(nat_shape, F32),
            compiler_params=_cparams("parallel", "parallel"), name="post_last",
        )(*args)
        return xo.reshape(nb * seq, D_MODEL), None
    nxt_ada3, sh_idx, sc_idx = nxt
    in_specs += [ada(sh_idx), ada(sc_idx)]
    args += [nxt_ada3, nxt_ada3]
    xo, h = pl.pallas_call(
        functools.partial(_post_kernel, y_interleaved=y_interleaved, h_interleaved=h_interleaved),
        grid=grid, in_specs=in_specs, out_specs=(nat, itl if h_interleaved else nat),
        out_shape=(jax.ShapeDtypeStruct(nat_shape, F32),
                   jax.ShapeDtypeStruct(int_shape if h_interleaved else nat_shape, BF16)),
        compiler_params=_cparams("parallel", "parallel"), name="post",
    )(*args)
    return xo.reshape(nb * seq, D_MODEL), h.reshape(nb * seq, D_MODEL)


def _mm_bias_kernel(x_ref, w_ref, b_ref, o_ref):
    o_ref[...] = (_dot(x_ref[...], w_ref[...].astype(BF16)) + b_ref[...]).astype(o_ref.dtype)


def _mm_kernel(x_ref, w_ref, o_ref):
    o_ref[...] = _dot(x_ref[...], w_ref[...].astype(BF16)).astype(o_ref.dtype)


def _matmul(x, w, b, tm, tn, name, layer=None, single_buffer_x=False):
    m, k = x.shape
    n = w.shape[-1]
    tm = min(tm, m)
    x_mode = dict(pipeline_mode=pl.Buffered(1)) if single_buffer_x else {}
    if layer is None:
        w_spec = pl.BlockSpec((k, tn), lambda i, j: (0, j))
    else:
        w_spec = pl.BlockSpec((None, k, tn), lambda i, j: (layer, 0, j))
    in_specs = [pl.BlockSpec((tm, k), lambda i, j: (i, 0), **x_mode), w_spec]
    args = [x, w]
    body = _mm_kernel
    if b is not None:
        in_specs.append(pl.BlockSpec((1, tn), lambda i, j: (0, j)))
        args.append(b.reshape(1, n))
        body = _mm_bias_kernel
    return pl.pallas_call(
        body, grid=(m // tm, n // tn), in_specs=in_specs,
        out_specs=pl.BlockSpec((tm, tn), lambda i, j: (i, j)),
        out_shape=jax.ShapeDtypeStruct((m, n), F32),
        compiler_params=_cparams("parallel", "arbitrary"), name=name,
    )(*args)


def _pool_kernel(u_ref, w_ref, sc_ref, o_ref):
    seq = u_ref.shape[0]
    t = lax.broadcasted_iota(jnp.int32, (seq, seq), 0)
    s = lax.broadcasted_iota(jnp.int32, (seq, seq), 1)
    d = s - t
    tcol = lax.broadcasted_iota(jnp.int32, (seq, 1), 0)
    for gi, win in enumerate(POOL_WINDOWS):
        half = win // 2
        cols = slice(gi * POOL_GW, (gi + 1) * POOL_GW)
        band = jnp.where((d >= -half) & (d < half), 1.0, 0.0).astype(BF16)
        cnt = jnp.minimum(tcol + half, seq) - jnp.maximum(tcol - half, 0)
        u = u_ref[:, cols]
        u_hi = u.astype(BF16)
        u_lo = (u - u_hi.astype(F32)).astype(BF16)
        mean = (_dot(band, u_hi) + _dot(band, u_lo)) / cnt.astype(F32)
        y = _dot((mean - u).astype(BF16), w_ref[gi])
        o_ref[:, cols] = (y * sc_ref[:, cols]).astype(BF16)


def _pool(z, pool_w, pool_scale, nb, seq):
    return pl.pallas_call(
        _pool_kernel,
        grid=(nb,),
        in_specs=[pl.BlockSpec((seq, BRANCH_W), lambda b: (b, OFF_POOL // BRANCH_W)),
                  pl.BlockSpec((len(POOL_WINDOWS), POOL_GW, POOL_GW), lambda b: (0, 0, 0)),
                  pl.BlockSpec((1, BRANCH_W), lambda b: (0, 0))],
        out_specs=pl.BlockSpec((seq, BRANCH_W), lambda b: (b, 0)),
        out_shape=jax.ShapeDtypeStruct((nb * seq, BRANCH_W), BF16),
        compiler_params=_cparams("parallel"), name="pool",
    )(z, pool_w, pool_scale.reshape(1, BRANCH_W))


def _softmax_attend(q, k, v, sink):
    s = _dot_nt(q, k) * ATTN_SCALE
    m = jnp.max(s, axis=-1, keepdims=True)
    if sink is not None:
        m = jnp.maximum(m, sink)
    e = jnp.exp(s - m)
    den = jnp.sum(e, axis=-1, keepdims=True)
    if sink is not None:
        den = den + jnp.exp(sink - m)
    return _dot(e.astype(BF16), v) * (1.0 / den)


def _ctx_attn_kernel(sink_ref, qg_ref, kg_ref, vg_ref, qw_ref, kw_ref, vw_ref, qn_ref, kn_ref,
                     og_ref, ow_ref, gk_ref, gv_ref, wk_ref, wv_ref):
    kvh = pl.program_id(1)
    kg = _rms(kg_ref[...], kn_ref[...])
    gk_ref[...] = kg
    gv_ref[...] = vg_ref[...]
    wk_ref[...] = kw_ref[...]
    wv_ref[...] = vw_ref[...]
    kg_b, vg_b = kg.astype(BF16), vg_ref[...].astype(BF16)
    kw_b, vw_b = kw_ref[...].astype(BF16), vw_ref[...].astype(BF16)
    for h in range(N_GROUP):
        cols = slice(h * HEAD_DIM, (h + 1) * HEAD_DIM)
        qg = _rms(qg_ref[:, cols], qn_ref[...]).astype(BF16)
        og_ref[:, cols] = _softmax_attend(qg, kg_b, vg_b, None).astype(BF16)
        sink = sink_ref[kvh * N_GROUP + h]
        ow_ref[:, cols] = _softmax_attend(qw_ref[:, cols].astype(BF16), kw_b, vw_b, sink).astype(BF16)


def _ctx_attn(z, sink, qn_g, kn_g, nb, seq):
    q_spec = lambda off: pl.BlockSpec((seq, GQ_W), lambda b, k: (b, off // GQ_W + k))
    kv_spec = lambda off: pl.BlockSpec((seq, HEAD_DIM), lambda b, k: (b, off // HEAD_DIM + k))
    vec = pl.BlockSpec((1, HEAD_DIM), lambda b, k: (0, 0))
    o_spec = pl.BlockSpec((seq, GQ_W), lambda b, k: (b, k))
    c_spec = pl.BlockSpec((seq, HEAD_DIM), lambda b, k: (b, k))
    m = nb * seq
    return pl.pallas_call(
        _ctx_attn_kernel,
        grid=(nb, N_KV),
        in_specs=[pl.BlockSpec(memory_space=pltpu.SMEM),
                  q_spec(OFF_QG), kv_spec(OFF_KG), kv_spec(OFF_VG),
                  q_spec(OFF_QW), kv_spec(OFF_KW), kv_spec(OFF_VW), vec, vec],
        out_specs=(o_spec, o_spec, c_spec, c_spec, c_spec, c_spec),
        out_shape=(jax.ShapeDtypeStruct((m, Q_W), BF16), jax.ShapeDtypeStruct((m, Q_W), BF16))
        + tuple(jax.ShapeDtypeStruct((m, KV_W), F32) for _ in range(4)),
        compiler_params=_cparams("parallel", "parallel"), name="ctx_attn",
    )(sink, z, z, z, z, z, z, qn_g.reshape(1, HEAD_DIM), kn_g.reshape(1, HEAD_DIM))


def _rope(x, cos, sin, low_half):
    partner = jnp.where(low_half, pltpu.roll(x, HEAD_DIM - 32, axis=1), pltpu.roll(x, 32, axis=1))
    return x * cos + partner * sin


def _low_half(rows):
    lane = lax.broadcasted_iota(jnp.int32, (rows, HEAD_DIM), 1)
    return (lane & 63) < 32


def _lat_global_kernel(q_ref, k_ref, v_ref, ck_ref, cv_ref, cos_ref, sin_ref, qn_ref, kn_ref,
                       o_ref, kall, vall, *, tq):
    seq = q_ref.shape[0]
    past = ck_ref.shape[0]
    kall[0:past, :] = ck_ref[...].astype(BF16)
    vall[0:past, :] = cv_ref[...].astype(BF16)
    kn = _rope(_rms(k_ref[...], kn_ref[...]), cos_ref[...], sin_ref[...], _low_half(seq))
    kall[past:past + seq, :] = kn.astype(BF16)
    vall[past:past + seq, :] = v_ref[...].astype(BF16)
    low = _low_half(tq)

    def q_block(qb, carry):
        rows = pl.ds(pl.multiple_of(qb * tq, tq), tq)
        cos, sin = cos_ref[rows, :], sin_ref[rows, :]
        for h in range(N_GROUP):
            cols = slice(h * HEAD_DIM, (h + 1) * HEAD_DIM)
            q = _rope(_rms(q_ref[rows, cols], qn_ref[...]), cos, sin, low).astype(BF16)
            o_ref[rows, cols] = _softmax_attend(q, kall[...], vall[...], None).astype(BF16)
        return carry

    lax.fori_loop(0, seq // tq, q_block, 0)


def _lat_global(z, cache_k, cache_v, layer, cos, sin, qn_g, kn_g, nb, seq, tq=256):
    past = cache_k.shape[2]
    q_spec = pl.BlockSpec((seq, GQ_W), lambda b, k: (b, OFF_QG // GQ_W + k))
    kv_spec = lambda off: pl.BlockSpec((seq, HEAD_DIM), lambda b, k: (b, off // HEAD_DIM + k))
    c_spec = pl.BlockSpec((None, None, past, HEAD_DIM), lambda b, k: (b, layer, 0, k))
    tab = pl.BlockSpec((seq, HEAD_DIM), lambda b, k: (0, 0))
    vec = pl.BlockSpec((1, HEAD_DIM), lambda b, k: (0, 0))
    return pl.pallas_call(
        functools.partial(_lat_global_kernel, tq=tq),
        grid=(nb, N_KV),
        in_specs=[q_spec, kv_spec(OFF_KG), kv_spec(OFF_VG), c_spec, c_spec, tab, tab, vec, vec],
        out_specs=pl.BlockSpec((seq, GQ_W), lambda b, k: (b, k)),
        out_shape=jax.ShapeDtypeStruct((nb * seq, Q_W), BF16),
        scratch_shapes=[pltpu.VMEM((past + seq, HEAD_DIM), BF16), pltpu.VMEM((past + seq, HEAD_DIM), BF16)],
        compiler_params=_cparams("parallel", "parallel"), name="lat_global",
    )(z, z, z, cache_k, cache_v, cos, sin, qn_g.reshape(1, HEAD_DIM), kn_g.reshape(1, HEAD_DIM))


def _lat_window_kernel(sink_ref, q_ref, k_ref, v_ref, ck_ref, cv_ref, cos_ref, sin_ref,
                       o_ref, kpad, vpad):
    seq = q_ref.shape[0]
    blk = WINDOW
    kvh = pl.program_id(1)
    zeros = jnp.zeros((blk, HEAD_DIM), BF16)
    kpad[0:blk, :] = zeros
    vpad[0:blk, :] = zeros
    kpad[blk + seq:2 * blk + seq, :] = zeros
    vpad[blk + seq:2 * blk + seq, :] = zeros
    kpad[blk:blk + seq, :] = _rope(k_ref[...], cos_ref[...], sin_ref[...], _low_half(seq)).astype(BF16)
    vpad[blk:blk + seq, :] = v_ref[...].astype(BF16)
    ck = ck_ref[...].astype(BF16)
    cv = cv_ref[...].astype(BF16)
    low = _low_half(blk)
    rows4 = N_GROUP * blk
    head = lax.broadcasted_iota(jnp.int32, (rows4, 1), 0) // blk
    sink = jnp.zeros((rows4, 1), F32)
    for h in range(N_GROUP):
        sink = jnp.where(head == h, sink_ref[kvh * N_GROUP + h], sink)
    qi = lax.broadcasted_iota(jnp.int32, (rows4, 3 * blk), 0) & (blk - 1)
    kj = lax.broadcasted_iota(jnp.int32, (rows4, 3 * blk), 1)
    in_band = jnp.abs(qi + blk - kj) <= WINDOW

    def q_block(n, carry):
        r0 = pl.multiple_of(n * blk, blk)
        rows = pl.ds(r0, blk)
        cos, sin = cos_ref[rows, :], sin_ref[rows, :]
        q4 = jnp.concatenate(
            [_rope(q_ref[rows, h * HEAD_DIM:(h + 1) * HEAD_DIM], cos, sin, low) for h in range(N_GROUP)],
            axis=0).astype(BF16)
        tk = (n - 1) * blk + kj
        valid = in_band & (tk >= 0) & (tk < seq)
        s_loc = jnp.where(valid, _dot_nt(q4, kpad[pl.ds(r0, 3 * blk), :]) * ATTN_SCALE, NEG_INF)
        s_ctx = _dot_nt(q4, ck) * ATTN_SCALE
        m = jnp.maximum(jnp.maximum(jnp.max(s_loc, axis=-1, keepdims=True),
                                    jnp.max(s_ctx, axis=-1, keepdims=True)), sink)
        e_loc = jnp.exp(s_loc - m)
        e_ctx = jnp.exp(s_ctx - m)
        den = (jnp.sum(e_loc, axis=-1, keepdims=True) + jnp.sum(e_ctx, axis=-1, keepdims=True)
               + jnp.exp(sink - m))
        o4 = (_dot(e_loc.astype(BF16), vpad[pl.ds(r0, 3 * blk), :]) + _dot(e_ctx.astype(BF16), cv)) * (1.0 / den)
        for h in range(N_GROUP):
            o_ref[rows, h * HEAD_DIM:(h + 1) * HEAD_DIM] = o4[h * blk:(h + 1) * blk].astype(BF16)
        return carry

    lax.fori_loop(0, seq // blk, q_block, 0)


def _lat_window(z, cache_k, cache_v, layer, cos, sin, sink, nb, seq):
    past = cache_k.shape[2]
    q_spec = pl.BlockSpec((seq, GQ_W), lambda b, k: (b, OFF_QW // GQ_W + k))
    kv_spec = lambda off: pl.BlockSpec((seq, HEAD_DIM), lambda b, k: (b, off // HEAD_DIM + k))
    c_spec = pl.BlockSpec((None, None, past, HEAD_DIM), lambda b, k: (b, layer, 0, k))
    tab = pl.BlockSpec((seq, HEAD_DIM), lambda b, k: (0, 0))
    return pl.pallas_call(
        _lat_window_kernel,
        grid=(nb, N_KV),
        in_specs=[pl.BlockSpec(memory_space=pltpu.SMEM), q_spec, kv_spec(OFF_KW), kv_spec(OFF_VW),
                  c_spec, c_spec, tab, tab],
        out_specs=pl.BlockSpec((seq, GQ_W), lambda b, k: (b, k)),
        out_shape=jax.ShapeDtypeStruct((nb * seq, Q_W), BF16),
        scratch_shapes=[pltpu.VMEM((seq + 2 * WINDOW, HEAD_DIM), BF16),
                        pltpu.VMEM((seq + 2 * WINDOW, HEAD_DIM), BF16)],
        compiler_params=_cparams("parallel", "parallel"), name="lat_window",
    )(sink, z, z, z, cache_k, cache_v, cos, sin)


def _interleave(v, n_tiles):
    return jnp.swapaxes(v.reshape(SUBLANE, n_tiles, v.shape[-1]), 0, 1).reshape(v.shape)


def _deinterleave(v, n_tiles):
    return jnp.swapaxes(v.reshape(n_tiles, SUBLANE, v.shape[-1]), 0, 1).reshape(v.shape)


def _lru_kernel(x_ref, y_ref, cw_ref, cb_ref, lam_ref, wa_ref, ba_ref, wx_ref, bx_ref, h0_ref,
                o_ref, st_ref, a_f, b_f, a_b, b_b):
    seq, width = x_ref.shape
    n_tiles = seq // SUBLANE
    sub = lax.broadcasted_iota(jnp.int32, (SUBLANE, width), 0)

    def from_prev_segment(tile):
        return jnp.where(sub >= 1, pltpu.roll(tile, 1, axis=0), 0.0)

    def from_next_segment(tile):
        return jnp.where(sub < SUBLANE - 1, pltpu.roll(tile, SUBLANE - 1, axis=0), 0.0)

    x = _interleave(x_ref[...], n_tiles)
    last, last2, first = x[seq - SUBLANE:], x[seq - 2 * SUBLANE:seq - SUBLANE], x[:SUBLANE]
    xm1 = jnp.concatenate([from_prev_segment(last), x[:seq - SUBLANE]], axis=0)
    xm2 = jnp.concatenate([from_prev_segment(last2), from_prev_segment(last), x[:seq - 2 * SUBLANE]], axis=0)
    xp1 = jnp.concatenate([x[SUBLANE:], from_next_segment(first)], axis=0)
    xc = (cb_ref[...] + xm2 * cw_ref[0:1, :] + xm1 * cw_ref[1:2, :] + x * cw_ref[2:3, :] + xp1 * cw_ref[3:4, :])
    xcb = xc.astype(BF16)

    def block_diag(w_ref, d):
        return jnp.concatenate(
            [_dot(xcb[:, j * LRU_BW:(j + 1) * LRU_BW], w_ref[d, j]) for j in range(width // LRU_BW)], axis=1)

    for d, (a_s, b_s) in enumerate(((a_f, b_f), (a_b, b_b))):
        r = jax.nn.sigmoid(block_diag(wa_ref, d) + ba_ref[d:d + 1, :])
        gate_i = jax.nn.sigmoid(block_diag(wx_ref, d) + bx_ref[d:d + 1, :])
        neg_lam = -lam_ref[d:d + 1, :]
        softplus = jnp.maximum(neg_lam, 0.0) + jnp.log1p(jnp.exp(-jnp.abs(neg_lam)))
        log_a = -LRU_C * r * softplus
        a_s[...] = jnp.exp(log_a)
        th = jnp.tanh(log_a)
        b_s[...] = jnp.sqrt(-2.0 * th / (1.0 - th)) * (gate_i * xc)

    def local_step(i, carry):
        hf, pf, hb, pb = carry
        rf = pl.ds(pl.multiple_of(i * SUBLANE, SUBLANE), SUBLANE)
        rb = pl.ds(pl.multiple_of((n_tiles - 1 - i) * SUBLANE, SUBLANE), SUBLANE)
        af, ab = a_f[rf, :], a_b[rb, :]
        hf, pf = af * hf + b_f[rf, :], af * pf
        hb, pb = ab * hb + b_b[rb, :], ab * pb
        a_f[rf, :], b_f[rf, :] = pf, hf
        a_b[rb, :], b_b[rb, :] = pb, hb
        return hf, pf, hb, pb

    zeros, ones = jnp.zeros((SUBLANE, width), F32), jnp.ones((SUBLANE, width), F32)
    hf, pf, hb, pb = lax.fori_loop(0, n_tiles, local_step, (zeros, ones, zeros, ones), unroll=8)

    for s in (1, 2, 4):
        take_f, take_b = sub >= s, sub < SUBLANE - s
        hf = jnp.where(take_f, pf * pltpu.roll(hf, s, axis=0) + hf, hf)
        pf = jnp.where(take_f, pf * pltpu.roll(pf, s, axis=0), pf)
        hb = jnp.where(take_b, pb * pltpu.roll(hb, SUBLANE - s, axis=0) + hb, hb)
        pb = jnp.where(take_b, pb * pltpu.roll(pb, SUBLANE - s, axis=0), pb)
    h0f = jnp.broadcast_to(h0_ref[0:1, :], (SUBLANE, width))
    h0b = jnp.broadcast_to(h0_ref[1:2, :], (SUBLANE, width))
    out_f = pf * h0f + hf
    out_b = pb * h0b + hb
    st_ref[0:1, :] = out_f[SUBLANE - 1:SUBLANE, :]
    st_ref[1:2, :] = out_b[0:1, :]
    in_f = jnp.where(sub >= 1, pltpu.roll(out_f, 1, axis=0), h0f)
    in_b = jnp.where(sub < SUBLANE - 1, pltpu.roll(out_b, SUBLANE - 1, axis=0), h0b)

    def tiles(ref):
        return ref[...].reshape(n_tiles, SUBLANE, width)

    h = (tiles(b_f) + tiles(a_f) * in_f[None]) + (tiles(b_b) + tiles(a_b) * in_b[None])
    h = _deinterleave(h.reshape(seq, width), n_tiles)
    o_ref[...] = (h * jax.nn.gelu(y_ref[...])).astype(BF16)


def _lru(z, h0, h0_layer, conv_w, conv_b, lam, wa, ba, wx, bx, nb, seq, cw=256):
    nc = LRU_W // cw
    nd = cw // LRU_BW
    vec2 = pl.BlockSpec((2, cw), lambda b, c: (0, c))
    wspec = pl.BlockSpec((2, nd, LRU_BW, LRU_BW), lambda b, c: (0, c, 0, 0))
    return pl.pallas_call(
        _lru_kernel,
        grid=(nb, nc),
        in_specs=[pl.BlockSpec((seq, cw), lambda b, c: (b, OFF_XL // cw + c)),
                  pl.BlockSpec((seq, cw), lambda b, c: (b, OFF_YL // cw + c)),
                  pl.BlockSpec((4, cw), lambda b, c: (0, c)),
                  pl.BlockSpec((1, cw), lambda b, c: (0, c)),
                  vec2, wspec, vec2, wspec, vec2,
                  pl.BlockSpec((None, None, 2, cw), lambda b, c: (b, h0_layer, 0, c))],
        out_specs=(pl.BlockSpec((seq, cw), lambda b, c: (b, c)),
                   pl.BlockSpec((None, 2, cw), lambda b, c: (b, 0, c))),
        out_shape=(jax.ShapeDtypeStruct((nb * seq, LRU_W), BF16), jax.ShapeDtypeStruct((nb, 2, LRU_W), F32)),
        scratch_shapes=[pltpu.VMEM((seq, cw), F32)] * 4,
        compiler_params=_cparams("parallel", "parallel"), name="lru",
    )(z, z, conv_w, conv_b.reshape(1, LRU_W), lam, wa, ba, wx, bx, h0)


def _merge_kernel(b0, b1, b2, b3, w_ref, g0, g1, g2, g3, o_ref):
    acc = None
    for n, (br, gr) in enumerate(((b0, g0), (b1, g1), (b2, g2), (b3, g3))):
        t = jax.nn.sigmoid(gr[...]) * _dot(br[...], w_ref[n])
        acc = t if acc is None else acc + t
    o_ref[...] = acc.astype(BF16)


def _merge(branches, w_br, z, tm=1024, tn=256):
    m = z.shape[0]
    br = pl.BlockSpec((tm, BRANCH_W), lambda i, j: (i, 0))
    gate = lambda n: pl.BlockSpec((tm, tn), lambda i, j: (i, (OFF_GATE + n * D_MODEL) // tn + j))
    return pl.pallas_call(
        _merge_kernel,
        grid=(m // tm, D_MODEL // tn),
        in_specs=[br, br, br, br, pl.BlockSpec((N_BRANCH, BRANCH_W, tn), lambda i, j: (0, 0, j)),
                  gate(0), gate(1), gate(2), gate(3)],
        out_specs=pl.BlockSpec((tm, tn), lambda i, j: (i, j)),
        out_shape=jax.ShapeDtypeStruct((m, D_MODEL), BF16),
        compiler_params=_cparams("parallel", "arbitrary"), name="merge",
    )(*branches, w_br, z, z, z, z)


def _up_kernel(x_ref, wu_ref, wg_ref, cwu_ref, cwg_ref, cbu_ref, cbg_ref, o_ref, *, seq):
    rows, tn = o_ref.shape
    sub = lax.broadcasted_iota(jnp.int32, (SUBLANE, tn), 0)

    def conv(v, w_ref, b_ref):
        prev, nxt = [], []
        for q in range(rows // seq):
            vq = v[q * seq:(q + 1) * seq]
            head, tail = vq[:SUBLANE], vq[seq - SUBLANE:]
            prev += [jnp.where(sub >= 1, pltpu.roll(tail, 1, axis=0), 0.0), vq[:seq - SUBLANE]]
            nxt += [vq[SUBLANE:], jnp.where(sub < SUBLANE - 1, pltpu.roll(head, SUBLANE - 1, axis=0), 0.0)]
        prev, nxt = jnp.concatenate(prev, axis=0), jnp.concatenate(nxt, axis=0)
        return b_ref[...] + prev * w_ref[0:1, :] + v * w_ref[1:2, :] + nxt * w_ref[2:3, :]

    x = x_ref[...]
    u = conv(_dot(x, wu_ref[...].astype(BF16)), cwu_ref, cbu_ref)
    g = conv(_dot(x, wg_ref[...].astype(BF16)), cwg_ref, cbg_ref)
    o_ref[...] = (g * jax.nn.sigmoid(g) * u).astype(BF16)


def _up(h, w_up, conv_w, conv_b, layer, seq, tm=1024, tn=256):
    m = h.shape[0]
    nj = D_FF // tn
    w_spec = lambda off: pl.BlockSpec((None, D_MODEL, tn), lambda i, j: (layer, 0, off + j))
    cw_spec = lambda off: pl.BlockSpec((None, 3, tn), lambda i, j: (layer, 0, off + j))
    cb_spec = lambda off: pl.BlockSpec((None, 1, tn), lambda i, j: (layer, 0, off + j))
    conv_b3 = conv_b.reshape(DEPTH, 1, 2 * D_FF)
    return pl.pallas_call(
        functools.partial(_up_kernel, seq=seq),
        grid=(m // tm, nj),
        in_specs=[pl.BlockSpec((tm, D_MODEL), lambda i, j: (i, 0)), w_spec(0), w_spec(nj),
                  cw_spec(0), cw_spec(nj), cb_spec(0), cb_spec(nj)],
        out_specs=pl.BlockSpec((tm, tn), lambda i, j: (i, j)),
        out_shape=jax.ShapeDtypeStruct((m, D_FF), BF16),
        compiler_params=_cparams("parallel", "arbitrary"), name="ffn_up",
    )(h, w_up, w_up, conv_w, conv_w, conv_b3, conv_b3)


def _rope_tables(seq):
    t = jnp.arange(seq)
    half = HEAD_DIM // 2
    inv = ROPE_THETA ** (-jnp.arange(0, half, 2, dtype=F32) / half)
    ang_r = (t // GRID_W).astype(F32)[:, None] * inv[None, :]
    ang_c = (t % GRID_W).astype(F32)[:, None] * inv[None, :]
    cos = jnp.concatenate([jnp.cos(ang_r), jnp.cos(ang_r), jnp.cos(ang_c), jnp.cos(ang_c)], axis=1)
    sin = jnp.concatenate([-jnp.sin(ang_r), jnp.sin(ang_r), -jnp.sin(ang_c), jnp.sin(ang_c)], axis=1)
    return cos, sin


def kernel(x_prompt, x_sample, cache_gk, cache_gv, cache_wk, cache_wv, state_lru, c, c_ctx, w_ada, b_ada, w_in, b_in, pool_w, pool_scale, qn_g, kn_g, sink, lru_conv_w, lru_conv_b, lru_lambda, lru_wa, lru_ba, lru_wx, lru_bx, w_br, w_o, ln1_g, ln1_b, w_up, ffn_conv_w, ffn_conv_b, w_down, ln2_g, ln2_b):
    nb_c, seq_c, _ = x_prompt.shape
    nb_s, seq_s, _ = x_sample.shape
    past = cache_gk.shape[2]

    c_all = jnp.concatenate([c_ctx[None, :], c, jnp.zeros((ADA_ROWS - 1 - nb_s, D_MODEL), F32)], axis=0)
    ada = _ada(c_all, w_ada, b_ada)
    ada3 = [ada[l].reshape(ADA_ROWS * 6, 1, D_MODEL) for l in range(DEPTH)]
    cos, sin = _rope_tables(seq_s)
    caches = [a.reshape(nb_s, DEPTH, past, KV_W) for a in (cache_gk, cache_gv, cache_wk, cache_wv)]
    zero_state = jnp.zeros((nb_c, 1, 2, LRU_W), F32)

    groups = [dict(x=x_prompt.reshape(nb_c * seq_c, D_MODEL), nb=nb_c, seq=seq_c, cond=None),
              dict(x=x_sample.reshape(nb_s * seq_s, D_MODEL), nb=nb_s, seq=seq_s, cond=seq_s)]
    for g in groups:
        g["h"] = _lnmod(g["x"], ada3[0], g["cond"])
    new_cache = [[] for _ in range(5)]

    for l in range(DEPTH):
        w_br_l = w_br[l].astype(BF16)
        w_down_l = w_down[l].astype(BF16)
        pool_w_l = pool_w[l].astype(BF16)
        wa_l = lru_wa[l].astype(BF16)
        wx_l = lru_wx[l].astype(BF16)
        for gi, g in enumerate(groups):
            nb, seq = g["nb"], g["seq"]
            z = _matmul(g["h"], w_in, b_in[l], 2048, 512, "in_proj", layer=l, single_buffer_x=True)
            o_pool = _pool(z, pool_w_l, pool_scale[l], nb, seq)
            if gi == 0:
                o_g, o_w, gk, gv, wk, wv = _ctx_attn(z, sink[l], qn_g[l], kn_g[l], nb, seq)
                h0, h0_layer = zero_state, 0
            else:
                o_g = _lat_global(z, caches[0], caches[1], l, cos, sin, qn_g[l], kn_g[l], nb, seq)
                o_w = _lat_window(z, caches[2], caches[3], l, cos, sin, sink[l], nb, seq)
                h0, h0_layer = state_lru, l
            o_lru, st = _lru(z, h0, h0_layer, lru_conv_w[l], lru_conv_b[l], lru_lambda[l],
                             wa_l, lru_ba[l], wx_l, lru_bx[l], nb, seq)
            if gi == 0:
                for lst, a in zip(new_cache, (gk, gv, wk, wv, st)):
                    lst.append(a)
            merged = _merge((o_pool, o_g, o_w, o_lru), w_br_l, z)
            mix = _matmul(merged, w_o, None, 2048, 512, "out_proj", layer=l, single_buffer_x=True)
            latent = gi == 1
            x1, h2 = _post(g["x"], mix, ada3[l], 2, ln1_g[l], ln1_b[l], nb, seq, latent,
                           nxt=(ada3[l], 3, 4), h_interleaved=True)
            act = _up(h2, w_up, ffn_conv_w, ffn_conv_b, l, seq)
            ff = _matmul(act, w_down_l, None, 512, 512, "ffn_down")
            nxt = (ada3[l + 1], 0, 1) if l + 1 < DEPTH else None
            g["x"], g["h"] = _post(x1, ff, ada3[l], 5, ln2_g[l], ln2_b[l], nb, seq, latent,
                                   nxt=nxt, y_interleaved=True)

    y_prompt = groups[0]["x"].reshape(nb_c, seq_c, D_MODEL)
    y_sample = groups[1]["x"].reshape(nb_s, seq_s, D_MODEL)
    kv = [jnp.stack([a.reshape(nb_c, seq_c, N_KV, HEAD_DIM) for a in new_cache[i]], axis=1) for i in range(4)]
    new_lru = jnp.stack(new_cache[4], axis=1)
    return (y_prompt, y_sample, kv[0], kv[1], kv[2], kv[3], new_lru)
```

```python
import functools

import jax
import jax.numpy as jnp
from jax import lax
from jax.experimental import pallas as pl
from jax.experimental.pallas import tpu as pltpu

F32 = jnp.float32
BF16 = jnp.bfloat16

D_MODEL = 4096
DEPTH = 2
GRID_W = 64
BRANCH_W = D_MODEL // 4
N_BRANCH = 4
HEAD_DIM = 128
N_HEADS = BRANCH_W // HEAD_DIM
N_KV = N_HEADS // 4
N_GROUP = N_HEADS // N_KV
Q_W = N_HEADS * HEAD_DIM
KV_W = N_KV * HEAD_DIM
WINDOW = 128
ROPE_THETA = 10000.0
ATTN_SCALE = HEAD_DIM ** -0.5
NEG_INF = -1e30
POOL_WINDOWS = (2, 4, 8, 16)
POOL_GW = BRANCH_W // len(POOL_WINDOWS)
LRU_W = BRANCH_W
LRU_BW = 128
LRU_C = 8.0
D_FF = 11008
ALPHA = (2 * DEPTH) ** 0.25
IN_SIZES = (BRANCH_W, Q_W, KV_W, KV_W, Q_W, KV_W, KV_W, LRU_W, LRU_W, N_BRANCH * D_MODEL)
IN_W = sum(IN_SIZES)

OFF_POOL = 0
OFF_QG = OFF_POOL + BRANCH_W
OFF_KG = OFF_QG + Q_W
OFF_VG = OFF_KG + KV_W
OFF_QW = OFF_VG + KV_W
OFF_KW = OFF_QW + Q_W
OFF_VW = OFF_KW + KV_W
OFF_XL = OFF_VW + KV_W
OFF_YL = OFF_XL + LRU_W
OFF_GATE = OFF_YL + LRU_W

LANE = 128
SUBLANE = 8
ADA_ROWS = 16
GQ_W = N_GROUP * HEAD_DIM
VMEM_LIMIT = 56 << 20


def _cparams(*sem):
    return pltpu.CompilerParams(dimension_semantics=sem, vmem_limit_bytes=VMEM_LIMIT)


def _dot(a, b):
    return jnp.dot(a, b, preferred_element_type=F32)


def _dot_nt(a, b):
    return lax.dot_general(a, b, (((1,), (1,)), ((), ())), preferred_element_type=F32)


def _ln(x, eps=1e-6):
    mu = jnp.mean(x, axis=-1, keepdims=True)
    xc = x - mu
    var = jnp.mean(xc * xc, axis=-1, keepdims=True)
    return xc * lax.rsqrt(var + eps)


def _rms(x, g, eps=1e-6):
    return x * lax.rsqrt(jnp.mean(x * x, axis=-1, keepdims=True) + eps) * g


def _ada_kernel(c_ref, w_ref, b_ref, o_ref):
    c = c_ref[...]
    s = (c * jax.nn.sigmoid(c)).astype(BF16)
    o_ref[...] = _dot(s, w_ref[...].astype(BF16)) + b_ref[...]


def _ada(c_all, w_ada, b_ada, tn=512):
    n = w_ada.shape[-1]
    return pl.pallas_call(
        _ada_kernel,
        grid=(DEPTH, n // tn),
        in_specs=[pl.BlockSpec((ADA_ROWS, D_MODEL), lambda l, j: (0, 0)),
                  pl.BlockSpec((None, D_MODEL, tn), lambda l, j: (l, 0, j)),
                  pl.BlockSpec((None, 1, tn), lambda l, j: (l, 0, j))],
        out_specs=pl.BlockSpec((None, ADA_ROWS, tn), lambda l, j: (l, 0, j)),
        out_shape=jax.ShapeDtypeStruct((DEPTH, ADA_ROWS, n), F32),
        compiler_params=_cparams("parallel", "parallel"),
        name="ada",
    )(c_all, w_ada, b_ada.reshape(DEPTH, 1, n))


def _ada_spec(which, tm, rows_per_cond):
    if rows_per_cond is None:
        return pl.BlockSpec((None, 1, D_MODEL), lambda i: (which, 0, 0))
    return pl.BlockSpec((None, 1, D_MODEL), lambda i: ((1 + (i * tm) // rows_per_cond) * 6 + which, 0, 0))


def _lnmod_kernel(x_ref, sh_ref, sc_ref, h_ref):
    h_ref[...] = (_ln(x_ref[...]) * (1.0 + sc_ref[...]) + sh_ref[...]).astype(BF16)


def _lnmod(x, ada3, rows_per_cond, tm=256):
    m = x.shape[0]
    return pl.pallas_call(
        _lnmod_kernel,
        grid=(m // tm,),
        in_specs=[pl.BlockSpec((tm, D_MODEL), lambda i: (i, 0)),
                  _ada_spec(0, tm, rows_per_cond), _ada_spec(1, tm, rows_per_cond)],
        out_specs=pl.BlockSpec((tm, D_MODEL), lambda i: (i, 0)),
        out_shape=jax.ShapeDtypeStruct((m, D_MODEL), BF16),
        compiler_params=_cparams("parallel"),
        name="lnmod",
    )(x, ada3, ada3)


def _post_kernel(x_ref, y_ref, g_ref, lg_ref, lb_ref, sh_ref, sc_ref, xo_ref, h_ref, *, y_interleaved, h_interleaved):
    y = y_ref[...]
    if y_interleaved:
        y = jnp.swapaxes(y, 0, 1)
    xn = _ln(ALPHA * x_ref[...] + g_ref[...] * y) * lg_ref[...] + lb_ref[...]
    xo_ref[...] = xn
    if h_ref is not None:
        h = _ln(xn) * (1.0 + sc_ref[...]) + sh_ref[...]
        if h_interleaved:
            h = jnp.swapaxes(h, 0, 1)
        h_ref[...] = h.astype(BF16)


def _post_last_kernel(x_ref, y_ref, g_ref, lg_ref, lb_ref, xo_ref, *, y_interleaved):
    _post_kernel(x_ref, y_ref, g_ref, lg_ref, lb_ref, None, None, xo_ref, None,
                 y_interleaved=y_interleaved, h_interleaved=False)


POST_ROWS = 32


def _post(x, y, ada3, gate_idx, ln_g, ln_b, nb, seq, latent, nxt=None, y_interleaved=False, h_interleaved=False):
    nt = seq // SUBLANE
    nat_shape, int_shape = (nb, SUBLANE, nt, D_MODEL), (nb, nt, SUBLANE, D_MODEL)
    nat = pl.BlockSpec((None, SUBLANE, POST_ROWS, D_MODEL), lambda q, j: (q, 0, j, 0))
    itl = pl.BlockSpec((None, POST_ROWS, SUBLANE, D_MODEL), lambda q, j: (q, j, 0, 0))
    vec = pl.BlockSpec((1, D_MODEL), lambda q, j: (0, 0))
    ada = lambda which: pl.BlockSpec((None, 1, D_MODEL),
                                     lambda q, j: (((1 + q) * 6 if latent else 0) + which, 0, 0))
    in_specs = [nat, itl if y_interleaved else nat, ada(gate_idx), vec, vec]
    args = [x.reshape(nat_shape), y.reshape(int_shape if y_interleaved else nat_shape), ada3,
            ln_g.reshape(1, D_MODEL), ln_b.reshape(1, D_MODEL)]
    grid = (nb, nt // POST_ROWS)
    if nxt is None:
        xo = pl.pallas_call(
            functools.partial(_post_last_kernel, y_interleaved=y_interleaved),
            grid=grid, in_specs=in_specs, out_specs=nat,
            out_shape=jax.ShapeDtypeStruct(nat_shape, F32),
            compiler_params=_cparams("parallel", "parallel"), name="post_last",
        )(*args)
        return xo.reshape(nb * seq, D_MODEL), None
    nxt_ada3, sh_idx, sc_idx = nxt
    in_specs += [ada(sh_idx), ada(sc_idx)]
    args += [nxt_ada3, nxt_ada3]
    xo, h = pl.pallas_call(
        functools.partial(_post_kernel, y_interleaved=y_interleaved, h_interleaved=h_interleaved),
        grid=grid, in_specs=in_specs, out_specs=(nat, itl if h_interleaved else nat),
        out_shape=(jax.ShapeDtypeStruct(nat_shape, F32),
                   jax.ShapeDtypeStruct(int_shape if h_interleaved else nat_shape, BF16)),
        compiler_params=_cparams("parallel", "parallel"), name="post",
    )(*args)
    return xo.reshape(nb * seq, D_MODEL), h.reshape(nb * seq, D_MODEL)


def _mm_bias_kernel(x_ref, w_ref, b_ref, o_ref):
    o_ref[...] = (_dot(x_ref[...], w_ref[...].astype(BF16)) + b_ref[...]).astype(o_ref.dtype)


def _mm_kernel(x_ref, w_ref, o_ref):
    o_ref[...] = _dot(x_ref[...], w_ref[...].astype(BF16)).astype(o_ref.dtype)


def _matmul(x, w, b, tm, tn, name, layer=None, single_buffer_x=False):
    m, k = x.shape
    n = w.shape[-1]
    tm = min(tm, m)
    x_mode = dict(pipeline_mode=pl.Buffered(1)) if single_buffer_x else {}
    if layer is None:
        w_spec = pl.BlockSpec((k, tn), lambda i, j: (0, j))
    else:
        w_spec = pl.BlockSpec((None, k, tn), lambda i, j: (layer, 0, j))
    in_specs = [pl.BlockSpec((tm, k), lambda i, j: (i, 0), **x_mode), w_spec]
    args = [x, w]
    body = _mm_kernel
    if b is not None:
        in_specs.append(pl.BlockSpec((1, tn), lambda i, j: (0, j)))
        args.append(b.reshape(1, n))
        body = _mm_bias_kernel
    return pl.pallas_call(
        body, grid=(m // tm, n // tn), in_specs=in_specs,
        out_specs=pl.BlockSpec((tm, tn), lambda i, j: (i, j)),
        out_shape=jax.ShapeDtypeStruct((m, n), F32),
        compiler_params=_cparams("parallel", "arbitrary"), name=name,
    )(*args)


def _pool_kernel(u_ref, w_ref, sc_ref, o_ref):
    seq = u_ref.shape[0]
    t = lax.broadcasted_iota(jnp.int32, (seq, seq), 0)
    s = lax.broadcasted_iota(jnp.int32, (seq, seq), 1)
    d = s - t
    tcol = lax.broadcasted_iota(jnp.int32, (seq, 1), 0)
    for gi, win in enumerate(POOL_WINDOWS):
        half = win // 2
        cols = slice(gi * POOL_GW, (gi + 1) * POOL_GW)
        band = jnp.where((d >= -half) & (d < half), 1.0, 0.0).astype(BF16)
        cnt = jnp.minimum(tcol + half, seq) - jnp.maximum(tcol - half, 0)
        u = u_ref[:, cols]
        u_hi = u.astype(BF16)
        u_lo = (u - u_hi.astype(F32)).astype(BF16)
        mean = (_dot(band, u_hi) + _dot(band, u_lo)) / cnt.astype(F32)
        y = _dot((mean - u).astype(BF16), w_ref[gi])
        o_ref[:, cols] = (y * sc_ref[:, cols]).astype(BF16)


def _pool(z, pool_w, pool_scale, nb, seq):
    return pl.pallas_call(
        _pool_kernel,
        grid=(nb,),
        in_specs=[pl.BlockSpec((seq, BRANCH_W), lambda b: (b, OFF_POOL // BRANCH_W)),
                  pl.BlockSpec((len(POOL_WINDOWS), POOL_GW, POOL_GW), lambda b: (0, 0, 0)),
                  pl.BlockSpec((1, BRANCH_W), lambda b: (0, 0))],
        out_specs=pl.BlockSpec((seq, BRANCH_W), lambda b: (b, 0)),
        out_shape=jax.ShapeDtypeStruct((nb * seq, BRANCH_W), BF16),
        compiler_params=_cparams("parallel"), name="pool",
    )(z, pool_w, pool_scale.reshape(1, BRANCH_W))


def _softmax_attend(q, k, v, sink):
    s = _dot_nt(q, k) * ATTN_SCALE
    m = jnp.max(s, axis=-1, keepdims=True)
    if sink is not None:
        m = jnp.maximum(m, sink)
    e = jnp.exp(s - m)
    den = jnp.sum(e, axis=-1, keepdims=True)
    if sink is not None:
        den = den + jnp.exp(sink - m)
    return _dot(e.astype(BF16), v) * (1.0 / den)


def _ctx_attn_kernel(sink_ref, qg_ref, kg_ref, vg_ref, qw_ref, kw_ref, vw_ref, qn_ref, kn_ref,
                     og_ref, ow_ref, gk_ref, gv_ref, wk_ref, wv_ref):
    kvh = pl.program_id(1)
    kg = _rms(kg_ref[...], kn_ref[...])
    gk_ref[...] = kg
    gv_ref[...] = vg_ref[...]
    wk_ref[...] = kw_ref[...]
    wv_ref[...] = vw_ref[...]
    kg_b, vg_b = kg.astype(BF16), vg_ref[...].astype(BF16)
    kw_b, vw_b = kw_ref[...].astype(BF16), vw_ref[...].astype(BF16)
    for h in range(N_GROUP):
        cols = slice(h * HEAD_DIM, (h + 1) * HEAD_DIM)
        qg = _rms(qg_ref[:, cols], qn_ref[...]).astype(BF16)
        og_ref[:, cols] = _softmax_attend(qg, kg_b, vg_b, None).astype(BF16)
        sink = sink_ref[kvh * N_GROUP + h]
        ow_ref[:, cols] = _softmax_attend(qw_ref[:, cols].astype(BF16), kw_b, vw_b, sink).astype(BF16)


def _ctx_attn(z, sink, qn_g, kn_g, nb, seq):
    q_spec = lambda off: pl.BlockSpec((seq, GQ_W), lambda b, k: (b, off // GQ_W + k))
    kv_spec = lambda off: pl.BlockSpec((seq, HEAD_DIM), lambda b, k: (b, off // HEAD_DIM + k))
    vec = pl.BlockSpec((1, HEAD_DIM), lambda b, k: (0, 0))
    o_spec = pl.BlockSpec((seq, GQ_W), lambda b, k: (b, k))
    c_spec = pl.BlockSpec((seq, HEAD_DIM), lambda b, k: (b, k))
    m = nb * seq
    return pl.pallas_call(
        _ctx_attn_kernel,
        grid=(nb, N_KV),
        in_specs=[pl.BlockSpec(memory_space=pltpu.SMEM),
                  q_spec(OFF_QG), kv_spec(OFF_KG), kv_spec(OFF_VG),
                  q_spec(OFF_QW), kv_spec(OFF_KW), kv_spec(OFF_VW), vec, vec],
        out_specs=(o_spec, o_spec, c_spec, c_spec, c_spec, c_spec),
        out_shape=(jax.ShapeDtypeStruct((m, Q_W), BF16), jax.ShapeDtypeStruct((m, Q_W), BF16))
        + tuple(jax.ShapeDtypeStruct((m, KV_W), F32) for _ in range(4)),
        compiler_params=_cparams("parallel", "parallel"), name="ctx_attn",
    )(sink, z, z, z, z, z, z, qn_g.reshape(1, HEAD_DIM), kn_g.reshape(1, HEAD_DIM))


def _rope(x, cos, sin, low_half):
    partner = jnp.where(low_half, pltpu.roll(x, HEAD_DIM - 32, axis=1), pltpu.roll(x, 32, axis=1))
    return x * cos + partner * sin


def _low_half(rows):
    lane = lax.broadcasted_iota(jnp.int32, (rows, HEAD_DIM), 1)
    return (lane & 63) < 32


def _lat_global_kernel(q_ref, k_ref, v_ref, ck_ref, cv_ref, cos_ref, sin_ref, qn_ref, kn_ref,
                       o_ref, kall, vall, *, tq):
    seq = q_ref.shape[0]
    past = ck_ref.shape[0]
    kall[0:past, :] = ck_ref[...].astype(BF16)
    vall[0:past, :] = cv_ref[...].astype(BF16)
    kn = _rope(_rms(k_ref[...], kn_ref[...]), cos_ref[...], sin_ref[...], _low_half(seq))
    kall[past:past + seq, :] = kn.astype(BF16)
    vall[past:past + seq, :] = v_ref[...].astype(BF16)
    low = _low_half(tq)

    def q_block(qb, carry):
        rows = pl.ds(pl.multiple_of(qb * tq, tq), tq)
        cos, sin = cos_ref[rows, :], sin_ref[rows, :]
        for h in range(N_GROUP):
            cols = slice(h * HEAD_DIM, (h + 1) * HEAD_DIM)
            q = _rope(_rms(q_ref[rows, cols], qn_ref[...]), cos, sin, low).astype(BF16)
            o_ref[rows, cols] = _softmax_attend(q, kall[...], vall[...], None).astype(BF16)
        return carry

    lax.fori_loop(0, seq // tq, q_block, 0)


def _lat_global(z, cache_k, cache_v, layer, cos, sin, qn_g, kn_g, nb, seq, tq=256):
    past = cache_k.shape[2]
    q_spec = pl.BlockSpec((seq, GQ_W), lambda b, k: (b, OFF_QG // GQ_W + k))
    kv_spec = lambda off: pl.BlockSpec((seq, HEAD_DIM), lambda b, k: (b, off // HEAD_DIM + k))
    c_spec = pl.BlockSpec((None, None, past, HEAD_DIM), lambda b, k: (b, layer, 0, k))
    tab = pl.BlockSpec((seq, HEAD_DIM), lambda b, k: (0, 0))
    vec = pl.BlockSpec((1, HEAD_DIM), lambda b, k: (0, 0))
    return pl.pallas_call(
        functools.partial(_lat_global_kernel, tq=tq),
        grid=(nb, N_KV),
        in_specs=[q_spec, kv_spec(OFF_KG), kv_spec(OFF_VG), c_spec, c_spec, tab, tab, vec, vec],
        out_specs=pl.BlockSpec((seq, GQ_W), lambda b, k: (b, k)),
        out_shape=jax.ShapeDtypeStruct((nb * seq, Q_W), BF16),
        scratch_shapes=[pltpu.VMEM((past + seq, HEAD_DIM), BF16), pltpu.VMEM((past + seq, HEAD_DIM), BF16)],
        compiler_params=_cparams("parallel", "parallel"), name="lat_global",
    )(z, z, z, cache_k, cache_v, cos, sin, qn_g.reshape(1, HEAD_DIM), kn_g.reshape(1, HEAD_DIM))


def _lat_window_kernel(sink_ref, q_ref, k_ref, v_ref, ck_ref, cv_ref, cos_ref, sin_ref,
                       o_ref, kpad, vpad):
    seq = q_ref.shape[0]
    blk = WINDOW
    kvh = pl.program_id(1)
    zeros = jnp.zeros((blk, HEAD_DIM), BF16)
    kpad[0:blk, :] = zeros
    vpad[0:blk, :] = zeros
    kpad[blk + seq:2 * blk + seq, :] = zeros
    vpad[blk + seq:2 * blk + seq, :] = zeros
    kpad[blk:blk + seq, :] = _rope(k_ref[...], cos_ref[...], sin_ref[...], _low_half(seq)).astype(BF16)
    vpad[blk:blk + seq, :] = v_ref[...].astype(BF16)
    ck = ck_ref[...].astype(BF16)
    cv = cv_ref[...].astype(BF16)
    low = _low_half(blk)
    rows4 = N_GROUP * blk
    head = lax.broadcasted_iota(jnp.int32, (rows4, 1), 0) // blk
    sink = jnp.zeros((rows4, 1), F32)
    for h in range(N_GROUP):
        sink = jnp.where(head == h, sink_ref[kvh * N_GROUP + h], sink)
    qi = lax.broadcasted_iota(jnp.int32, (rows4, 3 * blk), 0) & (blk - 1)
    kj = lax.broadcasted_iota(jnp.int32, (rows4, 3 * blk), 1)
    in_band = jnp.abs(qi + blk - kj) <= WINDOW

    def q_block(n, carry):
        r0 = pl.multiple_of(n * blk, blk)
        rows = pl.ds(r0, blk)
        cos, sin = cos_ref[rows, :], sin_ref[rows, :]
        q4 = jnp.concatenate(
            [_rope(q_ref[rows, h * HEAD_DIM:(h + 1) * HEAD_DIM], cos, sin, low) for h in range(N_GROUP)],
            axis=0).astype(BF16)
        tk = (n - 1) * blk + kj
        valid = in_band & (tk >= 0) & (tk < seq)
        s_loc = jnp.where(valid, _dot_nt(q4, kpad[pl.ds(r0, 3 * blk), :]) * ATTN_SCALE, NEG_INF)
        s_ctx = _dot_nt(q4, ck) * ATTN_SCALE
        m = jnp.maximum(jnp.maximum(jnp.max(s_loc, axis=-1, keepdims=True),
                                    jnp.max(s_ctx, axis=-1, keepdims=True)), sink)
        e_loc = jnp.exp(s_loc - m)
        e_ctx = jnp.exp(s_ctx - m)
        den = (jnp.sum(e_loc, axis=-1, keepdims=True) + jnp.sum(e_ctx, axis=-1, keepdims=True)
               + jnp.exp(sink - m))
        o4 = (_dot(e_loc.astype(BF16), vpad[pl.ds(r0, 3 * blk), :]) + _dot(e_ctx.astype(BF16), cv)) * (1.0 / den)
        for h in range(N_GROUP):
            o_ref[rows, h * HEAD_DIM:(h + 1) * HEAD_DIM] = o4[h * blk:(h + 1) * blk].astype(BF16)
        return carry

    lax.fori_loop(0, seq // blk, q_block, 0)


def _lat_window(z, cache_k, cache_v, layer, cos, sin, sink, nb, seq):
    past = cache_k.shape[2]
    q_spec = pl.BlockSpec((seq, GQ_W), lambda b, k: (b, OFF_QW // GQ_W + k))
    kv_spec = lambda off: pl.BlockSpec((seq, HEAD_DIM), lambda b, k: (b, off // HEAD_DIM + k))
    c_spec = pl.BlockSpec((None, None, past, HEAD_DIM), lambda b, k: (b, layer, 0, k))
    tab = pl.BlockSpec((seq, HEAD_DIM), lambda b, k: (0, 0))
    return pl.pallas_call(
        _lat_window_kernel,
        grid=(nb, N_KV),
        in_specs=[pl.BlockSpec(memory_space=pltpu.SMEM), q_spec, kv_spec(OFF_KW), kv_spec(OFF_VW),
                  c_spec, c_spec, tab, tab],
        out_specs=pl.BlockSpec((seq, GQ_W), lambda b, k: (b, k)),
        out_shape=jax.ShapeDtypeStruct((nb * seq, Q_W), BF16),
        scratch_shapes=[pltpu.VMEM((seq + 2 * WINDOW, HEAD_DIM), BF16),
                        pltpu.VMEM((seq + 2 * WINDOW, HEAD_DIM), BF16)],
        compiler_params=_cparams("parallel", "parallel"), name="lat_window",
    )(sink, z, z, z, cache_k, cache_v, cos, sin)


def _interleave(v, n_tiles):
    return jnp.swapaxes(v.reshape(SUBLANE, n_tiles, v.shape[-1]), 0, 1).reshape(v.shape)


def _deinterleave(v, n_tiles):
    return jnp.swapaxes(v.reshape(n_tiles, SUBLANE, v.shape[-1]), 0, 1).reshape(v.shape)


def _lru_kernel(x_ref, y_ref, cw_ref, cb_ref, lam_ref, wa_ref, ba_ref, wx_ref, bx_ref, h0_ref,
                o_ref, st_ref, a_f, b_f, a_b, b_b):
    seq, width = x_ref.shape
    n_tiles = seq // SUBLANE
    sub = lax.broadcasted_iota(jnp.int32, (SUBLANE, width), 0)

    def from_prev_segment(tile):
        return jnp.where(sub >= 1, pltpu.roll(tile, 1, axis=0), 0.0)

    def from_next_segment(tile):
        return jnp.where(sub < SUBLANE - 1, pltpu.roll(tile, SUBLANE - 1, axis=0), 0.0)

    x = _interleave(x_ref[...], n_tiles)
    last, last2, first = x[seq - SUBLANE:], x[seq - 2 * SUBLANE:seq - SUBLANE], x[:SUBLANE]
    xm1 = jnp.concatenate([from_prev_segment(last), x[:seq - SUBLANE]], axis=0)
    xm2 = jnp.concatenate([from_prev_segment(last2), from_prev_segment(last), x[:seq - 2 * SUBLANE]], axis=0)
    xp1 = jnp.concatenate([x[SUBLANE:], from_next_segment(first)], axis=0)
    xc = (cb_ref[...] + xm2 * cw_ref[0:1, :] + xm1 * cw_ref[1:2, :] + x * cw_ref[2:3, :] + xp1 * cw_ref[3:4, :])
    xcb = xc.astype(BF16)

    def block_diag(w_ref, d):
        return jnp.concatenate(
            [_dot(xcb[:, j * LRU_BW:(j + 1) * LRU_BW], w_ref[d, j]) for j in range(width // LRU_BW)], axis=1)

    for d, (a_s, b_s) in enumerate(((a_f, b_f), (a_b, b_b))):
        r = jax.nn.sigmoid(block_diag(wa_ref, d) + ba_ref[d:d + 1, :])
        gate_i = jax.nn.sigmoid(block_diag(wx_ref, d) + bx_ref[d:d + 1, :])
        neg_lam = -lam_ref[d:d + 1, :]
        softplus = jnp.maximum(neg_lam, 0.0) + jnp.log1p(jnp.exp(-jnp.abs(neg_lam)))
        log_a = -LRU_C * r * softplus
        a_s[...] = jnp.exp(log_a)
        th = jnp.tanh(log_a)
        b_s[...] = jnp.sqrt(-2.0 * th / (1.0 - th)) * (gate_i * xc)

    def local_step(i, carry):
        hf, pf, hb, pb = carry
        rf = pl.ds(pl.multiple_of(i * SUBLANE, SUBLANE), SUBLANE)
        rb = pl.ds(pl.multiple_of((n_tiles - 1 - i) * SUBLANE, SUBLANE), SUBLANE)
        af, ab = a_f[rf, :], a_b[rb, :]
        hf, pf = af * hf + b_f[rf, :], af * pf
        hb, pb = ab * hb + b_b[rb, :], ab * pb
        a_f[rf, :], b_f[rf, :] = pf, hf
        a_b[rb, :], b_b[rb, :] = pb, hb
        return hf, pf, hb, pb

    zeros, ones = jnp.zeros((SUBLANE, width), F32), jnp.ones((SUBLANE, width), F32)
    hf, pf, hb, pb = lax.fori_loop(0, n_tiles, local_step, (zeros, ones, zeros, ones), unroll=8)

    for s in (1, 2, 4):
        take_f, take_b = sub >= s, sub < SUBLANE - s
        hf = jnp.where(take_f, pf * pltpu.roll(hf, s, axis=0) + hf, hf)
        pf = jnp.where(take_f, pf * pltpu.roll(pf, s, axis=0), pf)
        hb = jnp.where(take_b, pb * pltpu.roll(hb, SUBLANE - s, axis=0) + hb, hb)
        pb = jnp.where(take_b, pb * pltpu.roll(pb, SUBLANE - s, axis=0), pb)
    h0f = jnp.broadcast_to(h0_ref[0:1, :], (SUBLANE, width))
    h0b = jnp.broadcast_to(h0_ref[1:2, :], (SUBLANE, width))
    out_f = pf * h0f + hf
    out_b = pb * h0b + hb
    st_ref[0:1, :] = out_f[SUBLANE - 1:SUBLANE, :]
    st_ref[1:2, :] = out_b[0:1, :]
    in_f = jnp.where(sub >= 1, pltpu.roll(out_f, 1, axis=0), h0f)
    in_b = jnp.where(sub < SUBLANE - 1, pltpu.roll(out_b, SUBLANE - 1, axis=0), h0b)

    def tiles(ref):
        return ref[...].reshape(n_tiles, SUBLANE, width)

    h = (tiles(b_f) + tiles(a_f) * in_f[None]) + (tiles(b_b) + tiles(a_b) * in_b[None])
    h = _deinterleave(h.reshape(seq, width), n_tiles)
    o_ref[...] = (h * jax.nn.gelu(y_ref[...])).astype(BF16)


def _lru(z, h0, h0_layer, conv_w, conv_b, lam, wa, ba, wx, bx, nb, seq, cw=256):
    nc = LRU_W // cw
    nd = cw // LRU_BW
    vec2 = pl.BlockSpec((2, cw), lambda b, c: (0, c))
    wspec = pl.BlockSpec((2, nd, LRU_BW, LRU_BW), lambda b, c: (0, c, 0, 0))
    return pl.pallas_call(
        _lru_kernel,
        grid=(nb, nc),
        in_specs=[pl.BlockSpec((seq, cw), lambda b, c: (b, OFF_XL // cw + c)),
                  pl.BlockSpec((seq, cw), lambda b, c: (b, OFF_YL // cw + c)),
                  pl.BlockSpec((4, cw), lambda b, c: (0, c)),
                  pl.BlockSpec((1, cw), lambda b, c: (0, c)),
                  vec2, wspec, vec2, wspec, vec2,
                  pl.BlockSpec((None, None, 2, cw), lambda b, c: (b, h0_layer, 0, c))],
        out_specs=(pl.BlockSpec((seq, cw), lambda b, c: (b, c)),
                   pl.BlockSpec((None, 2, cw), lambda b, c: (b, 0, c))),
        out_shape=(jax.ShapeDtypeStruct((nb * seq, LRU_W), BF16), jax.ShapeDtypeStruct((nb, 2, LRU_W), F32)),
        scratch_shapes=[pltpu.VMEM((seq, cw), F32)] * 4,
        compiler_params=_cparams("parallel", "parallel"), name="lru",
    )(z, z, conv_w, conv_b.reshape(1, LRU_W), lam, wa, ba, wx, bx, h0)


MERGE_ROW_CHUNK = 128


def _merge_kernel(b0, b1, b2, b3, w_ref, g0, g1, g2, g3, o_ref):
    for r in range(0, o_ref.shape[0], MERGE_ROW_CHUNK):
        rows = slice(r, r + MERGE_ROW_CHUNK)
        acc = None
        for n, (br, gr) in enumerate(((b0, g0), (b1, g1), (b2, g2), (b3, g3))):
            t = jax.nn.sigmoid(gr[rows, :]) * _dot(br[rows, :], w_ref[n])
            acc = t if acc is None else acc + t
        o_ref[rows, :] = acc.astype(BF16)


def _merge(branches, w_br, z, tm=1024, tn=256):
    m = z.shape[0]
    br = pl.BlockSpec((tm, BRANCH_W), lambda i, j: (i, 0))
    gate = lambda n: pl.BlockSpec((tm, tn), lambda i, j: (i, (OFF_GATE + n * D_MODEL) // tn + j))
    return pl.pallas_call(
        _merge_kernel,
        grid=(m // tm, D_MODEL // tn),
        in_specs=[br, br, br, br, pl.BlockSpec((N_BRANCH, BRANCH_W, tn), lambda i, j: (0, 0, j)),
                  gate(0), gate(1), gate(2), gate(3)],
        out_specs=pl.BlockSpec((tm, tn), lambda i, j: (i, j)),
        out_shape=jax.ShapeDtypeStruct((m, D_MODEL), BF16),
        compiler_params=_cparams("parallel", "arbitrary"), name="merge",
    )(*branches, w_br, z, z, z, z)


UP_ROW_CHUNK = 128


def _up_kernel(x_ref, wu_ref, wg_ref, cwu_ref, cwg_ref, cbu_ref, cbg_ref, o_ref, *, seq):
    rows, tn = o_ref.shape
    sub = lax.broadcasted_iota(jnp.int32, (SUBLANE, tn), 0)

    def conv(v, w_ref, b_ref):
        prev, nxt = [], []
        for q in range(rows // seq):
            vq = v[q * seq:(q + 1) * seq]
            head, tail = vq[:SUBLANE], vq[seq - SUBLANE:]
            prev += [jnp.where(sub >= 1, pltpu.roll(tail, 1, axis=0), 0.0), vq[:seq - SUBLANE]]
            nxt += [vq[SUBLANE:], jnp.where(sub < SUBLANE - 1, pltpu.roll(head, SUBLANE - 1, axis=0), 0.0)]
        prev, nxt = jnp.concatenate(prev, axis=0), jnp.concatenate(nxt, axis=0)
        return b_ref[...] + prev * w_ref[0:1, :] + v * w_ref[1:2, :] + nxt * w_ref[2:3, :]

    wu, wg = wu_ref[...].astype(BF16), wg_ref[...].astype(BF16)

    def row_chunked_dot(w):
        return jnp.concatenate([_dot(x_ref[r:r + UP_ROW_CHUNK, :], w) for r in range(0, rows, UP_ROW_CHUNK)],
                               axis=0)

    u = conv(row_chunked_dot(wu), cwu_ref, cbu_ref)
    g = conv(row_chunked_dot(wg), cwg_ref, cbg_ref)
    o_ref[...] = (g * jax.nn.sigmoid(g) * u).astype(BF16)


def _up(h, w_up, conv_w, conv_b, layer, seq, tm=1024, tn=256):
    m = h.shape[0]
    nj = D_FF // tn
    w_spec = lambda off: pl.BlockSpec((None, D_MODEL, tn), lambda i, j: (layer, 0, off + j))
    cw_spec = lambda off: pl.BlockSpec((None, 3, tn), lambda i, j: (layer, 0, off + j))
    cb_spec = lambda off: pl.BlockSpec((None, 1, tn), lambda i, j: (layer, 0, off + j))
    conv_b3 = conv_b.reshape(DEPTH, 1, 2 * D_FF)
    return pl.pallas_call(
        functools.partial(_up_kernel, seq=seq),
        grid=(m // tm, nj),
        in_specs=[pl.BlockSpec((tm, D_MODEL), lambda i, j: (i, 0)), w_spec(0), w_spec(nj),
                  cw_spec(0), cw_spec(nj), cb_spec(0), cb_spec(nj)],
        out_specs=pl.BlockSpec((tm, tn), lambda i, j: (i, j)),
        out_shape=jax.ShapeDtypeStruct((m, D_FF), BF16),
        compiler_params=_cparams("parallel", "arbitrary"), name="ffn_up",
    )(h, w_up, w_up, conv_w, conv_w, conv_b3, conv_b3)


def _rope_tables(seq):
    t = jnp.arange(seq)
    half = HEAD_DIM // 2
    inv = ROPE_THETA ** (-jnp.arange(0, half, 2, dtype=F32) / half)
    ang_r = (t // GRID_W).astype(F32)[:, None] * inv[None, :]
    ang_c = (t % GRID_W).astype(F32)[:, None] * inv[None, :]
    cos = jnp.concatenate([jnp.cos(ang_r), jnp.cos(ang_r), jnp.cos(ang_c), jnp.cos(ang_c)], axis=1)
    sin = jnp.concatenate([-jnp.sin(ang_r), jnp.sin(ang_r), -jnp.sin(ang_c), jnp.sin(ang_c)], axis=1)
    return cos, sin


def kernel(x_prompt, x_sample, cache_gk, cache_gv, cache_wk, cache_wv, state_lru, c, c_ctx, w_ada, b_ada, w_in, b_in, pool_w, pool_scale, qn_g, kn_g, sink, lru_conv_w, lru_conv_b, lru_lambda, lru_wa, lru_ba, lru_wx, lru_bx, w_br, w_o, ln1_g, ln1_b, w_up, ffn_conv_w, ffn_conv_b, w_down, ln2_g, ln2_b):
    nb_c, seq_c, _ = x_prompt.shape
    nb_s, seq_s, _ = x_sample.shape
    past = cache_gk.shape[2]

    c_all = jnp.concatenate([c_ctx[None, :], c, jnp.zeros((ADA_ROWS - 1 - nb_s, D_MODEL), F32)], axis=0)
    ada = _ada(c_all, w_ada, b_ada)
    ada3 = [ada[l].reshape(ADA_ROWS * 6, 1, D_MODEL) for l in range(DEPTH)]
    cos, sin = _rope_tables(seq_s)
    caches = [a.reshape(nb_s, DEPTH, past, KV_W) for a in (cache_gk, cache_gv, cache_wk, cache_wv)]
    zero_state = jnp.zeros((nb_c, 1, 2, LRU_W), F32)

    groups = [dict(x=x_prompt.reshape(nb_c * seq_c, D_MODEL), nb=nb_c, seq=seq_c, cond=None),
              dict(x=x_sample.reshape(nb_s * seq_s, D_MODEL), nb=nb_s, seq=seq_s, cond=seq_s)]
    for g in groups:
        g["h"] = _lnmod(g["x"], ada3[0], g["cond"])
    new_cache = [[] for _ in range(5)]

    for l in range(DEPTH):
        w_br_l = w_br[l].astype(BF16)
        w_down_l = w_down[l].astype(BF16)
        pool_w_l = pool_w[l].astype(BF16)
        wa_l = lru_wa[l].astype(BF16)
        wx_l = lru_wx[l].astype(BF16)
        for gi, g in enumerate(groups):
            nb, seq = g["nb"], g["seq"]
            z = _matmul(g["h"], w_in, b_in[l], 2048, 512, "in_proj", layer=l, single_buffer_x=True)
            o_pool = _pool(z, pool_w_l, pool_scale[l], nb, seq)
            if gi == 0:
                o_g, o_w, gk, gv, wk, wv = _ctx_attn(z, sink[l], qn_g[l], kn_g[l], nb, seq)
                h0, h0_layer = zero_state, 0
            else:
                o_g = _lat_global(z, caches[0], caches[1], l, cos, sin, qn_g[l], kn_g[l], nb, seq)
                o_w = _lat_window(z, caches[2], caches[3], l, cos, sin, sink[l], nb, seq)
                h0, h0_layer = state_lru, l
            o_lru, st = _lru(z, h0, h0_layer, lru_conv_w[l], lru_conv_b[l], lru_lambda[l],
                             wa_l, lru_ba[l], wx_l, lru_bx[l], nb, seq)
            if gi == 0:
                for lst, a in zip(new_cache, (gk, gv, wk, wv, st)):
                    lst.append(a)
            merged = _merge((o_pool, o_g, o_w, o_lru), w_br_l, z)
            mix = _matmul(merged, w_o, None, 2048, 512, "out_proj", layer=l, single_buffer_x=True)
            latent = gi == 1
            x1, h2 = _post(g["x"], mix, ada3[l], 2, ln1_g[l], ln1_b[l], nb, seq, latent,
                           nxt=(ada3[l], 3, 4), h_interleaved=True)
            act = _up(h2, w_up, ffn_conv_w, ffn_conv_b, l, seq)
            ff = _matmul(act, w_down_l, None, 512, 512, "ffn_down")
            nxt = (ada3[l + 1], 0, 1) if l + 1 < DEPTH else None
            g["x"], g["h"] = _post(x1, ff, ada3[l], 5, ln2_g[l], ln2_b[l], nb, seq, latent,
                                   nxt=nxt, y_interleaved=True)

    y_prompt = groups[0]["x"].reshape(nb_c, seq_c, D_MODEL)
    y_sample = groups[1]["x"].reshape(nb_s, seq_s, D_MODEL)
    kv = [jnp.stack([a.reshape(nb_c, seq_c, N_KV, HEAD_DIM) for a in new_cache[i]], axis=1) for i in range(4)]
    new_lru = jnp.stack(new_cache[4], axis=1)
    return (y_prompt, y_sample, kv[0], kv[1], kv[2], kv[3], new_lru)
```

```python
import functools

import jax
import jax.numpy as jnp
from jax import lax
from jax.experimental import pallas as pl
from jax.experimental.pallas import tpu as pltpu

F32 = jnp.float32
BF16 = jnp.bfloat16

D_MODEL = 4096
DEPTH = 2
GRID_W = 64
BRANCH_W = D_MODEL // 4
N_BRANCH = 4
HEAD_DIM = 128
N_HEADS = BRANCH_W // HEAD_DIM
N_KV = N_HEADS // 4
N_GROUP = N_HEADS // N_KV
Q_W = N_HEADS * HEAD_DIM
KV_W = N_KV * HEAD_DIM
WINDOW = 128
ROPE_THETA = 10000.0
ATTN_SCALE = HEAD_DIM ** -0.5
NEG_INF = -1e30
POOL_WINDOWS = (2, 4, 8, 16)
POOL_GW = BRANCH_W // len(POOL_WINDOWS)
LRU_W = BRANCH_W
LRU_BW = 128
LRU_C = 8.0
D_FF = 11008
ALPHA = (2 * DEPTH) ** 0.25
IN_SIZES = (BRANCH_W, Q_W, KV_W, KV_W, Q_W, KV_W, KV_W, LRU_W, LRU_W, N_BRANCH * D_MODEL)
IN_W = sum(IN_SIZES)

OFF_POOL = 0
OFF_QG = OFF_POOL + BRANCH_W
OFF_KG = OFF_QG + Q_W
OFF_VG = OFF_KG + KV_W
OFF_QW = OFF_VG + KV_W
OFF_KW = OFF_QW + Q_W
OFF_VW = OFF_KW + KV_W
OFF_XL = OFF_VW + KV_W
OFF_YL = OFF_XL + LRU_W
OFF_GATE = OFF_YL + LRU_W

LANE = 128
SUBLANE = 8
ADA_ROWS = 16
GQ_W = N_GROUP * HEAD_DIM
VMEM_LIMIT = 56 << 20


def _cparams(*sem):
    return pltpu.CompilerParams(dimension_semantics=sem, vmem_limit_bytes=VMEM_LIMIT)


def _dot(a, b):
    return jnp.dot(a, b, preferred_element_type=F32)


def _dot_nt(a, b):
    return lax.dot_general(a, b, (((1,), (1,)), ((), ())), preferred_element_type=F32)


def _ln(x, eps=1e-6):
    mu = jnp.mean(x, axis=-1, keepdims=True)
    xc = x - mu
    var = jnp.mean(xc * xc, axis=-1, keepdims=True)
    return xc * lax.rsqrt(var + eps)


def _rms(x, g, eps=1e-6):
    return x * lax.rsqrt(jnp.mean(x * x, axis=-1, keepdims=True) + eps) * g


def _ada_kernel(c_ref, w_ref, b_ref, o_ref):
    c = c_ref[...]
    s = (c * jax.nn.sigmoid(c)).astype(BF16)
    o_ref[...] = _dot(s, w_ref[...].astype(BF16)) + b_ref[...]


def _ada(c_all, w_ada, b_ada, tn=512):
    n = w_ada.shape[-1]
    return pl.pallas_call(
        _ada_kernel,
        grid=(DEPTH, n // tn),
        in_specs=[pl.BlockSpec((ADA_ROWS, D_MODEL), lambda l, j: (0, 0)),
                  pl.BlockSpec((None, D_MODEL, tn), lambda l, j: (l, 0, j)),
                  pl.BlockSpec((None, 1, tn), lambda l, j: (l, 0, j))],
        out_specs=pl.BlockSpec((None, ADA_ROWS, tn), lambda l, j: (l, 0, j)),
        out_shape=jax.ShapeDtypeStruct((DEPTH, ADA_ROWS, n), F32),
        compiler_params=_cparams("parallel", "parallel"),
        name="ada",
    )(c_all, w_ada, b_ada.reshape(DEPTH, 1, n))


def _ada_spec(which, tm, rows_per_cond):
    if rows_per_cond is None:
        return pl.BlockSpec((None, 1, D_MODEL), lambda i: (which, 0, 0))
    return pl.BlockSpec((None, 1, D_MODEL), lambda i: ((1 + (i * tm) // rows_per_cond) * 6 + which, 0, 0))


def _lnmod_kernel(x_ref, sh_ref, sc_ref, h_ref):
    h_ref[...] = (_ln(x_ref[...]) * (1.0 + sc_ref[...]) + sh_ref[...]).astype(BF16)


def _lnmod(x, ada3, rows_per_cond, tm=256):
    m = x.shape[0]
    return pl.pallas_call(
        _lnmod_kernel,
        grid=(m // tm,),
        in_specs=[pl.BlockSpec((tm, D_MODEL), lambda i: (i, 0)),
                  _ada_spec(0, tm, rows_per_cond), _ada_spec(1, tm, rows_per_cond)],
        out_specs=pl.BlockSpec((tm, D_MODEL), lambda i: (i, 0)),
        out_shape=jax.ShapeDtypeStruct((m, D_MODEL), BF16),
        compiler_params=_cparams("parallel"),
        name="lnmod",
    )(x, ada3, ada3)


def _post_kernel(x_ref, y_ref, g_ref, lg_ref, lb_ref, sh_ref, sc_ref, xo_ref, h_ref, *, y_interleaved, h_interleaved):
    y = y_ref[...]
    if y_interleaved:
        y = jnp.swapaxes(y, 0, 1)
    xn = _ln(ALPHA * x_ref[...] + g_ref[...] * y) * lg_ref[...] + lb_ref[...]
    xo_ref[...] = xn
    if h_ref is not None:
        h = _ln(xn) * (1.0 + sc_ref[...]) + sh_ref[...]
        if h_interleaved:
            h = jnp.swapaxes(h, 0, 1)
        h_ref[...] = h.astype(BF16)


def _post_last_kernel(x_ref, y_ref, g_ref, lg_ref, lb_ref, xo_ref, *, y_interleaved):
    _post_kernel(x_ref, y_ref, g_ref, lg_ref, lb_ref, None, None, xo_ref, None,
                 y_interleaved=y_interleaved, h_interleaved=False)


POST_ROWS = 32


def _post(x, y, ada3, gate_idx, ln_g, ln_b, nb, seq, latent, nxt=None, y_interleaved=False, h_interleaved=False):
    nt = seq // SUBLANE
    nat_shape, int_shape = (nb, SUBLANE, nt, D_MODEL), (nb, nt, SUBLANE, D_MODEL)
    nat = pl.BlockSpec((None, SUBLANE, POST_ROWS, D_MODEL), lambda q, j: (q, 0, j, 0))
    itl = pl.BlockSpec((None, POST_ROWS, SUBLANE, D_MODEL), lambda q, j: (q, j, 0, 0))
    vec = pl.BlockSpec((1, D_MODEL), lambda q, j: (0, 0))
    ada = lambda which: pl.BlockSpec((None, 1, D_MODEL),
                                     lambda q, j: (((1 + q) * 6 if latent else 0) + which, 0, 0))
    in_specs = [nat, itl if y_interleaved else nat, ada(gate_idx), vec, vec]
    args = [x.reshape(nat_shape), y.reshape(int_shape if y_interleaved else nat_shape), ada3,
            ln_g.reshape(1, D_MODEL), ln_b.reshape(1, D_MODEL)]
    grid = (nb, nt // POST_ROWS)
    if nxt is None:
        xo = pl.pallas_call(
            functools.partial(_post_last_kernel, y_interleaved=y_interleaved),
            grid=grid, in_specs=in_specs, out_specs=nat,
            out_shape=jax.ShapeDtypeStruct(nat_shape, F32),
            compiler_params=_cparams("parallel", "parallel"), name="post_last",
        )(*args)
        return xo.reshape(nb * seq, D_MODEL), None
    nxt_ada3, sh_idx, sc_idx = nxt
    in_specs += [ada(sh_idx), ada(sc_idx)]
    args += [nxt_ada3, nxt_ada3]
    xo, h = pl.pallas_call(
        functools.partial(_post_kernel, y_interleaved=y_interleaved, h_interleaved=h_interleaved),
        grid=grid, in_specs=in_specs, out_specs=(nat, itl if h_interleaved else nat),
        out_shape=(jax.ShapeDtypeStruct(nat_shape, F32),
                   jax.ShapeDtypeStruct(int_shape if h_interleaved else nat_shape, BF16)),
        compiler_params=_cparams("parallel", "parallel"), name="post",
    )(*args)
    return xo.reshape(nb * seq, D_MODEL), h.reshape(nb * seq, D_MODEL)


def _mm_bias_kernel(x_ref, w_ref, b_ref, o_ref):
    o_ref[...] = (_dot(x_ref[...], w_ref[...].astype(BF16)) + b_ref[...]).astype(o_ref.dtype)


def _mm_kernel(x_ref, w_ref, o_ref):
    o_ref[...] = _dot(x_ref[...], w_ref[...].astype(BF16)).astype(o_ref.dtype)


def _matmul(x, w, b, tm, tn, name, layer=None, single_buffer_x=False):
    m, k = x.shape
    n = w.shape[-1]
    tm = min(tm, m)
    x_mode = dict(pipeline_mode=pl.Buffered(1)) if single_buffer_x else {}
    if layer is None:
        w_spec = pl.BlockSpec((k, tn), lambda i, j: (0, j))
    else:
        w_spec = pl.BlockSpec((None, k, tn), lambda i, j: (layer, 0, j))
    in_specs = [pl.BlockSpec((tm, k), lambda i, j: (i, 0), **x_mode), w_spec]
    args = [x, w]
    body = _mm_kernel
    if b is not None:
        in_specs.append(pl.BlockSpec((1, tn), lambda i, j: (0, j)))
        args.append(b.reshape(1, n))
        body = _mm_bias_kernel
    return pl.pallas_call(
        body, grid=(m // tm, n // tn), in_specs=in_specs,
        out_specs=pl.BlockSpec((tm, tn), lambda i, j: (i, j)),
        out_shape=jax.ShapeDtypeStruct((m, n), F32),
        compiler_params=_cparams("parallel", "arbitrary"), name=name,
    )(*args)


def _cast_kernel(w_ref, o_ref):
    o_ref[...] = w_ref[...].astype(BF16)


def _cast_layer(w, layer, rows_per_step):
    _, r, c = w.shape
    return pl.pallas_call(
        _cast_kernel,
        grid=(r // rows_per_step,),
        in_specs=[pl.BlockSpec((None, rows_per_step, c), lambda i: (layer, i, 0))],
        out_specs=pl.BlockSpec((rows_per_step, c), lambda i: (i, 0)),
        out_shape=jax.ShapeDtypeStruct((r, c), BF16),
        compiler_params=_cparams("parallel"), name="cast_w",
    )(w)


POOL_ROW_BLOCK = 256
POOL_HALO = LANE


def _pool_kernel(u_ref, w_ref, sc_ref, o_ref):
    seq = u_ref.shape[0]
    rb = min(seq, POOL_ROW_BLOCK)
    for r0 in range(0, seq, rb):
        k0, k1 = max(0, r0 - POOL_HALO), min(seq, r0 + rb + POOL_HALO)
        t = r0 + lax.broadcasted_iota(jnp.int32, (rb, k1 - k0), 0)
        s = k0 + lax.broadcasted_iota(jnp.int32, (rb, k1 - k0), 1)
        d = s - t
        tcol = r0 + lax.broadcasted_iota(jnp.int32, (rb, 1), 0)
        for gi, win in enumerate(POOL_WINDOWS):
            half = win // 2
            cols = slice(gi * POOL_GW, (gi + 1) * POOL_GW)
            band = jnp.where((d >= -half) & (d < half), 1.0, 0.0).astype(BF16)
            cnt = jnp.minimum(tcol + half, seq) - jnp.maximum(tcol - half, 0)
            keys = u_ref[k0:k1, cols]
            k_hi = keys.astype(BF16)
            k_lo = (keys - k_hi.astype(F32)).astype(BF16)
            mean = (_dot(band, k_hi) + _dot(band, k_lo)) / cnt.astype(F32)
            y = _dot((mean - u_ref[r0:r0 + rb, cols]).astype(BF16), w_ref[gi])
            o_ref[r0:r0 + rb, cols] = (y * sc_ref[:, cols]).astype(BF16)


def _pool(z, pool_w, pool_scale, nb, seq):
    return pl.pallas_call(
        _pool_kernel,
        grid=(nb,),
        in_specs=[pl.BlockSpec((seq, BRANCH_W), lambda b: (b, OFF_POOL // BRANCH_W)),
                  pl.BlockSpec((len(POOL_WINDOWS), POOL_GW, POOL_GW), lambda b: (0, 0, 0)),
                  pl.BlockSpec((1, BRANCH_W), lambda b: (0, 0))],
        out_specs=pl.BlockSpec((seq, BRANCH_W), lambda b: (b, 0)),
        out_shape=jax.ShapeDtypeStruct((nb * seq, BRANCH_W), BF16),
        compiler_params=_cparams("parallel"), name="pool",
    )(z, pool_w, pool_scale.reshape(1, BRANCH_W))


def _softmax_attend(q, k, v, sink):
    s = _dot_nt(q, k) * ATTN_SCALE
    m = jnp.max(s, axis=-1, keepdims=True)
    if sink is not None:
        m = jnp.maximum(m, sink)
    e = jnp.exp(s - m)
    den = jnp.sum(e, axis=-1, keepdims=True)
    if sink is not None:
        den = den + jnp.exp(sink - m)
    return _dot(e.astype(BF16), v) * (1.0 / den)


def _ctx_attn_kernel(sink_ref, qg_ref, kg_ref, vg_ref, qw_ref, kw_ref, vw_ref, qn_ref, kn_ref,
                     og_ref, ow_ref, gk_ref, gv_ref, wk_ref, wv_ref):
    kvh = pl.program_id(1)
    kg = _rms(kg_ref[...], kn_ref[...])
    gk_ref[...] = kg
    gv_ref[...] = vg_ref[...]
    wk_ref[...] = kw_ref[...]
    wv_ref[...] = vw_ref[...]
    kg_b, vg_b = kg.astype(BF16), vg_ref[...].astype(BF16)
    kw_b, vw_b = kw_ref[...].astype(BF16), vw_ref[...].astype(BF16)
    for h in range(N_GROUP):
        cols = slice(h * HEAD_DIM, (h + 1) * HEAD_DIM)
        qg = _rms(qg_ref[:, cols], qn_ref[...]).astype(BF16)
        og_ref[:, cols] = _softmax_attend(qg, kg_b, vg_b, None).astype(BF16)
        sink = sink_ref[kvh * N_GROUP + h]
        ow_ref[:, cols] = _softmax_attend(qw_ref[:, cols].astype(BF16), kw_b, vw_b, sink).astype(BF16)


def _ctx_attn(z, sink, qn_g, kn_g, nb, seq):
    q_spec = lambda off: pl.BlockSpec((seq, GQ_W), lambda b, k: (b, off // GQ_W + k))
    kv_spec = lambda off: pl.BlockSpec((seq, HEAD_DIM), lambda b, k: (b, off // HEAD_DIM + k))
    vec = pl.BlockSpec((1, HEAD_DIM), lambda b, k: (0, 0))
    o_spec = pl.BlockSpec((seq, GQ_W), lambda b, k: (b, k))
    c_spec = pl.BlockSpec((seq, HEAD_DIM), lambda b, k: (b, k))
    m = nb * seq
    return pl.pallas_call(
        _ctx_attn_kernel,
        grid=(nb, N_KV),
        in_specs=[pl.BlockSpec(memory_space=pltpu.SMEM),
                  q_spec(OFF_QG), kv_spec(OFF_KG), kv_spec(OFF_VG),
                  q_spec(OFF_QW), kv_spec(OFF_KW), kv_spec(OFF_VW), vec, vec],
        out_specs=(o_spec, o_spec, c_spec, c_spec, c_spec, c_spec),
        out_shape=(jax.ShapeDtypeStruct((m, Q_W), BF16), jax.ShapeDtypeStruct((m, Q_W), BF16))
        + tuple(jax.ShapeDtypeStruct((m, KV_W), F32) for _ in range(4)),
        compiler_params=_cparams("parallel", "parallel"), name="ctx_attn",
    )(sink, z, z, z, z, z, z, qn_g.reshape(1, HEAD_DIM), kn_g.reshape(1, HEAD_DIM))


def _rope(x, cos, sin, low_half):
    partner = jnp.where(low_half, pltpu.roll(x, HEAD_DIM - 32, axis=1), pltpu.roll(x, 32, axis=1))
    return x * cos + partner * sin


def _low_half(rows):
    lane = lax.broadcasted_iota(jnp.int32, (rows, HEAD_DIM), 1)
    return (lane & 63) < 32


def _lat_global_kernel(q_ref, k_ref, v_ref, ck_ref, cv_ref, cos_ref, sin_ref, qn_ref, kn_ref,
                       o_ref, kall, vall, *, tq):
    seq = q_ref.shape[0]
    past = ck_ref.shape[0]
    kall[0:past, :] = ck_ref[...].astype(BF16)
    vall[0:past, :] = cv_ref[...].astype(BF16)
    kn = _rope(_rms(k_ref[...], kn_ref[...]), cos_ref[...], sin_ref[...], _low_half(seq))
    kall[past:past + seq, :] = kn.astype(BF16)
    vall[past:past + seq, :] = v_ref[...].astype(BF16)
    low = _low_half(tq)

    def q_block(qb, carry):
        rows = pl.ds(pl.multiple_of(qb * tq, tq), tq)
        cos, sin = cos_ref[rows, :], sin_ref[rows, :]
        for h in range(N_GROUP):
            cols = slice(h * HEAD_DIM, (h + 1) * HEAD_DIM)
            q = _rope(_rms(q_ref[rows, cols], qn_ref[...]), cos, sin, low).astype(BF16)
            o_ref[rows, cols] = _softmax_attend(q, kall[...], vall[...], None).astype(BF16)
        return carry

    lax.fori_loop(0, seq // tq, q_block, 0)


def _lat_global(z, cache_k, cache_v, layer, cos, sin, qn_g, kn_g, nb, seq, tq=256):
    past = cache_k.shape[2]
    q_spec = pl.BlockSpec((seq, GQ_W), lambda b, k: (b, OFF_QG // GQ_W + k))
    kv_spec = lambda off: pl.BlockSpec((seq, HEAD_DIM), lambda b, k: (b, off // HEAD_DIM + k))
    c_spec = pl.BlockSpec((None, None, past, HEAD_DIM), lambda b, k: (b, layer, 0, k))
    tab = pl.BlockSpec((seq, HEAD_DIM), lambda b, k: (0, 0))
    vec = pl.BlockSpec((1, HEAD_DIM), lambda b, k: (0, 0))
    return pl.pallas_call(
        functools.partial(_lat_global_kernel, tq=tq),
        grid=(nb, N_KV),
        in_specs=[q_spec, kv_spec(OFF_KG), kv_spec(OFF_VG), c_spec, c_spec, tab, tab, vec, vec],
        out_specs=pl.BlockSpec((seq, GQ_W), lambda b, k: (b, k)),
        out_shape=jax.ShapeDtypeStruct((nb * seq, Q_W), BF16),
        scratch_shapes=[pltpu.VMEM((past + seq, HEAD_DIM), BF16), pltpu.VMEM((past + seq, HEAD_DIM), BF16)],
        compiler_params=_cparams("parallel", "parallel"), name="lat_global",
    )(z, z, z, cache_k, cache_v, cos, sin, qn_g.reshape(1, HEAD_DIM), kn_g.reshape(1, HEAD_DIM))


def _lat_window_kernel(sink_ref, q_ref, k_ref, v_ref, ck_ref, cv_ref, cos_ref, sin_ref,
                       o_ref, kpad, vpad):
    seq = q_ref.shape[0]
    blk = WINDOW
    kvh = pl.program_id(1)
    zeros = jnp.zeros((blk, HEAD_DIM), BF16)
    kpad[0:blk, :] = zeros
    vpad[0:blk, :] = zeros
    kpad[blk + seq:2 * blk + seq, :] = zeros
    vpad[blk + seq:2 * blk + seq, :] = zeros
    kpad[blk:blk + seq, :] = _rope(k_ref[...], cos_ref[...], sin_ref[...], _low_half(seq)).astype(BF16)
    vpad[blk:blk + seq, :] = v_ref[...].astype(BF16)
    ck = ck_ref[...].astype(BF16)
    cv = cv_ref[...].astype(BF16)
    low = _low_half(blk)
    rows4 = N_GROUP * blk
    head = lax.broadcasted_iota(jnp.int32, (rows4, 1), 0) // blk
    sink = jnp.zeros((rows4, 1), F32)
    for h in range(N_GROUP):
        sink = jnp.where(head == h, sink_ref[kvh * N_GROUP + h], sink)
    qi = lax.broadcasted_iota(jnp.int32, (rows4, 3 * blk), 0) & (blk - 1)
    kj = lax.broadcasted_iota(jnp.int32, (rows4, 3 * blk), 1)
    in_band = jnp.abs(qi + blk - kj) <= WINDOW

    def q_block(n, carry):
        r0 = pl.multiple_of(n * blk, blk)
        rows = pl.ds(r0, blk)
        cos, sin = cos_ref[rows, :], sin_ref[rows, :]
        q4 = jnp.concatenate(
            [_rope(q_ref[rows, h * HEAD_DIM:(h + 1) * HEAD_DIM], cos, sin, low) for h in range(N_GROUP)],
            axis=0).astype(BF16)
        tk = (n - 1) * blk + kj
        valid = in_band & (tk >= 0) & (tk < seq)
        s_loc = jnp.where(valid, _dot_nt(q4, kpad[pl.ds(r0, 3 * blk), :]) * ATTN_SCALE, NEG_INF)
        s_ctx = _dot_nt(q4, ck) * ATTN_SCALE
        m = jnp.maximum(jnp.maximum(jnp.max(s_loc, axis=-1, keepdims=True),
                                    jnp.max(s_ctx, axis=-1, keepdims=True)), sink)
        e_loc = jnp.exp(s_loc - m)
        e_ctx = jnp.exp(s_ctx - m)
        den = (jnp.sum(e_loc, axis=-1, keepdims=True) + jnp.sum(e_ctx, axis=-1, keepdims=True)
               + jnp.exp(sink - m))
        o4 = (_dot(e_loc.astype(BF16), vpad[pl.ds(r0, 3 * blk), :]) + _dot(e_ctx.astype(BF16), cv)) * (1.0 / den)
        for h in range(N_GROUP):
            o_ref[rows, h * HEAD_DIM:(h + 1) * HEAD_DIM] = o4[h * blk:(h + 1) * blk].astype(BF16)
        return carry

    lax.fori_loop(0, seq // blk, q_block, 0)


def _lat_window(z, cache_k, cache_v, layer, cos, sin, sink, nb, seq):
    past = cache_k.shape[2]
    q_spec = pl.BlockSpec((seq, GQ_W), lambda b, k: (b, OFF_QW // GQ_W + k))
    kv_spec = lambda off: pl.BlockSpec((seq, HEAD_DIM), lambda b, k: (b, off // HEAD_DIM + k))
    c_spec = pl.BlockSpec((None, None, past, HEAD_DIM), lambda b, k: (b, layer, 0, k))
    tab = pl.BlockSpec((seq, HEAD_DIM), lambda b, k: (0, 0))
    return pl.pallas_call(
        _lat_window_kernel,
        grid=(nb, N_KV),
        in_specs=[pl.BlockSpec(memory_space=pltpu.SMEM), q_spec, kv_spec(OFF_KW), kv_spec(OFF_VW),
                  c_spec, c_spec, tab, tab],
        out_specs=pl.BlockSpec((seq, GQ_W), lambda b, k: (b, k)),
        out_shape=jax.ShapeDtypeStruct((nb * seq, Q_W), BF16),
        scratch_shapes=[pltpu.VMEM((seq + 2 * WINDOW, HEAD_DIM), BF16),
                        pltpu.VMEM((seq + 2 * WINDOW, HEAD_DIM), BF16)],
        compiler_params=_cparams("parallel", "parallel"), name="lat_window",
    )(sink, z, z, z, cache_k, cache_v, cos, sin)


def _interleave(v, n_tiles):
    return jnp.swapaxes(v.reshape(SUBLANE, n_tiles, v.shape[-1]), 0, 1).reshape(v.shape)


def _deinterleave(v, n_tiles):
    return jnp.swapaxes(v.reshape(n_tiles, SUBLANE, v.shape[-1]), 0, 1).reshape(v.shape)


def _lru_kernel(x_ref, y_ref, cw_ref, cb_ref, lam_ref, wa_ref, ba_ref, wx_ref, bx_ref, h0_ref,
                o_ref, st_ref, a_f, b_f, a_b, b_b):
    seq, width = x_ref.shape
    n_tiles = seq // SUBLANE
    sub = lax.broadcasted_iota(jnp.int32, (SUBLANE, width), 0)

    def from_prev_segment(tile):
        return jnp.where(sub >= 1, pltpu.roll(tile, 1, axis=0), 0.0)

    def from_next_segment(tile):
        return jnp.where(sub < SUBLANE - 1, pltpu.roll(tile, SUBLANE - 1, axis=0), 0.0)

    x = _interleave(x_ref[...], n_tiles)
    last, last2, first = x[seq - SUBLANE:], x[seq - 2 * SUBLANE:seq - SUBLANE], x[:SUBLANE]
    xm1 = jnp.concatenate([from_prev_segment(last), x[:seq - SUBLANE]], axis=0)
    xm2 = jnp.concatenate([from_prev_segment(last2), from_prev_segment(last), x[:seq - 2 * SUBLANE]], axis=0)
    xp1 = jnp.concatenate([x[SUBLANE:], from_next_segment(first)], axis=0)
    xc = (cb_ref[...] + xm2 * cw_ref[0:1, :] + xm1 * cw_ref[1:2, :] + x * cw_ref[2:3, :] + xp1 * cw_ref[3:4, :])
    xcb = xc.astype(BF16)

    def block_diag(w_ref, d):
        return jnp.concatenate(
            [_dot(xcb[:, j * LRU_BW:(j + 1) * LRU_BW], w_ref[d, j]) for j in range(width // LRU_BW)], axis=1)

    for d, (a_s, b_s) in enumerate(((a_f, b_f), (a_b, b_b))):
        r = jax.nn.sigmoid(block_diag(wa_ref, d) + ba_ref[d:d + 1, :])
        gate_i = jax.nn.sigmoid(block_diag(wx_ref, d) + bx_ref[d:d + 1, :])
        neg_lam = -lam_ref[d:d + 1, :]
        softplus = jnp.maximum(neg_lam, 0.0) + jnp.log1p(jnp.exp(-jnp.abs(neg_lam)))
        log_a = -LRU_C * r * softplus
        a_s[...] = jnp.exp(log_a)
        th = jnp.tanh(log_a)
        b_s[...] = jnp.sqrt(-2.0 * th / (1.0 - th)) * (gate_i * xc)

    def local_step(i, carry):
        hf, pf, hb, pb = carry
        rf = pl.ds(pl.multiple_of(i * SUBLANE, SUBLANE), SUBLANE)
        rb = pl.ds(pl.multiple_of((n_tiles - 1 - i) * SUBLANE, SUBLANE), SUBLANE)
        af, ab = a_f[rf, :], a_b[rb, :]
        hf, pf = af * hf + b_f[rf, :], af * pf
        hb, pb = ab * hb + b_b[rb, :], ab * pb
        a_f[rf, :], b_f[rf, :] = pf, hf
        a_b[rb, :], b_b[rb, :] = pb, hb
        return hf, pf, hb, pb

    zeros, ones = jnp.zeros((SUBLANE, width), F32), jnp.ones((SUBLANE, width), F32)
    hf, pf, hb, pb = lax.fori_loop(0, n_tiles, local_step, (zeros, ones, zeros, ones), unroll=8)

    for s in (1, 2, 4):
        take_f, take_b = sub >= s, sub < SUBLANE - s
        hf = jnp.where(take_f, pf * pltpu.roll(hf, s, axis=0) + hf, hf)
        pf = jnp.where(take_f, pf * pltpu.roll(pf, s, axis=0), pf)
        hb = jnp.where(take_b, pb * pltpu.roll(hb, SUBLANE - s, axis=0) + hb, hb)
        pb = jnp.where(take_b, pb * pltpu.roll(pb, SUBLANE - s, axis=0), pb)
    h0f = jnp.broadcast_to(h0_ref[0:1, :], (SUBLANE, width))
    h0b = jnp.broadcast_to(h0_ref[1:2, :], (SUBLANE, width))
    out_f = pf * h0f + hf
    out_b = pb * h0b + hb
    st_ref[0:1, :] = out_f[SUBLANE - 1:SUBLANE, :]
    st_ref[1:2, :] = out_b[0:1, :]
    in_f = jnp.where(sub >= 1, pltpu.roll(out_f, 1, axis=0), h0f)
    in_b = jnp.where(sub < SUBLANE - 1, pltpu.roll(out_b, SUBLANE - 1, axis=0), h0b)

    def tiles(ref):
        return ref[...].reshape(n_tiles, SUBLANE, width)

    h = (tiles(b_f) + tiles(a_f) * in_f[None]) + (tiles(b_b) + tiles(a_b) * in_b[None])
    h = _deinterleave(h.reshape(seq, width), n_tiles)
    o_ref[...] = (h * jax.nn.gelu(y_ref[...])).astype(BF16)


def _lru(z, h0, h0_layer, conv_w, conv_b, lam, wa, ba, wx, bx, nb, seq, cw=256):
    nc = LRU_W // cw
    nd = cw // LRU_BW
    vec2 = pl.BlockSpec((2, cw), lambda b, c: (0, c))
    wspec = pl.BlockSpec((2, nd, LRU_BW, LRU_BW), lambda b, c: (0, c, 0, 0))
    return pl.pallas_call(
        _lru_kernel,
        grid=(nb, nc),
        in_specs=[pl.BlockSpec((seq, cw), lambda b, c: (b, OFF_XL // cw + c)),
                  pl.BlockSpec((seq, cw), lambda b, c: (b, OFF_YL // cw + c)),
                  pl.BlockSpec((4, cw), lambda b, c: (0, c)),
                  pl.BlockSpec((1, cw), lambda b, c: (0, c)),
                  vec2, wspec, vec2, wspec, vec2,
                  pl.BlockSpec((None, None, 2, cw), lambda b, c: (b, h0_layer, 0, c))],
        out_specs=(pl.BlockSpec((seq, cw), lambda b, c: (b, c)),
                   pl.BlockSpec((None, 2, cw), lambda b, c: (b, 0, c))),
        out_shape=(jax.ShapeDtypeStruct((nb * seq, LRU_W), BF16), jax.ShapeDtypeStruct((nb, 2, LRU_W), F32)),
        scratch_shapes=[pltpu.VMEM((seq, cw), F32)] * 4,
        compiler_params=_cparams("parallel", "parallel"), name="lru",
    )(z, z, conv_w, conv_b.reshape(1, LRU_W), lam, wa, ba, wx, bx, h0)


MERGE_ROW_CHUNK = 128


def _merge_kernel(b0, b1, b2, b3, w_ref, g0, g1, g2, g3, o_ref):
    for r in range(0, o_ref.shape[0], MERGE_ROW_CHUNK):
        rows = slice(r, r + MERGE_ROW_CHUNK)
        acc = None
        for n, (br, gr) in enumerate(((b0, g0), (b1, g1), (b2, g2), (b3, g3))):
            t = jax.nn.sigmoid(gr[rows, :]) * _dot(br[rows, :], w_ref[n])
            acc = t if acc is None else acc + t
        o_ref[rows, :] = acc.astype(BF16)


def _merge(branches, w_br, z, tm=1024, tn=256):
    m = z.shape[0]
    br = pl.BlockSpec((tm, BRANCH_W), lambda i, j: (i, 0))
    gate = lambda n: pl.BlockSpec((tm, tn), lambda i, j: (i, (OFF_GATE + n * D_MODEL) // tn + j))
    return pl.pallas_call(
        _merge_kernel,
        grid=(m // tm, D_MODEL // tn),
        in_specs=[br, br, br, br, pl.BlockSpec((N_BRANCH, BRANCH_W, tn), lambda i, j: (0, 0, j)),
                  gate(0), gate(1), gate(2), gate(3)],
        out_specs=pl.BlockSpec((tm, tn), lambda i, j: (i, j)),
        out_shape=jax.ShapeDtypeStruct((m, D_MODEL), BF16),
        compiler_params=_cparams("parallel", "arbitrary"), name="merge",
    )(*branches, w_br, z, z, z, z)


UP_ROW_CHUNK = 128


def _up_kernel(x_ref, wu_ref, wg_ref, cwu_ref, cwg_ref, cbu_ref, cbg_ref, o_ref, *, seq):
    rows, tn = o_ref.shape
    sub = lax.broadcasted_iota(jnp.int32, (SUBLANE, tn), 0)

    def conv(v, w_ref, b_ref):
        prev, nxt = [], []
        for q in range(rows // seq):
            vq = v[q * seq:(q + 1) * seq]
            head, tail = vq[:SUBLANE], vq[seq - SUBLANE:]
            prev += [jnp.where(sub >= 1, pltpu.roll(tail, 1, axis=0), 0.0), vq[:seq - SUBLANE]]
            nxt += [vq[SUBLANE:], jnp.where(sub < SUBLANE - 1, pltpu.roll(head, SUBLANE - 1, axis=0), 0.0)]
        prev, nxt = jnp.concatenate(prev, axis=0), jnp.concatenate(nxt, axis=0)
        return b_ref[...] + prev * w_ref[0:1, :] + v * w_ref[1:2, :] + nxt * w_ref[2:3, :]

    wu, wg = wu_ref[...].astype(BF16), wg_ref[...].astype(BF16)

    def row_chunked_dot(w):
        return jnp.concatenate([_dot(x_ref[r:r + UP_ROW_CHUNK, :], w) for r in range(0, rows, UP_ROW_CHUNK)],
                               axis=0)

    u = conv(row_chunked_dot(wu), cwu_ref, cbu_ref)
    g = conv(row_chunked_dot(wg), cwg_ref, cbg_ref)
    o_ref[...] = (g * jax.nn.sigmoid(g) * u).astype(BF16)


def _up(h, w_up, conv_w, conv_b, layer, seq, tm=1024, tn=256):
    m = h.shape[0]
    nj = D_FF // tn
    w_spec = lambda off: pl.BlockSpec((None, D_MODEL, tn), lambda i, j: (layer, 0, off + j))
    cw_spec = lambda off: pl.BlockSpec((None, 3, tn), lambda i, j: (layer, 0, off + j))
    cb_spec = lambda off: pl.BlockSpec((None, 1, tn), lambda i, j: (layer, 0, off + j))
    conv_b3 = conv_b.reshape(DEPTH, 1, 2 * D_FF)
    return pl.pallas_call(
        functools.partial(_up_kernel, seq=seq),
        grid=(m // tm, nj),
        in_specs=[pl.BlockSpec((tm, D_MODEL), lambda i, j: (i, 0)), w_spec(0), w_spec(nj),
                  cw_spec(0), cw_spec(nj), cb_spec(0), cb_spec(nj)],
        out_specs=pl.BlockSpec((tm, tn), lambda i, j: (i, j)),
        out_shape=jax.ShapeDtypeStruct((m, D_FF), BF16),
        compiler_params=_cparams("parallel", "arbitrary"), name="ffn_up",
    )(h, w_up, w_up, conv_w, conv_w, conv_b3, conv_b3)


def _rope_tables(seq):
    t = jnp.arange(seq)
    half = HEAD_DIM // 2
    inv = ROPE_THETA ** (-jnp.arange(0, half, 2, dtype=F32) / half)
    ang_r = (t // GRID_W).astype(F32)[:, None] * inv[None, :]
    ang_c = (t % GRID_W).astype(F32)[:, None] * inv[None, :]
    cos = jnp.concatenate([jnp.cos(ang_r), jnp.cos(ang_r), jnp.cos(ang_c), jnp.cos(ang_c)], axis=1)
    sin = jnp.concatenate([-jnp.sin(ang_r), jnp.sin(ang_r), -jnp.sin(ang_c), jnp.sin(ang_c)], axis=1)
    return cos, sin


def kernel(x_prompt, x_sample, cache_gk, cache_gv, cache_wk, cache_wv, state_lru, c, c_ctx, w_ada, b_ada, w_in, b_in, pool_w, pool_scale, qn_g, kn_g, sink, lru_conv_w, lru_conv_b, lru_lambda, lru_wa, lru_ba, lru_wx, lru_bx, w_br, w_o, ln1_g, ln1_b, w_up, ffn_conv_w, ffn_conv_b, w_down, ln2_g, ln2_b):
    nb_c, seq_c, _ = x_prompt.shape
    nb_s, seq_s, _ = x_sample.shape
    past = cache_gk.shape[2]

    c_all = jnp.concatenate([c_ctx[None, :], c, jnp.zeros((ADA_ROWS - 1 - nb_s, D_MODEL), F32)], axis=0)
    ada = _ada(c_all, w_ada, b_ada)
    ada3 = [ada[l].reshape(ADA_ROWS * 6, 1, D_MODEL) for l in range(DEPTH)]
    cos, sin = _rope_tables(seq_s)
    caches = [a.reshape(nb_s, DEPTH, past, KV_W) for a in (cache_gk, cache_gv, cache_wk, cache_wv)]
    zero_state = jnp.zeros((nb_c, 1, 2, LRU_W), F32)

    groups = [dict(x=x_prompt.reshape(nb_c * seq_c, D_MODEL), nb=nb_c, seq=seq_c, cond=None),
              dict(x=x_sample.reshape(nb_s * seq_s, D_MODEL), nb=nb_s, seq=seq_s, cond=seq_s)]
    for g in groups:
        g["h"] = _lnmod(g["x"], ada3[0], g["cond"])
    new_cache = [[] for _ in range(5)]

    for l in range(DEPTH):
        w_br_l = _cast_layer(w_br.reshape(DEPTH, N_BRANCH * BRANCH_W, D_MODEL), l, 512).reshape(
            N_BRANCH, BRANCH_W, D_MODEL)
        w_down_l = _cast_layer(w_down, l, D_FF // 16)
        pool_w_l = pool_w[l].astype(BF16)
        wa_l = lru_wa[l].astype(BF16)
        wx_l = lru_wx[l].astype(BF16)
        for gi, g in enumerate(groups):
            nb, seq = g["nb"], g["seq"]
            z = _matmul(g["h"], w_in, b_in[l], 2048, 512, "in_proj", layer=l, single_buffer_x=True)
            o_pool = _pool(z, pool_w_l, pool_scale[l], nb, seq)
            if gi == 0:
                o_g, o_w, gk, gv, wk, wv = _ctx_attn(z, sink[l], qn_g[l], kn_g[l], nb, seq)
                h0, h0_layer = zero_state, 0
            else:
                o_g = _lat_global(z, caches[0], caches[1], l, cos, sin, qn_g[l], kn_g[l], nb, seq)
                o_w = _lat_window(z, caches[2], caches[3], l, cos, sin, sink[l], nb, seq)
                h0, h0_layer = state_lru, l
            o_lru, st = _lru(z, h0, h0_layer, lru_conv_w[l], lru_conv_b[l], lru_lambda[l],
                             wa_l, lru_ba[l], wx_l, lru_bx[l], nb, seq)
            if gi == 0:
                for lst, a in zip(new_cache, (gk, gv, wk, wv, st)):
                    lst.append(a)
            merged = _merge((o_pool, o_g, o_w, o_lru), w_br_l, z)
            mix = _matmul(merged, w_o, None, 2048, 512, "out_proj", layer=l, single_buffer_x=True)
            latent = gi == 1
            x1, h2 = _post(g["x"], mix, ada3[l], 2, ln1_g[l], ln1_b[l], nb, seq, latent,
                           nxt=(ada3[l], 3, 4), h_interleaved=True)
            act = _up(h2, w_up, ffn_conv_w, ffn_conv_b, l, seq)
            ff = _matmul(act, w_down_l, None, 512, 512, "ffn_down")
            nxt = (ada3[l + 1], 0, 1) if l + 1 < DEPTH else None
            g["x"], g["h"] = _post(x1, ff, ada3[l], 5, ln2_g[l], ln2_b[l], nb, seq, latent,
                                   nxt=nxt, y_interleaved=True)

    y_prompt = groups[0]["x"].reshape(nb_c, seq_c, D_MODEL)
    y_sample = groups[1]["x"].reshape(nb_s, seq_s, D_MODEL)
    kv = [jnp.stack([a.reshape(nb_c, seq_c, N_KV, HEAD_DIM) for a in new_cache[i]], axis=1) for i in range(4)]
    new_lru = jnp.stack(new_cache[4], axis=1)
    return (y_prompt, y_sample, kv[0], kv[1], kv[2], kv[3], new_lru)
```

```python
import functools

import jax
import jax.numpy as jnp
from jax import lax
from jax.experimental import pallas as pl
from jax.experimental.pallas import tpu as pltpu

F32 = jnp.float32
BF16 = jnp.bfloat16

D_MODEL = 4096
DEPTH = 2
GRID_W = 64
BRANCH_W = D_MODEL // 4
N_BRANCH = 4
HEAD_DIM = 128
N_HEADS = BRANCH_W // HEAD_DIM
N_KV = N_HEADS // 4
N_GROUP = N_HEADS // N_KV
Q_W = N_HEADS * HEAD_DIM
KV_W = N_KV * HEAD_DIM
WINDOW = 128
ROPE_THETA = 10000.0
ATTN_SCALE = HEAD_DIM ** -0.5
NEG_INF = -1e30
POOL_WINDOWS = (2, 4, 8, 16)
POOL_GW = BRANCH_W // len(POOL_WINDOWS)
LRU_W = BRANCH_W
LRU_BW = 128
LRU_C = 8.0
D_FF = 11008
ALPHA = (2 * DEPTH) ** 0.25
IN_SIZES = (BRANCH_W, Q_W, KV_W, KV_W, Q_W, KV_W, KV_W, LRU_W, LRU_W, N_BRANCH * D_MODEL)
IN_W = sum(IN_SIZES)

OFF_POOL = 0
OFF_QG = OFF_POOL + BRANCH_W
OFF_KG = OFF_QG + Q_W
OFF_VG = OFF_KG + KV_W
OFF_QW = OFF_VG + KV_W
OFF_KW = OFF_QW + Q_W
OFF_VW = OFF_KW + KV_W
OFF_XL = OFF_VW + KV_W
OFF_YL = OFF_XL + LRU_W
OFF_GATE = OFF_YL + LRU_W

LANE = 128
SUBLANE = 8
ADA_ROWS = 16
GQ_W = N_GROUP * HEAD_DIM
VMEM_LIMIT = 56 << 20


def _cparams(*sem):
    return pltpu.CompilerParams(dimension_semantics=sem, vmem_limit_bytes=VMEM_LIMIT)


def _dot(a, b):
    return jnp.dot(a, b, preferred_element_type=F32)


def _dot_nt(a, b):
    return lax.dot_general(a, b, (((1,), (1,)), ((), ())), preferred_element_type=F32)


def _ln(x, eps=1e-6):
    mu = jnp.mean(x, axis=-1, keepdims=True)
    xc = x - mu
    var = jnp.mean(xc * xc, axis=-1, keepdims=True)
    return xc * lax.rsqrt(var + eps)


def _rms(x, g, eps=1e-6):
    return x * lax.rsqrt(jnp.mean(x * x, axis=-1, keepdims=True) + eps) * g


def _ada_kernel(c_ref, w_ref, b_ref, o_ref):
    c = c_ref[...]
    s = (c * jax.nn.sigmoid(c)).astype(BF16)
    o_ref[...] = _dot(s, w_ref[...].astype(BF16)) + b_ref[...]


def _ada(c_all, w_ada, b_ada, tn=512):
    n = w_ada.shape[-1]
    return pl.pallas_call(
        _ada_kernel,
        grid=(DEPTH, n // tn),
        in_specs=[pl.BlockSpec((ADA_ROWS, D_MODEL), lambda l, j: (0, 0)),
                  pl.BlockSpec((None, D_MODEL, tn), lambda l, j: (l, 0, j)),
                  pl.BlockSpec((None, 1, tn), lambda l, j: (l, 0, j))],
        out_specs=pl.BlockSpec((None, ADA_ROWS, tn), lambda l, j: (l, 0, j)),
        out_shape=jax.ShapeDtypeStruct((DEPTH, ADA_ROWS, n), F32),
        compiler_params=_cparams("parallel", "parallel"),
        name="ada",
    )(c_all, w_ada, b_ada.reshape(DEPTH, 1, n))


def _ada_spec(which, tm, rows_per_cond):
    if rows_per_cond is None:
        return pl.BlockSpec((None, 1, D_MODEL), lambda i: (which, 0, 0))
    return pl.BlockSpec((None, 1, D_MODEL), lambda i: ((1 + (i * tm) // rows_per_cond) * 6 + which, 0, 0))


def _lnmod_kernel(x_ref, sh_ref, sc_ref, h_ref):
    h_ref[...] = (_ln(x_ref[...]) * (1.0 + sc_ref[...]) + sh_ref[...]).astype(BF16)


def _lnmod(x, ada3, rows_per_cond, tm=256):
    m = x.shape[0]
    return pl.pallas_call(
        _lnmod_kernel,
        grid=(m // tm,),
        in_specs=[pl.BlockSpec((tm, D_MODEL), lambda i: (i, 0)),
                  _ada_spec(0, tm, rows_per_cond), _ada_spec(1, tm, rows_per_cond)],
        out_specs=pl.BlockSpec((tm, D_MODEL), lambda i: (i, 0)),
        out_shape=jax.ShapeDtypeStruct((m, D_MODEL), BF16),
        compiler_params=_cparams("parallel"),
        name="lnmod",
    )(x, ada3, ada3)


def _post_kernel(x_ref, y_ref, g_ref, lg_ref, lb_ref, sh_ref, sc_ref, xo_ref, h_ref, *, y_interleaved, h_interleaved):
    y = y_ref[...]
    if y_interleaved:
        y = jnp.swapaxes(y, 0, 1)
    xn = _ln(ALPHA * x_ref[...] + g_ref[...] * y) * lg_ref[...] + lb_ref[...]
    xo_ref[...] = xn
    if h_ref is not None:
        h = _ln(xn) * (1.0 + sc_ref[...]) + sh_ref[...]
        if h_interleaved:
            h = jnp.swapaxes(h, 0, 1)
        h_ref[...] = h.astype(BF16)


def _post_last_kernel(x_ref, y_ref, g_ref, lg_ref, lb_ref, xo_ref, *, y_interleaved):
    _post_kernel(x_ref, y_ref, g_ref, lg_ref, lb_ref, None, None, xo_ref, None,
                 y_interleaved=y_interleaved, h_interleaved=False)


POST_ROWS = 32


def _post(x, y, ada3, gate_idx, ln_g, ln_b, nb, seq, latent, nxt=None, y_interleaved=False, h_interleaved=False):
    nt = seq // SUBLANE
    nat_shape, int_shape = (nb, SUBLANE, nt, D_MODEL), (nb, nt, SUBLANE, D_MODEL)
    nat = pl.BlockSpec((None, SUBLANE, POST_ROWS, D_MODEL), lambda q, j: (q, 0, j, 0))
    itl = pl.BlockSpec((None, POST_ROWS, SUBLANE, D_MODEL), lambda q, j: (q, j, 0, 0))
    vec = pl.BlockSpec((1, D_MODEL), lambda q, j: (0, 0))
    ada = lambda which: pl.BlockSpec((None, 1, D_MODEL),
                                     lambda q, j: (((1 + q) * 6 if latent else 0) + which, 0, 0))
    in_specs = [nat, itl if y_interleaved else nat, ada(gate_idx), vec, vec]
    args = [x.reshape(nat_shape), y.reshape(int_shape if y_interleaved else nat_shape), ada3,
            ln_g.reshape(1, D_MODEL), ln_b.reshape(1, D_MODEL)]
    grid = (nb, nt // POST_ROWS)
    if nxt is None:
        xo = pl.pallas_call(
            functools.partial(_post_last_kernel, y_interleaved=y_interleaved),
            grid=grid, in_specs=in_specs, out_specs=nat,
            out_shape=jax.ShapeDtypeStruct(nat_shape, F32),
            compiler_params=_cparams("parallel", "parallel"), name="post_last",
        )(*args)
        return xo.reshape(nb * seq, D_MODEL), None
    nxt_ada3, sh_idx, sc_idx = nxt
    in_specs += [ada(sh_idx), ada(sc_idx)]
    args += [nxt_ada3, nxt_ada3]
    xo, h = pl.pallas_call(
        functools.partial(_post_kernel, y_interleaved=y_interleaved, h_interleaved=h_interleaved),
        grid=grid, in_specs=in_specs, out_specs=(nat, itl if h_interleaved else nat),
        out_shape=(jax.ShapeDtypeStruct(nat_shape, F32),
                   jax.ShapeDtypeStruct(int_shape if h_interleaved else nat_shape, BF16)),
        compiler_params=_cparams("parallel", "parallel"), name="post",
    )(*args)
    return xo.reshape(nb * seq, D_MODEL), h.reshape(nb * seq, D_MODEL)


def _mm_bias_kernel(x_ref, w_ref, b_ref, o_ref):
    o_ref[...] = (_dot(x_ref[...], w_ref[...].astype(BF16)) + b_ref[...]).astype(o_ref.dtype)


def _mm_kernel(x_ref, w_ref, o_ref):
    o_ref[...] = _dot(x_ref[...], w_ref[...].astype(BF16)).astype(o_ref.dtype)


def _matmul(x, w, b, tm, tn, name, layer=None, single_buffer_x=False):
    m, k = x.shape
    n = w.shape[-1]
    tm = min(tm, m)
    x_mode = dict(pipeline_mode=pl.Buffered(1)) if single_buffer_x else {}
    if layer is None:
        w_spec = pl.BlockSpec((k, tn), lambda i, j: (0, j))
    else:
        w_spec = pl.BlockSpec((None, k, tn), lambda i, j: (layer, 0, j))
    in_specs = [pl.BlockSpec((tm, k), lambda i, j: (i, 0), **x_mode), w_spec]
    args = [x, w]
    body = _mm_kernel
    if b is not None:
        in_specs.append(pl.BlockSpec((1, tn), lambda i, j: (0, j)))
        args.append(b.reshape(1, n))
        body = _mm_bias_kernel
    return pl.pallas_call(
        body, grid=(m // tm, n // tn), in_specs=in_specs,
        out_specs=pl.BlockSpec((tm, tn), lambda i, j: (i, j)),
        out_shape=jax.ShapeDtypeStruct((m, n), F32),
        compiler_params=_cparams("parallel", "arbitrary"), name=name,
    )(*args)


def _cast_kernel(w_ref, o_ref):
    o_ref[...] = w_ref[...].astype(BF16)


def _cast_layer(w, layer, rows_per_step):
    _, r, c = w.shape
    return pl.pallas_call(
        _cast_kernel,
        grid=(r // rows_per_step,),
        in_specs=[pl.BlockSpec((None, rows_per_step, c), lambda i: (layer, i, 0))],
        out_specs=pl.BlockSpec((rows_per_step, c), lambda i: (i, 0)),
        out_shape=jax.ShapeDtypeStruct((r, c), BF16),
        compiler_params=_cparams("parallel"), name="cast_w",
    )(w)


POOL_ROW_BLOCK = 256
POOL_HALO = LANE


def _pool_kernel(u_ref, w_ref, sc_ref, o_ref):
    seq = u_ref.shape[0]
    rb = min(seq, POOL_ROW_BLOCK)
    for r0 in range(0, seq, rb):
        k0, k1 = max(0, r0 - POOL_HALO), min(seq, r0 + rb + POOL_HALO)
        t = r0 + lax.broadcasted_iota(jnp.int32, (rb, k1 - k0), 0)
        s = k0 + lax.broadcasted_iota(jnp.int32, (rb, k1 - k0), 1)
        d = s - t
        tcol = r0 + lax.broadcasted_iota(jnp.int32, (rb, 1), 0)
        for gi, win in enumerate(POOL_WINDOWS):
            half = win // 2
            cols = slice(gi * POOL_GW, (gi + 1) * POOL_GW)
            band = jnp.where((d >= -half) & (d < half), 1.0, 0.0).astype(BF16)
            cnt = jnp.minimum(tcol + half, seq) - jnp.maximum(tcol - half, 0)
            keys = u_ref[k0:k1, cols]
            k_hi = keys.astype(BF16)
            k_lo = (keys - k_hi.astype(F32)).astype(BF16)
            mean = (_dot(band, k_hi) + _dot(band, k_lo)) / cnt.astype(F32)
            y = _dot((mean - u_ref[r0:r0 + rb, cols]).astype(BF16), w_ref[gi])
            o_ref[r0:r0 + rb, cols] = (y * sc_ref[:, cols]).astype(BF16)


def _pool(z, pool_w, pool_scale, nb, seq):
    return pl.pallas_call(
        _pool_kernel,
        grid=(nb,),
        in_specs=[pl.BlockSpec((seq, BRANCH_W), lambda b: (b, OFF_POOL // BRANCH_W)),
                  pl.BlockSpec((len(POOL_WINDOWS), POOL_GW, POOL_GW), lambda b: (0, 0, 0)),
                  pl.BlockSpec((1, BRANCH_W), lambda b: (0, 0))],
        out_specs=pl.BlockSpec((seq, BRANCH_W), lambda b: (b, 0)),
        out_shape=jax.ShapeDtypeStruct((nb * seq, BRANCH_W), BF16),
        compiler_params=_cparams("parallel"), name="pool",
    )(z, pool_w, pool_scale.reshape(1, BRANCH_W))


def _softmax_attend(q, k, v, sink):
    s = _dot_nt(q, k) * ATTN_SCALE
    m = jnp.max(s, axis=-1, keepdims=True)
    if sink is not None:
        m = jnp.maximum(m, sink)
    e = jnp.exp(s - m)
    den = jnp.sum(e, axis=-1, keepdims=True)
    if sink is not None:
        den = den + jnp.exp(sink - m)
    return _dot(e.astype(BF16), v) * (1.0 / den)


def _ctx_attn_kernel(sink_ref, qg_ref, kg_ref, vg_ref, qw_ref, kw_ref, vw_ref, qn_ref, kn_ref,
                     og_ref, ow_ref, gk_ref, gv_ref, wk_ref, wv_ref):
    kvh = pl.program_id(1)
    kg = _rms(kg_ref[...], kn_ref[...])
    gk_ref[...] = kg
    gv_ref[...] = vg_ref[...]
    wk_ref[...] = kw_ref[...]
    wv_ref[...] = vw_ref[...]
    kg_b, vg_b = kg.astype(BF16), vg_ref[...].astype(BF16)
    kw_b, vw_b = kw_ref[...].astype(BF16), vw_ref[...].astype(BF16)
    for h in range(N_GROUP):
        cols = slice(h * HEAD_DIM, (h + 1) * HEAD_DIM)
        qg = _rms(qg_ref[:, cols], qn_ref[...]).astype(BF16)
        og_ref[:, cols] = _softmax_attend(qg, kg_b, vg_b, None).astype(BF16)
        sink = sink_ref[kvh * N_GROUP + h]
        ow_ref[:, cols] = _softmax_attend(qw_ref[:, cols].astype(BF16), kw_b, vw_b, sink).astype(BF16)


def _ctx_attn(z, sink, qn_g, kn_g, nb, seq):
    q_spec = lambda off: pl.BlockSpec((seq, GQ_W), lambda b, k: (b, off // GQ_W + k))
    kv_spec = lambda off: pl.BlockSpec((seq, HEAD_DIM), lambda b, k: (b, off // HEAD_DIM + k))
    vec = pl.BlockSpec((1, HEAD_DIM), lambda b, k: (0, 0))
    o_spec = pl.BlockSpec((seq, GQ_W), lambda b, k: (b, k))
    c_spec = pl.BlockSpec((seq, HEAD_DIM), lambda b, k: (b, k))
    m = nb * seq
    return pl.pallas_call(
        _ctx_attn_kernel,
        grid=(nb, N_KV),
        in_specs=[pl.BlockSpec(memory_space=pltpu.SMEM),
                  q_spec(OFF_QG), kv_spec(OFF_KG), kv_spec(OFF_VG),
                  q_spec(OFF_QW), kv_spec(OFF_KW), kv_spec(OFF_VW), vec, vec],
        out_specs=(o_spec, o_spec, c_spec, c_spec, c_spec, c_spec),
        out_shape=(jax.ShapeDtypeStruct((m, Q_W), BF16), jax.ShapeDtypeStruct((m, Q_W), BF16))
        + tuple(jax.ShapeDtypeStruct((m, KV_W), F32) for _ in range(4)),
        compiler_params=_cparams("parallel", "parallel"), name="ctx_attn",
    )(sink, z, z, z, z, z, z, qn_g.reshape(1, HEAD_DIM), kn_g.reshape(1, HEAD_DIM))


def _rope(x, cos, sin, low_half):
    partner = jnp.where(low_half, pltpu.roll(x, HEAD_DIM - 32, axis=1), pltpu.roll(x, 32, axis=1))
    return x * cos + partner * sin


def _low_half(rows):
    lane = lax.broadcasted_iota(jnp.int32, (rows, HEAD_DIM), 1)
    return (lane & 63) < 32


def _lat_global_kernel(q_ref, k_ref, v_ref, ck_ref, cv_ref, cos_ref, sin_ref, qn_ref, kn_ref,
                       o_ref, kall, vall, *, tq):
    seq = q_ref.shape[0]
    past = ck_ref.shape[0]
    kall[0:past, :] = ck_ref[...].astype(BF16)
    vall[0:past, :] = cv_ref[...].astype(BF16)
    kn = _rope(_rms(k_ref[...], kn_ref[...]), cos_ref[...], sin_ref[...], _low_half(seq))
    kall[past:past + seq, :] = kn.astype(BF16)
    vall[past:past + seq, :] = v_ref[...].astype(BF16)
    low = _low_half(tq)

    def q_block(qb, carry):
        rows = pl.ds(pl.multiple_of(qb * tq, tq), tq)
        cos, sin = cos_ref[rows, :], sin_ref[rows, :]
        for h in range(N_GROUP):
            cols = slice(h * HEAD_DIM, (h + 1) * HEAD_DIM)
            q = _rope(_rms(q_ref[rows, cols], qn_ref[...]), cos, sin, low).astype(BF16)
            o_ref[rows, cols] = _softmax_attend(q, kall[...], vall[...], None).astype(BF16)
        return carry

    lax.fori_loop(0, seq // tq, q_block, 0)


def _lat_global(z, cache_k, cache_v, layer, cos, sin, qn_g, kn_g, nb, seq, tq=256):
    past = cache_k.shape[2]
    q_spec = pl.BlockSpec((seq, GQ_W), lambda b, k: (b, OFF_QG // GQ_W + k))
    kv_spec = lambda off: pl.BlockSpec((seq, HEAD_DIM), lambda b, k: (b, off // HEAD_DIM + k))
    c_spec = pl.BlockSpec((None, None, past, HEAD_DIM), lambda b, k: (b, layer, 0, k))
    tab = pl.BlockSpec((seq, HEAD_DIM), lambda b, k: (0, 0))
    vec = pl.BlockSpec((1, HEAD_DIM), lambda b, k: (0, 0))
    return pl.pallas_call(
        functools.partial(_lat_global_kernel, tq=tq),
        grid=(nb, N_KV),
        in_specs=[q_spec, kv_spec(OFF_KG), kv_spec(OFF_VG), c_spec, c_spec, tab, tab, vec, vec],
        out_specs=pl.BlockSpec((seq, GQ_W), lambda b, k: (b, k)),
        out_shape=jax.ShapeDtypeStruct((nb * seq, Q_W), BF16),
        scratch_shapes=[pltpu.VMEM((past + seq, HEAD_DIM), BF16), pltpu.VMEM((past + seq, HEAD_DIM), BF16)],
        compiler_params=_cparams("parallel", "parallel"), name="lat_global",
    )(z, z, z, cache_k, cache_v, cos, sin, qn_g.reshape(1, HEAD_DIM), kn_g.reshape(1, HEAD_DIM))


def _lat_window_kernel(sink_ref, q_ref, k_ref, v_ref, ck_ref, cv_ref, cos_ref, sin_ref,
                       o_ref, kpad, vpad):
    seq = q_ref.shape[0]
    blk = WINDOW
    kvh = pl.program_id(1)
    zeros = jnp.zeros((blk, HEAD_DIM), BF16)
    kpad[0:blk, :] = zeros
    vpad[0:blk, :] = zeros
    kpad[blk + seq:2 * blk + seq, :] = zeros
    vpad[blk + seq:2 * blk + seq, :] = zeros
    kpad[blk:blk + seq, :] = _rope(k_ref[...], cos_ref[...], sin_ref[...], _low_half(seq)).astype(BF16)
    vpad[blk:blk + seq, :] = v_ref[...].astype(BF16)
    ck = ck_ref[...].astype(BF16)
    cv = cv_ref[...].astype(BF16)
    low = _low_half(blk)
    rows4 = N_GROUP * blk
    head = lax.broadcasted_iota(jnp.int32, (rows4, 1), 0) // blk
    sink = jnp.zeros((rows4, 1), F32)
    for h in range(N_GROUP):
        sink = jnp.where(head == h, sink_ref[kvh * N_GROUP + h], sink)
    qi = lax.broadcasted_iota(jnp.int32, (rows4, 3 * blk), 0) & (blk - 1)
    kj = lax.broadcasted_iota(jnp.int32, (rows4, 3 * blk), 1)
    in_band = jnp.abs(qi + blk - kj) <= WINDOW

    def q_block(n, carry):
        r0 = pl.multiple_of(n * blk, blk)
        rows = pl.ds(r0, blk)
        cos, sin = cos_ref[rows, :], sin_ref[rows, :]
        q4 = jnp.concatenate(
            [_rope(q_ref[rows, h * HEAD_DIM:(h + 1) * HEAD_DIM], cos, sin, low) for h in range(N_GROUP)],
            axis=0).astype(BF16)
        tk = (n - 1) * blk + kj
        valid = in_band & (tk >= 0) & (tk < seq)
        s_loc = jnp.where(valid, _dot_nt(q4, kpad[pl.ds(r0, 3 * blk), :]) * ATTN_SCALE, NEG_INF)
        s_ctx = _dot_nt(q4, ck) * ATTN_SCALE
        m = jnp.maximum(jnp.maximum(jnp.max(s_loc, axis=-1, keepdims=True),
                                    jnp.max(s_ctx, axis=-1, keepdims=True)), sink)
        e_loc = jnp.exp(s_loc - m)
        e_ctx = jnp.exp(s_ctx - m)
        den = (jnp.sum(e_loc, axis=-1, keepdims=True) + jnp.sum(e_ctx, axis=-1, keepdims=True)
               + jnp.exp(sink - m))
        o4 = (_dot(e_loc.astype(BF16), vpad[pl.ds(r0, 3 * blk), :]) + _dot(e_ctx.astype(BF16), cv)) * (1.0 / den)
        for h in range(N_GROUP):
            o_ref[rows, h * HEAD_DIM:(h + 1) * HEAD_DIM] = o4[h * blk:(h + 1) * blk].astype(BF16)
        return carry

    lax.fori_loop(0, seq // blk, q_block, 0)


def _lat_window(z, cache_k, cache_v, layer, cos, sin, sink, nb, seq):
    past = cache_k.shape[2]
    q_spec = pl.BlockSpec((seq, GQ_W), lambda b, k: (b, OFF_QW // GQ_W + k))
    kv_spec = lambda off: pl.BlockSpec((seq, HEAD_DIM), lambda b, k: (b, off // HEAD_DIM + k))
    c_spec = pl.BlockSpec((None, None, past, HEAD_DIM), lambda b, k: (b, layer, 0, k))
    tab = pl.BlockSpec((seq, HEAD_DIM), lambda b, k: (0, 0))
    return pl.pallas_call(
        _lat_window_kernel,
        grid=(nb, N_KV),
        in_specs=[pl.BlockSpec(memory_space=pltpu.SMEM), q_spec, kv_spec(OFF_KW), kv_spec(OFF_VW),
                  c_spec, c_spec, tab, tab],
        out_specs=pl.BlockSpec((seq, GQ_W), lambda b, k: (b, k)),
        out_shape=jax.ShapeDtypeStruct((nb * seq, Q_W), BF16),
        scratch_shapes=[pltpu.VMEM((seq + 2 * WINDOW, HEAD_DIM), BF16),
                        pltpu.VMEM((seq + 2 * WINDOW, HEAD_DIM), BF16)],
        compiler_params=_cparams("parallel", "parallel"), name="lat_window",
    )(sink, z, z, z, cache_k, cache_v, cos, sin)


def _interleave(v, n_tiles):
    return jnp.swapaxes(v.reshape(SUBLANE, n_tiles, v.shape[-1]), 0, 1).reshape(v.shape)


def _deinterleave(v, n_tiles):
    return jnp.swapaxes(v.reshape(n_tiles, SUBLANE, v.shape[-1]), 0, 1).reshape(v.shape)


def _lru_kernel(x_ref, y_ref, cw_ref, cb_ref, lam_ref, wa_ref, ba_ref, wx_ref, bx_ref, h0_ref,
                o_ref, st_ref, a_f, b_f, a_b, b_b):
    seq, width = x_ref.shape
    n_tiles = seq // SUBLANE
    sub = lax.broadcasted_iota(jnp.int32, (SUBLANE, width), 0)

    def from_prev_segment(tile):
        return jnp.where(sub >= 1, pltpu.roll(tile, 1, axis=0), 0.0)

    def from_next_segment(tile):
        return jnp.where(sub < SUBLANE - 1, pltpu.roll(tile, SUBLANE - 1, axis=0), 0.0)

    x = _interleave(x_ref[...], n_tiles)
    last, last2, first = x[seq - SUBLANE:], x[seq - 2 * SUBLANE:seq - SUBLANE], x[:SUBLANE]
    xm1 = jnp.concatenate([from_prev_segment(last), x[:seq - SUBLANE]], axis=0)
    xm2 = jnp.concatenate([from_prev_segment(last2), from_prev_segment(last), x[:seq - 2 * SUBLANE]], axis=0)
    xp1 = jnp.concatenate([x[SUBLANE:], from_next_segment(first)], axis=0)
    xc = (cb_ref[...] + xm2 * cw_ref[0:1, :] + xm1 * cw_ref[1:2, :] + x * cw_ref[2:3, :] + xp1 * cw_ref[3:4, :])
    xcb = xc.astype(BF16)

    def block_diag(w_ref, d):
        return jnp.concatenate(
            [_dot(xcb[:, j * LRU_BW:(j + 1) * LRU_BW], w_ref[d, j]) for j in range(width // LRU_BW)], axis=1)

    for d, (a_s, b_s) in enumerate(((a_f, b_f), (a_b, b_b))):
        r = jax.nn.sigmoid(block_diag(wa_ref, d) + ba_ref[d:d + 1, :])
        gate_i = jax.nn.sigmoid(block_diag(wx_ref, d) + bx_ref[d:d + 1, :])
        neg_lam = -lam_ref[d:d + 1, :]
        softplus = jnp.maximum(neg_lam, 0.0) + jnp.log1p(jnp.exp(-jnp.abs(neg_lam)))
        log_a = -LRU_C * r * softplus
        a_s[...] = jnp.exp(log_a)
        th = jnp.tanh(log_a)
        b_s[...] = jnp.sqrt(-2.0 * th / (1.0 - th)) * (gate_i * xc)

    def local_step(i, carry):
        hf, pf, hb, pb = carry
        rf = pl.ds(pl.multiple_of(i * SUBLANE, SUBLANE), SUBLANE)
        rb = pl.ds(pl.multiple_of((n_tiles - 1 - i) * SUBLANE, SUBLANE), SUBLANE)
        af, ab = a_f[rf, :], a_b[rb, :]
        hf, pf = af * hf + b_f[rf, :], af * pf
        hb, pb = ab * hb + b_b[rb, :], ab * pb
        a_f[rf, :], b_f[rf, :] = pf, hf
        a_b[rb, :], b_b[rb, :] = pb, hb
        return hf, pf, hb, pb

    zeros, ones = jnp.zeros((SUBLANE, width), F32), jnp.ones((SUBLANE, width), F32)
    hf, pf, hb, pb = lax.fori_loop(0, n_tiles, local_step, (zeros, ones, zeros, ones), unroll=8)

    for s in (1, 2, 4):
        take_f, take_b = sub >= s, sub < SUBLANE - s
        hf = jnp.where(take_f, pf * pltpu.roll(hf, s, axis=0) + hf, hf)
        pf = jnp.where(take_f, pf * pltpu.roll(pf, s, axis=0), pf)
        hb = jnp.where(take_b, pb * pltpu.roll(hb, SUBLANE - s, axis=0) + hb, hb)
        pb = jnp.where(take_b, pb * pltpu.roll(pb, SUBLANE - s, axis=0), pb)
    h0f = jnp.broadcast_to(h0_ref[0:1, :], (SUBLANE, width))
    h0b = jnp.broadcast_to(h0_ref[1:2, :], (SUBLANE, width))
    out_f = pf * h0f + hf
    out_b = pb * h0b + hb
    st_ref[0:1, :] = out_f[SUBLANE - 1:SUBLANE, :]
    st_ref[1:2, :] = out_b[0:1, :]
    in_f = jnp.where(sub >= 1, pltpu.roll(out_f, 1, axis=0), h0f)
    in_b = jnp.where(sub < SUBLANE - 1, pltpu.roll(out_b, SUBLANE - 1, axis=0), h0b)

    def tiles(ref):
        return ref[...].reshape(n_tiles, SUBLANE, width)

    h = (tiles(b_f) + tiles(a_f) * in_f[None]) + (tiles(b_b) + tiles(a_b) * in_b[None])
    h = _deinterleave(h.reshape(seq, width), n_tiles)
    o_ref[...] = (h * jax.nn.gelu(y_ref[...])).astype(BF16)


def _lru(z, h0, h0_layer, conv_w, conv_b, lam, wa, ba, wx, bx, nb, seq, cw=256):
    nc = LRU_W // cw
    nd = cw // LRU_BW
    vec2 = pl.BlockSpec((2, cw), lambda b, c: (0, c))
    wspec = pl.BlockSpec((2, nd, LRU_BW, LRU_BW), lambda b, c: (0, c, 0, 0))
    return pl.pallas_call(
        _lru_kernel,
        grid=(nb, nc),
        in_specs=[pl.BlockSpec((seq, cw), lambda b, c: (b, OFF_XL // cw + c)),
                  pl.BlockSpec((seq, cw), lambda b, c: (b, OFF_YL // cw + c)),
                  pl.BlockSpec((4, cw), lambda b, c: (0, c)),
                  pl.BlockSpec((1, cw), lambda b, c: (0, c)),
                  vec2, wspec, vec2, wspec, vec2,
                  pl.BlockSpec((None, None, 2, cw), lambda b, c: (b, h0_layer, 0, c))],
        out_specs=(pl.BlockSpec((seq, cw), lambda b, c: (b, c)),
                   pl.BlockSpec((None, 2, cw), lambda b, c: (b, 0, c))),
        out_shape=(jax.ShapeDtypeStruct((nb * seq, LRU_W), BF16), jax.ShapeDtypeStruct((nb, 2, LRU_W), F32)),
        scratch_shapes=[pltpu.VMEM((seq, cw), F32)] * 4,
        compiler_params=_cparams("parallel", "parallel"), name="lru",
    )(z, z, conv_w, conv_b.reshape(1, LRU_W), lam, wa, ba, wx, bx, h0)


MERGE_ROW_CHUNK = 128


def _merge_kernel(b0, b1, b2, b3, w_ref, g0, g1, g2, g3, o_ref):
    for r in range(0, o_ref.shape[0], MERGE_ROW_CHUNK):
        rows = slice(r, r + MERGE_ROW_CHUNK)
        acc = None
        for n, (br, gr) in enumerate(((b0, g0), (b1, g1), (b2, g2), (b3, g3))):
            t = jax.nn.sigmoid(gr[rows, :]) * _dot(br[rows, :], w_ref[n])
            acc = t if acc is None else acc + t
        o_ref[rows, :] = acc.astype(BF16)


def _merge(branches, w_br, z, tm=1024, tn=256):
    m = z.shape[0]
    br = pl.BlockSpec((tm, BRANCH_W), lambda i, j: (i, 0))
    gate = lambda n: pl.BlockSpec((tm, tn), lambda i, j: (i, (OFF_GATE + n * D_MODEL) // tn + j))
    return pl.pallas_call(
        _merge_kernel,
        grid=(m // tm, D_MODEL // tn),
        in_specs=[br, br, br, br, pl.BlockSpec((N_BRANCH, BRANCH_W, tn), lambda i, j: (0, 0, j)),
                  gate(0), gate(1), gate(2), gate(3)],
        out_specs=pl.BlockSpec((tm, tn), lambda i, j: (i, j)),
        out_shape=jax.ShapeDtypeStruct((m, D_MODEL), BF16),
        compiler_params=_cparams("parallel", "arbitrary"), name="merge",
    )(*branches, w_br, z, z, z, z)


UP_ROW_CHUNK = 128


def _up_kernel(x_ref, wu_ref, wg_ref, cwu_ref, cwg_ref, cbu_ref, cbg_ref, o_ref, *, seq):
    rows, tn = o_ref.shape
    sub = lax.broadcasted_iota(jnp.int32, (SUBLANE, tn), 0)

    def conv(v, w_ref, b_ref):
        prev, nxt = [], []
        for q in range(rows // seq):
            vq = v[q * seq:(q + 1) * seq]
            head, tail = vq[:SUBLANE], vq[seq - SUBLANE:]
            prev += [jnp.where(sub >= 1, pltpu.roll(tail, 1, axis=0), 0.0), vq[:seq - SUBLANE]]
            nxt += [vq[SUBLANE:], jnp.where(sub < SUBLANE - 1, pltpu.roll(head, SUBLANE - 1, axis=0), 0.0)]
        prev, nxt = jnp.concatenate(prev, axis=0), jnp.concatenate(nxt, axis=0)
        return b_ref[...] + prev * w_ref[0:1, :] + v * w_ref[1:2, :] + nxt * w_ref[2:3, :]

    wu, wg = wu_ref[...].astype(BF16), wg_ref[...].astype(BF16)

    def row_chunked_dot(w):
        return jnp.concatenate([_dot(x_ref[r:r + UP_ROW_CHUNK, :], w) for r in range(0, rows, UP_ROW_CHUNK)],
                               axis=0)

    u = conv(row_chunked_dot(wu), cwu_ref, cbu_ref)
    g = conv(row_chunked_dot(wg), cwg_ref, cbg_ref)
    o_ref[...] = (g * jax.nn.sigmoid(g) * u).astype(BF16)


def _up(h, w_up, conv_w, conv_b, layer, seq, tm=2048, tn=256):
    m = h.shape[0]
    tm = min(tm, m)
    nj = D_FF // tn
    w_spec = lambda off: pl.BlockSpec((None, D_MODEL, tn), lambda i, j: (layer, 0, off + j))
    cw_spec = lambda off: pl.BlockSpec((None, 3, tn), lambda i, j: (layer, 0, off + j))
    cb_spec = lambda off: pl.BlockSpec((None, 1, tn), lambda i, j: (layer, 0, off + j))
    conv_b3 = conv_b.reshape(DEPTH, 1, 2 * D_FF)
    return pl.pallas_call(
        functools.partial(_up_kernel, seq=seq),
        grid=(m // tm, nj),
        in_specs=[pl.BlockSpec((tm, D_MODEL), lambda i, j: (i, 0), pipeline_mode=pl.Buffered(1)),
                  w_spec(0), w_spec(nj),
                  cw_spec(0), cw_spec(nj), cb_spec(0), cb_spec(nj)],
        out_specs=pl.BlockSpec((tm, tn), lambda i, j: (i, j)),
        out_shape=jax.ShapeDtypeStruct((m, D_FF), BF16),
        compiler_params=_cparams("parallel", "arbitrary"), name="ffn_up",
    )(h, w_up, w_up, conv_w, conv_w, conv_b3, conv_b3)


def _rope_tables(seq):
    t = jnp.arange(seq)
    half = HEAD_DIM // 2
    inv = ROPE_THETA ** (-jnp.arange(0, half, 2, dtype=F32) / half)
    ang_r = (t // GRID_W).astype(F32)[:, None] * inv[None, :]
    ang_c = (t % GRID_W).astype(F32)[:, None] * inv[None, :]
    cos = jnp.concatenate([jnp.cos(ang_r), jnp.cos(ang_r), jnp.cos(ang_c), jnp.cos(ang_c)], axis=1)
    sin = jnp.concatenate([-jnp.sin(ang_r), jnp.sin(ang_r), -jnp.sin(ang_c), jnp.sin(ang_c)], axis=1)
    return cos, sin


def kernel(x_prompt, x_sample, cache_gk, cache_gv, cache_wk, cache_wv, state_lru, c, c_ctx, w_ada, b_ada, w_in, b_in, pool_w, pool_scale, qn_g, kn_g, sink, lru_conv_w, lru_conv_b, lru_lambda, lru_wa, lru_ba, lru_wx, lru_bx, w_br, w_o, ln1_g, ln1_b, w_up, ffn_conv_w, ffn_conv_b, w_down, ln2_g, ln2_b):
    nb_c, seq_c, _ = x_prompt.shape
    nb_s, seq_s, _ = x_sample.shape
    past = cache_gk.shape[2]

    c_all = jnp.concatenate([c_ctx[None, :], c, jnp.zeros((ADA_ROWS - 1 - nb_s, D_MODEL), F32)], axis=0)
    ada = _ada(c_all, w_ada, b_ada)
    ada3 = [ada[l].reshape(ADA_ROWS * 6, 1, D_MODEL) for l in range(DEPTH)]
    cos, sin = _rope_tables(seq_s)
    caches = [a.reshape(nb_s, DEPTH, past, KV_W) for a in (cache_gk, cache_gv, cache_wk, cache_wv)]
    zero_state = jnp.zeros((nb_c, 1, 2, LRU_W), F32)

    groups = [dict(x=x_prompt.reshape(nb_c * seq_c, D_MODEL), nb=nb_c, seq=seq_c, cond=None),
              dict(x=x_sample.reshape(nb_s * seq_s, D_MODEL), nb=nb_s, seq=seq_s, cond=seq_s)]
    for g in groups:
        g["h"] = _lnmod(g["x"], ada3[0], g["cond"])
    new_cache = [[] for _ in range(5)]

    for l in range(DEPTH):
        w_br_l = _cast_layer(w_br.reshape(DEPTH, N_BRANCH * BRANCH_W, D_MODEL), l, 512).reshape(
            N_BRANCH, BRANCH_W, D_MODEL)
        w_down_l = _cast_layer(w_down, l, D_FF // 16)
        pool_w_l = pool_w[l].astype(BF16)
        wa_l = lru_wa[l].astype(BF16)
        wx_l = lru_wx[l].astype(BF16)
        for gi, g in enumerate(groups):
            nb, seq = g["nb"], g["seq"]
            z = _matmul(g["h"], w_in, b_in[l], 2048, 512, "in_proj", layer=l, single_buffer_x=True)
            o_pool = _pool(z, pool_w_l, pool_scale[l], nb, seq)
            if gi == 0:
                o_g, o_w, gk, gv, wk, wv = _ctx_attn(z, sink[l], qn_g[l], kn_g[l], nb, seq)
                h0, h0_layer = zero_state, 0
            else:
                o_g = _lat_global(z, caches[0], caches[1], l, cos, sin, qn_g[l], kn_g[l], nb, seq)
                o_w = _lat_window(z, caches[2], caches[3], l, cos, sin, sink[l], nb, seq)
                h0, h0_layer = state_lru, l
            o_lru, st = _lru(z, h0, h0_layer, lru_conv_w[l], lru_conv_b[l], lru_lambda[l],
                             wa_l, lru_ba[l], wx_l, lru_bx[l], nb, seq)
            if gi == 0:
                for lst, a in zip(new_cache, (gk, gv, wk, wv, st)):
                    lst.append(a)
            merged = _merge((o_pool, o_g, o_w, o_lru), w_br_l, z)
            mix = _matmul(merged, w_o, None, 2048, 512, "out_proj", layer=l, single_buffer_x=True)
            latent = gi == 1
            x1, h2 = _post(g["x"], mix, ada3[l], 2, ln1_g[l], ln1_b[l], nb, seq, latent,
                           nxt=(ada3[l], 3, 4), h_interleaved=True)
            act = _up(h2, w_up, ffn_conv_w, ffn_conv_b, l, seq)
            ff = _matmul(act, w_down_l, None, 512, 512, "ffn_down")
            nxt = (ada3[l + 1], 0, 1) if l + 1 < DEPTH else None
            g["x"], g["h"] = _post(x1, ff, ada3[l], 5, ln2_g[l], ln2_b[l], nb, seq, latent,
                                   nxt=nxt, y_interleaved=True)

    y_prompt = groups[0]["x"].reshape(nb_c, seq_c, D_MODEL)
    y_sample = groups[1]["x"].reshape(nb_s, seq_s, D_MODEL)
    kv = [jnp.stack([a.reshape(nb_c, seq_c, N_KV, HEAD_DIM) for a in new_cache[i]], axis=1) for i in range(4)]
    new_lru = jnp.stack(new_cache[4], axis=1)
    return (y_prompt, y_sample, kv[0], kv[1], kv[2], kv[3], new_lru)
```

```python
import functools

import jax
import jax.numpy as jnp
from jax import lax
from jax.experimental import pallas as pl
from jax.experimental.pallas import tpu as pltpu

F32 = jnp.float32
BF16 = jnp.bfloat16

D_MODEL = 4096
DEPTH = 2
GRID_W = 64
BRANCH_W = D_MODEL // 4
N_BRANCH = 4
HEAD_DIM = 128
N_HEADS = BRANCH_W // HEAD_DIM
N_KV = N_HEADS // 4
N_GROUP = N_HEADS // N_KV
Q_W = N_HEADS * HEAD_DIM
KV_W = N_KV * HEAD_DIM
WINDOW = 128
ROPE_THETA = 10000.0
ATTN_SCALE = HEAD_DIM ** -0.5
NEG_INF = -1e30
POOL_WINDOWS = (2, 4, 8, 16)
POOL_GW = BRANCH_W // len(POOL_WINDOWS)
LRU_W = BRANCH_W
LRU_BW = 128
LRU_C = 8.0
D_FF = 11008
ALPHA = (2 * DEPTH) ** 0.25
IN_SIZES = (BRANCH_W, Q_W, KV_W, KV_W, Q_W, KV_W, KV_W, LRU_W, LRU_W, N_BRANCH * D_MODEL)
IN_W = sum(IN_SIZES)

OFF_POOL = 0
OFF_QG = OFF_POOL + BRANCH_W
OFF_KG = OFF_QG + Q_W
OFF_VG = OFF_KG + KV_W
OFF_QW = OFF_VG + KV_W
OFF_KW = OFF_QW + Q_W
OFF_VW = OFF_KW + KV_W
OFF_XL = OFF_VW + KV_W
OFF_YL = OFF_XL + LRU_W
OFF_GATE = OFF_YL + LRU_W

LANE = 128
SUBLANE = 8
ADA_ROWS = 16
GQ_W = N_GROUP * HEAD_DIM
VMEM_LIMIT = 56 << 20


def _cparams(*sem):
    return pltpu.CompilerParams(dimension_semantics=sem, vmem_limit_bytes=VMEM_LIMIT)


def _dot(a, b):
    return jnp.dot(a, b, preferred_element_type=F32)


def _dot_nt(a, b):
    return lax.dot_general(a, b, (((1,), (1,)), ((), ())), preferred_element_type=F32)


def _ln(x, eps=1e-6):
    mu = jnp.mean(x, axis=-1, keepdims=True)
    xc = x - mu
    var = jnp.mean(xc * xc, axis=-1, keepdims=True)
    return xc * lax.rsqrt(var + eps)


def _rms(x, g, eps=1e-6):
    return x * lax.rsqrt(jnp.mean(x * x, axis=-1, keepdims=True) + eps) * g


def _ada_kernel(c_ref, w_ref, b_ref, o_ref):
    c = c_ref[...]
    s = (c * jax.nn.sigmoid(c)).astype(BF16)
    o_ref[...] = _dot(s, w_ref[...].astype(BF16)) + b_ref[...]


def _ada(c_all, w_ada, b_ada, tn=512):
    n = w_ada.shape[-1]
    return pl.pallas_call(
        _ada_kernel,
        grid=(DEPTH, n // tn),
        in_specs=[pl.BlockSpec((ADA_ROWS, D_MODEL), lambda l, j: (0, 0)),
                  pl.BlockSpec((None, D_MODEL, tn), lambda l, j: (l, 0, j)),
                  pl.BlockSpec((None, 1, tn), lambda l, j: (l, 0, j))],
        out_specs=pl.BlockSpec((None, ADA_ROWS, tn), lambda l, j: (l, 0, j)),
        out_shape=jax.ShapeDtypeStruct((DEPTH, ADA_ROWS, n), F32),
        compiler_params=_cparams("parallel", "parallel"),
        name="ada",
    )(c_all, w_ada, b_ada.reshape(DEPTH, 1, n))


def _ada_spec(which, tm, rows_per_cond):
    if rows_per_cond is None:
        return pl.BlockSpec((None, 1, D_MODEL), lambda i: (which, 0, 0))
    return pl.BlockSpec((None, 1, D_MODEL), lambda i: ((1 + (i * tm) // rows_per_cond) * 6 + which, 0, 0))


def _lnmod_kernel(x_ref, sh_ref, sc_ref, h_ref):
    h_ref[...] = (_ln(x_ref[...]) * (1.0 + sc_ref[...]) + sh_ref[...]).astype(BF16)


def _lnmod(x, ada3, rows_per_cond, tm=256):
    m = x.shape[0]
    return pl.pallas_call(
        _lnmod_kernel,
        grid=(m // tm,),
        in_specs=[pl.BlockSpec((tm, D_MODEL), lambda i: (i, 0)),
                  _ada_spec(0, tm, rows_per_cond), _ada_spec(1, tm, rows_per_cond)],
        out_specs=pl.BlockSpec((tm, D_MODEL), lambda i: (i, 0)),
        out_shape=jax.ShapeDtypeStruct((m, D_MODEL), BF16),
        compiler_params=_cparams("parallel"),
        name="lnmod",
    )(x, ada3, ada3)


def _post_kernel(x_ref, y_ref, g_ref, lg_ref, lb_ref, sh_ref, sc_ref, xo_ref, h_ref, *, y_interleaved, h_interleaved):
    y = y_ref[...]
    if y_interleaved:
        y = jnp.swapaxes(y, 0, 1)
    xn = _ln(ALPHA * x_ref[...] + g_ref[...] * y) * lg_ref[...] + lb_ref[...]
    xo_ref[...] = xn
    if h_ref is not None:
        h = _ln(xn) * (1.0 + sc_ref[...]) + sh_ref[...]
        if h_interleaved:
            h = jnp.swapaxes(h, 0, 1)
        h_ref[...] = h.astype(BF16)


def _post_last_kernel(x_ref, y_ref, g_ref, lg_ref, lb_ref, xo_ref, *, y_interleaved):
    _post_kernel(x_ref, y_ref, g_ref, lg_ref, lb_ref, None, None, xo_ref, None,
                 y_interleaved=y_interleaved, h_interleaved=False)


POST_ROWS = 32


def _post(x, y, ada3, gate_idx, ln_g, ln_b, nb, seq, latent, nxt=None, y_interleaved=False, h_interleaved=False):
    nt = seq // SUBLANE
    nat_shape, int_shape = (nb, SUBLANE, nt, D_MODEL), (nb, nt, SUBLANE, D_MODEL)
    nat = pl.BlockSpec((None, SUBLANE, POST_ROWS, D_MODEL), lambda q, j: (q, 0, j, 0))
    itl = pl.BlockSpec((None, POST_ROWS, SUBLANE, D_MODEL), lambda q, j: (q, j, 0, 0))
    vec = pl.BlockSpec((1, D_MODEL), lambda q, j: (0, 0))
    ada = lambda which: pl.BlockSpec((None, 1, D_MODEL),
                                     lambda q, j: (((1 + q) * 6 if latent else 0) + which, 0, 0))
    in_specs = [nat, itl if y_interleaved else nat, ada(gate_idx), vec, vec]
    args = [x.reshape(nat_shape), y.reshape(int_shape if y_interleaved else nat_shape), ada3,
            ln_g.reshape(1, D_MODEL), ln_b.reshape(1, D_MODEL)]
    grid = (nb, nt // POST_ROWS)
    if nxt is None:
        xo = pl.pallas_call(
            functools.partial(_post_last_kernel, y_interleaved=y_interleaved),
            grid=grid, in_specs=in_specs, out_specs=nat,
            out_shape=jax.ShapeDtypeStruct(nat_shape, F32),
            compiler_params=_cparams("parallel", "parallel"), name="post_last",
        )(*args)
        return xo.reshape(nb * seq, D_MODEL), None
    nxt_ada3, sh_idx, sc_idx = nxt
    in_specs += [ada(sh_idx), ada(sc_idx)]
    args += [nxt_ada3, nxt_ada3]
    xo, h = pl.pallas_call(
        functools.partial(_post_kernel, y_interleaved=y_interleaved, h_interleaved=h_interleaved),
        grid=grid, in_specs=in_specs, out_specs=(nat, itl if h_interleaved else nat),
        out_shape=(jax.ShapeDtypeStruct(nat_shape, F32),
                   jax.ShapeDtypeStruct(int_shape if h_interleaved else nat_shape, BF16)),
        compiler_params=_cparams("parallel", "parallel"), name="post",
    )(*args)
    return xo.reshape(nb * seq, D_MODEL), h.reshape(nb * seq, D_MODEL)


def _mm_bias_kernel(x_ref, w_ref, b_ref, o_ref):
    o_ref[...] = (_dot(x_ref[...], w_ref[...].astype(BF16)) + b_ref[...]).astype(o_ref.dtype)


def _mm_kernel(x_ref, w_ref, o_ref):
    o_ref[...] = _dot(x_ref[...], w_ref[...].astype(BF16)).astype(o_ref.dtype)


def _matmul(x, w, b, tm, tn, name, layer=None, single_buffer_x=False):
    m, k = x.shape
    n = w.shape[-1]
    tm = min(tm, m)
    x_mode = dict(pipeline_mode=pl.Buffered(1)) if single_buffer_x else {}
    if layer is None:
        w_spec = pl.BlockSpec((k, tn), lambda i, j: (0, j))
    else:
        w_spec = pl.BlockSpec((None, k, tn), lambda i, j: (layer, 0, j))
    in_specs = [pl.BlockSpec((tm, k), lambda i, j: (i, 0), **x_mode), w_spec]
    args = [x, w]
    body = _mm_kernel
    if b is not None:
        in_specs.append(pl.BlockSpec((1, tn), lambda i, j: (0, j)))
        args.append(b.reshape(1, n))
        body = _mm_bias_kernel
    return pl.pallas_call(
        body, grid=(m // tm, n // tn), in_specs=in_specs,
        out_specs=pl.BlockSpec((tm, tn), lambda i, j: (i, j)),
        out_shape=jax.ShapeDtypeStruct((m, n), F32),
        compiler_params=_cparams("parallel", "arbitrary"), name=name,
    )(*args)


def _cast_kernel(w_ref, o_ref):
    o_ref[...] = w_ref[...].astype(BF16)


def _cast_layer(w, layer, rows_per_step):
    _, r, c = w.shape
    return pl.pallas_call(
        _cast_kernel,
        grid=(r // rows_per_step,),
        in_specs=[pl.BlockSpec((None, rows_per_step, c), lambda i: (layer, i, 0))],
        out_specs=pl.BlockSpec((rows_per_step, c), lambda i: (i, 0)),
        out_shape=jax.ShapeDtypeStruct((r, c), BF16),
        compiler_params=_cparams("parallel"), name="cast_w",
    )(w)


POOL_ROW_BLOCK = 256
POOL_HALO = LANE


def _pool_kernel(u_ref, w_ref, sc_ref, o_ref):
    seq = u_ref.shape[0]
    rb = min(seq, POOL_ROW_BLOCK)
    for r0 in range(0, seq, rb):
        k0, k1 = max(0, r0 - POOL_HALO), min(seq, r0 + rb + POOL_HALO)
        t = r0 + lax.broadcasted_iota(jnp.int32, (rb, k1 - k0), 0)
        s = k0 + lax.broadcasted_iota(jnp.int32, (rb, k1 - k0), 1)
        d = s - t
        tcol = r0 + lax.broadcasted_iota(jnp.int32, (rb, 1), 0)
        for gi, win in enumerate(POOL_WINDOWS):
            half = win // 2
            cols = slice(gi * POOL_GW, (gi + 1) * POOL_GW)
            band = jnp.where((d >= -half) & (d < half), 1.0, 0.0).astype(BF16)
            cnt = jnp.minimum(tcol + half, seq) - jnp.maximum(tcol - half, 0)
            keys = u_ref[k0:k1, cols]
            k_hi = keys.astype(BF16)
            k_lo = (keys - k_hi.astype(F32)).astype(BF16)
            mean = (_dot(band, k_hi) + _dot(band, k_lo)) / cnt.astype(F32)
            y = _dot((mean - u_ref[r0:r0 + rb, cols]).astype(BF16), w_ref[gi])
            o_ref[r0:r0 + rb, cols] = (y * sc_ref[:, cols]).astype(BF16)


def _pool(z, pool_w, pool_scale, nb, seq):
    return pl.pallas_call(
        _pool_kernel,
        grid=(nb,),
        in_specs=[pl.BlockSpec((seq, BRANCH_W), lambda b: (b, OFF_POOL // BRANCH_W)),
                  pl.BlockSpec((len(POOL_WINDOWS), POOL_GW, POOL_GW), lambda b: (0, 0, 0)),
                  pl.BlockSpec((1, BRANCH_W), lambda b: (0, 0))],
        out_specs=pl.BlockSpec((seq, BRANCH_W), lambda b: (b, 0)),
        out_shape=jax.ShapeDtypeStruct((nb * seq, BRANCH_W), BF16),
        compiler_params=_cparams("parallel"), name="pool",
    )(z, pool_w, pool_scale.reshape(1, BRANCH_W))


def _softmax_attend(q, k, v, sink):
    s = _dot_nt(q, k) * ATTN_SCALE
    m = jnp.max(s, axis=-1, keepdims=True)
    if sink is not None:
        m = jnp.maximum(m, sink)
    e = jnp.exp(s - m)
    den = jnp.sum(e, axis=-1, keepdims=True)
    if sink is not None:
        den = den + jnp.exp(sink - m)
    return _dot(e.astype(BF16), v) * (1.0 / den)


def _ctx_attn_kernel(sink_ref, qg_ref, kg_ref, vg_ref, qw_ref, kw_ref, vw_ref, qn_ref, kn_ref,
                     og_ref, ow_ref, gk_ref, gv_ref, wk_ref, wv_ref):
    kvh = pl.program_id(1)
    kg = _rms(kg_ref[...], kn_ref[...])
    gk_ref[...] = kg
    gv_ref[...] = vg_ref[...]
    wk_ref[...] = kw_ref[...]
    wv_ref[...] = vw_ref[...]
    kg_b, vg_b = kg.astype(BF16), vg_ref[...].astype(BF16)
    kw_b, vw_b = kw_ref[...].astype(BF16), vw_ref[...].astype(BF16)
    for h in range(N_GROUP):
        cols = slice(h * HEAD_DIM, (h + 1) * HEAD_DIM)
        qg = _rms(qg_ref[:, cols], qn_ref[...]).astype(BF16)
        og_ref[:, cols] = _softmax_attend(qg, kg_b, vg_b, None).astype(BF16)
        sink = sink_ref[kvh * N_GROUP + h]
        ow_ref[:, cols] = _softmax_attend(qw_ref[:, cols].astype(BF16), kw_b, vw_b, sink).astype(BF16)


def _ctx_attn(z, sink, qn_g, kn_g, nb, seq):
    q_spec = lambda off: pl.BlockSpec((seq, GQ_W), lambda b, k: (b, off // GQ_W + k))
    kv_spec = lambda off: pl.BlockSpec((seq, HEAD_DIM), lambda b, k: (b, off // HEAD_DIM + k))
    vec = pl.BlockSpec((1, HEAD_DIM), lambda b, k: (0, 0))
    o_spec = pl.BlockSpec((seq, GQ_W), lambda b, k: (b, k))
    c_spec = pl.BlockSpec((seq, HEAD_DIM), lambda b, k: (b, k))
    m = nb * seq
    return pl.pallas_call(
        _ctx_attn_kernel,
        grid=(nb, N_KV),
        in_specs=[pl.BlockSpec(memory_space=pltpu.SMEM),
                  q_spec(OFF_QG), kv_spec(OFF_KG), kv_spec(OFF_VG),
                  q_spec(OFF_QW), kv_spec(OFF_KW), kv_spec(OFF_VW), vec, vec],
        out_specs=(o_spec, o_spec, c_spec, c_spec, c_spec, c_spec),
        out_shape=(jax.ShapeDtypeStruct((m, Q_W), BF16), jax.ShapeDtypeStruct((m, Q_W), BF16))
        + tuple(jax.ShapeDtypeStruct((m, KV_W), F32) for _ in range(4)),
        compiler_params=_cparams("parallel", "parallel"), name="ctx_attn",
    )(sink, z, z, z, z, z, z, qn_g.reshape(1, HEAD_DIM), kn_g.reshape(1, HEAD_DIM))


def _rope(x, cos, sin, low_half):
    partner = jnp.where(low_half, pltpu.roll(x, HEAD_DIM - 32, axis=1), pltpu.roll(x, 32, axis=1))
    return x * cos + partner * sin


def _low_half(rows):
    lane = lax.broadcasted_iota(jnp.int32, (rows, HEAD_DIM), 1)
    return (lane & 63) < 32


def _lat_global_kernel(q_ref, k_ref, v_ref, ck_ref, cv_ref, cos_ref, sin_ref, qn_ref, kn_ref,
                       o_ref, kall, vall, *, tq):
    seq = q_ref.shape[0]
    past = ck_ref.shape[0]
    kall[0:past, :] = ck_ref[...].astype(BF16)
    vall[0:past, :] = cv_ref[...].astype(BF16)
    kn = _rope(_rms(k_ref[...], kn_ref[...]), cos_ref[...], sin_ref[...], _low_half(seq))
    kall[past:past + seq, :] = kn.astype(BF16)
    vall[past:past + seq, :] = v_ref[...].astype(BF16)
    low = _low_half(tq)

    def q_block(qb, carry):
        rows = pl.ds(pl.multiple_of(qb * tq, tq), tq)
        cos, sin = cos_ref[rows, :], sin_ref[rows, :]
        for h in range(N_GROUP):
            cols = slice(h * HEAD_DIM, (h + 1) * HEAD_DIM)
            q = _rope(_rms(q_ref[rows, cols], qn_ref[...]), cos, sin, low).astype(BF16)
            o_ref[rows, cols] = _softmax_attend(q, kall[...], vall[...], None).astype(BF16)
        return carry

    lax.fori_loop(0, seq // tq, q_block, 0)


def _lat_global(z, cache_k, cache_v, layer, cos, sin, qn_g, kn_g, nb, seq, tq=256):
    past = cache_k.shape[2]
    q_spec = pl.BlockSpec((seq, GQ_W), lambda b, k: (b, OFF_QG // GQ_W + k))
    kv_spec = lambda off: pl.BlockSpec((seq, HEAD_DIM), lambda b, k: (b, off // HEAD_DIM + k))
    c_spec = pl.BlockSpec((None, None, past, HEAD_DIM), lambda b, k: (b, layer, 0, k))
    tab = pl.BlockSpec((seq, HEAD_DIM), lambda b, k: (0, 0))
    vec = pl.BlockSpec((1, HEAD_DIM), lambda b, k: (0, 0))
    return pl.pallas_call(
        functools.partial(_lat_global_kernel, tq=tq),
        grid=(nb, N_KV),
        in_specs=[q_spec, kv_spec(OFF_KG), kv_spec(OFF_VG), c_spec, c_spec, tab, tab, vec, vec],
        out_specs=pl.BlockSpec((seq, GQ_W), lambda b, k: (b, k)),
        out_shape=jax.ShapeDtypeStruct((nb * seq, Q_W), BF16),
        scratch_shapes=[pltpu.VMEM((past + seq, HEAD_DIM), BF16), pltpu.VMEM((past + seq, HEAD_DIM), BF16)],
        compiler_params=_cparams("parallel", "parallel"), name="lat_global",
    )(z, z, z, cache_k, cache_v, cos, sin, qn_g.reshape(1, HEAD_DIM), kn_g.reshape(1, HEAD_DIM))


def _lat_window_kernel(sink_ref, q_ref, k_ref, v_ref, ck_ref, cv_ref, cos_ref, sin_ref,
                       o_ref, kpad, vpad):
    seq = q_ref.shape[0]
    blk = WINDOW
    kvh = pl.program_id(1)
    zeros = jnp.zeros((blk, HEAD_DIM), BF16)
    kpad[0:blk, :] = zeros
    vpad[0:blk, :] = zeros
    kpad[blk + seq:2 * blk + seq, :] = zeros
    vpad[blk + seq:2 * blk + seq, :] = zeros
    kpad[blk:blk + seq, :] = _rope(k_ref[...], cos_ref[...], sin_ref[...], _low_half(seq)).astype(BF16)
    vpad[blk:blk + seq, :] = v_ref[...].astype(BF16)
    ck = ck_ref[...].astype(BF16)
    cv = cv_ref[...].astype(BF16)
    low = _low_half(blk)
    rows4 = N_GROUP * blk
    head = lax.broadcasted_iota(jnp.int32, (rows4, 1), 0) // blk
    sink = jnp.zeros((rows4, 1), F32)
    for h in range(N_GROUP):
        sink = jnp.where(head == h, sink_ref[kvh * N_GROUP + h], sink)
    qi = lax.broadcasted_iota(jnp.int32, (rows4, 3 * blk), 0) & (blk - 1)
    kj = lax.broadcasted_iota(jnp.int32, (rows4, 3 * blk), 1)
    band_bias = jnp.where(jnp.abs(qi + blk - kj) <= WINDOW, 0.0, NEG_INF)

    def q_block(n, carry):
        r0 = pl.multiple_of(n * blk, blk)
        rows = pl.ds(r0, blk)
        cos, sin = cos_ref[rows, :], sin_ref[rows, :]
        q4 = jnp.concatenate(
            [_rope(q_ref[rows, h * HEAD_DIM:(h + 1) * HEAD_DIM], cos, sin, low) for h in range(N_GROUP)],
            axis=0).astype(BF16)
        in_seq = ((n - 1) * blk + kj).astype(jnp.uint32) < seq
        s_loc = jnp.where(in_seq, _dot_nt(q4, kpad[pl.ds(r0, 3 * blk), :]) * ATTN_SCALE + band_bias, NEG_INF)
        s_ctx = _dot_nt(q4, ck) * ATTN_SCALE
        m = jnp.maximum(jnp.maximum(jnp.max(s_loc, axis=-1, keepdims=True),
                                    jnp.max(s_ctx, axis=-1, keepdims=True)), sink)
        e_loc = jnp.exp(s_loc - m)
        e_ctx = jnp.exp(s_ctx - m)
        den = (jnp.sum(e_loc, axis=-1, keepdims=True) + jnp.sum(e_ctx, axis=-1, keepdims=True)
               + jnp.exp(sink - m))
        o4 = (_dot(e_loc.astype(BF16), vpad[pl.ds(r0, 3 * blk), :]) + _dot(e_ctx.astype(BF16), cv)) * (1.0 / den)
        for h in range(N_GROUP):
            o_ref[rows, h * HEAD_DIM:(h + 1) * HEAD_DIM] = o4[h * blk:(h + 1) * blk].astype(BF16)
        return carry

    lax.fori_loop(0, seq // blk, q_block, 0)


def _lat_window(z, cache_k, cache_v, layer, cos, sin, sink, nb, seq):
    past = cache_k.shape[2]
    q_spec = pl.BlockSpec((seq, GQ_W), lambda b, k: (b, OFF_QW // GQ_W + k))
    kv_spec = lambda off: pl.BlockSpec((seq, HEAD_DIM), lambda b, k: (b, off // HEAD_DIM + k))
    c_spec = pl.BlockSpec((None, None, past, HEAD_DIM), lambda b, k: (b, layer, 0, k))
    tab = pl.BlockSpec((seq, HEAD_DIM), lambda b, k: (0, 0))
    return pl.pallas_call(
        _lat_window_kernel,
        grid=(nb, N_KV),
        in_specs=[pl.BlockSpec(memory_space=pltpu.SMEM), q_spec, kv_spec(OFF_KW), kv_spec(OFF_VW),
                  c_spec, c_spec, tab, tab],
        out_specs=pl.BlockSpec((seq, GQ_W), lambda b, k: (b, k)),
        out_shape=jax.ShapeDtypeStruct((nb * seq, Q_W), BF16),
        scratch_shapes=[pltpu.VMEM((seq + 2 * WINDOW, HEAD_DIM), BF16),
                        pltpu.VMEM((seq + 2 * WINDOW, HEAD_DIM), BF16)],
        compiler_params=_cparams("parallel", "parallel"), name="lat_window",
    )(sink, z, z, z, cache_k, cache_v, cos, sin)


def _interleave(v, n_tiles):
    return jnp.swapaxes(v.reshape(SUBLANE, n_tiles, v.shape[-1]), 0, 1).reshape(v.shape)


def _deinterleave(v, n_tiles):
    return jnp.swapaxes(v.reshape(n_tiles, SUBLANE, v.shape[-1]), 0, 1).reshape(v.shape)


def _lru_kernel(x_ref, y_ref, cw_ref, cb_ref, lam_ref, wa_ref, ba_ref, wx_ref, bx_ref, h0_ref,
                o_ref, st_ref, a_f, b_f, a_b, b_b):
    seq, width = x_ref.shape
    n_tiles = seq // SUBLANE
    sub = lax.broadcasted_iota(jnp.int32, (SUBLANE, width), 0)

    def from_prev_segment(tile):
        return jnp.where(sub >= 1, pltpu.roll(tile, 1, axis=0), 0.0)

    def from_next_segment(tile):
        return jnp.where(sub < SUBLANE - 1, pltpu.roll(tile, SUBLANE - 1, axis=0), 0.0)

    x = _interleave(x_ref[...], n_tiles)
    last, last2, first = x[seq - SUBLANE:], x[seq - 2 * SUBLANE:seq - SUBLANE], x[:SUBLANE]
    xm1 = jnp.concatenate([from_prev_segment(last), x[:seq - SUBLANE]], axis=0)
    xm2 = jnp.concatenate([from_prev_segment(last2), from_prev_segment(last), x[:seq - 2 * SUBLANE]], axis=0)
    xp1 = jnp.concatenate([x[SUBLANE:], from_next_segment(first)], axis=0)
    xc = (cb_ref[...] + xm2 * cw_ref[0:1, :] + xm1 * cw_ref[1:2, :] + x * cw_ref[2:3, :] + xp1 * cw_ref[3:4, :])
    xcb = xc.astype(BF16)

    def block_diag(w_ref, d):
        return jnp.concatenate(
            [_dot(xcb[:, j * LRU_BW:(j + 1) * LRU_BW], w_ref[d, j]) for j in range(width // LRU_BW)], axis=1)

    for d, (a_s, b_s) in enumerate(((a_f, b_f), (a_b, b_b))):
        r = jax.nn.sigmoid(block_diag(wa_ref, d) + ba_ref[d:d + 1, :])
        gate_i = jax.nn.sigmoid(block_diag(wx_ref, d) + bx_ref[d:d + 1, :])
        neg_lam = -lam_ref[d:d + 1, :]
        softplus = jnp.maximum(neg_lam, 0.0) + jnp.log1p(jnp.exp(-jnp.abs(neg_lam)))
        log_a = -LRU_C * r * softplus
        a_s[...] = jnp.exp(log_a)
        th = jnp.tanh(log_a)
        b_s[...] = jnp.sqrt(-2.0 * th / (1.0 - th)) * (gate_i * xc)

    def local_step(i, carry):
        hf, pf, hb, pb = carry
        rf = pl.ds(pl.multiple_of(i * SUBLANE, SUBLANE), SUBLANE)
        rb = pl.ds(pl.multiple_of((n_tiles - 1 - i) * SUBLANE, SUBLANE), SUBLANE)
        af, ab = a_f[rf, :], a_b[rb, :]
        hf, pf = af * hf + b_f[rf, :], af * pf
        hb, pb = ab * hb + b_b[rb, :], ab * pb
        a_f[rf, :], b_f[rf, :] = pf, hf
        a_b[rb, :], b_b[rb, :] = pb, hb
        return hf, pf, hb, pb

    zeros, ones = jnp.zeros((SUBLANE, width), F32), jnp.ones((SUBLANE, width), F32)
    hf, pf, hb, pb = lax.fori_loop(0, n_tiles, local_step, (zeros, ones, zeros, ones), unroll=8)

    for s in (1, 2, 4):
        take_f, take_b = sub >= s, sub < SUBLANE - s
        hf = jnp.where(take_f, pf * pltpu.roll(hf, s, axis=0) + hf, hf)
        pf = jnp.where(take_f, pf * pltpu.roll(pf, s, axis=0), pf)
        hb = jnp.where(take_b, pb * pltpu.roll(hb, SUBLANE - s, axis=0) + hb, hb)
        pb = jnp.where(take_b, pb * pltpu.roll(pb, SUBLANE - s, axis=0), pb)
    h0f = jnp.broadcast_to(h0_ref[0:1, :], (SUBLANE, width))
    h0b = jnp.broadcast_to(h0_ref[1:2, :], (SUBLANE, width))
    out_f = pf * h0f + hf
    out_b = pb * h0b + hb
    st_ref[0:1, :] = out_f[SUBLANE - 1:SUBLANE, :]
    st_ref[1:2, :] = out_b[0:1, :]
    in_f = jnp.where(sub >= 1, pltpu.roll(out_f, 1, axis=0), h0f)
    in_b = jnp.where(sub < SUBLANE - 1, pltpu.roll(out_b, SUBLANE - 1, axis=0), h0b)

    def tiles(ref):
        return ref[...].reshape(n_tiles, SUBLANE, width)

    h = (tiles(b_f) + tiles(a_f) * in_f[None]) + (tiles(b_b) + tiles(a_b) * in_b[None])
    h = _deinterleave(h.reshape(seq, width), n_tiles)
    o_ref[...] = (h * jax.nn.gelu(y_ref[...])).astype(BF16)


def _lru(z, h0, h0_layer, conv_w, conv_b, lam, wa, ba, wx, bx, nb, seq, cw=256):
    nc = LRU_W // cw
    nd = cw // LRU_BW
    vec2 = pl.BlockSpec((2, cw), lambda b, c: (0, c))
    wspec = pl.BlockSpec((2, nd, LRU_BW, LRU_BW), lambda b, c: (0, c, 0, 0))
    return pl.pallas_call(
        _lru_kernel,
        grid=(nb, nc),
        in_specs=[pl.BlockSpec((seq, cw), lambda b, c: (b, OFF_XL // cw + c)),
                  pl.BlockSpec((seq, cw), lambda b, c: (b, OFF_YL // cw + c)),
                  pl.BlockSpec((4, cw), lambda b, c: (0, c)),
                  pl.BlockSpec((1, cw), lambda b, c: (0, c)),
                  vec2, wspec, vec2, wspec, vec2,
                  pl.BlockSpec((None, None, 2, cw), lambda b, c: (b, h0_layer, 0, c))],
        out_specs=(pl.BlockSpec((seq, cw), lambda b, c: (b, c)),
                   pl.BlockSpec((None, 2, cw), lambda b, c: (b, 0, c))),
        out_shape=(jax.ShapeDtypeStruct((nb * seq, LRU_W), BF16), jax.ShapeDtypeStruct((nb, 2, LRU_W), F32)),
        scratch_shapes=[pltpu.VMEM((seq, cw), F32)] * 4,
        compiler_params=_cparams("parallel", "parallel"), name="lru",
    )(z, z, conv_w, conv_b.reshape(1, LRU_W), lam, wa, ba, wx, bx, h0)


MERGE_ROW_CHUNK = 128


def _merge_kernel(b0, b1, b2, b3, w_ref, g0, g1, g2, g3, o_ref):
    for r in range(0, o_ref.shape[0], MERGE_ROW_CHUNK):
        rows = slice(r, r + MERGE_ROW_CHUNK)
        acc = None
        for n, (br, gr) in enumerate(((b0, g0), (b1, g1), (b2, g2), (b3, g3))):
            t = jax.nn.sigmoid(gr[rows, :]) * _dot(br[rows, :], w_ref[n])
            acc = t if acc is None else acc + t
        o_ref[rows, :] = acc.astype(BF16)


def _merge(branches, w_br, z, tm=2048, tn=256):
    m = z.shape[0]
    tm = min(tm, m)
    br = pl.BlockSpec((tm, BRANCH_W), lambda i, j: (i, 0), pipeline_mode=pl.Buffered(1))
    gate = lambda n: pl.BlockSpec((tm, tn), lambda i, j: (i, (OFF_GATE + n * D_MODEL) // tn + j))
    return pl.pallas_call(
        _merge_kernel,
        grid=(m // tm, D_MODEL // tn),
        in_specs=[br, br, br, br, pl.BlockSpec((N_BRANCH, BRANCH_W, tn), lambda i, j: (0, 0, j)),
                  gate(0), gate(1), gate(2), gate(3)],
        out_specs=pl.BlockSpec((tm, tn), lambda i, j: (i, j)),
        out_shape=jax.ShapeDtypeStruct((m, D_MODEL), BF16),
        compiler_params=_cparams("parallel", "arbitrary"), name="merge",
    )(*branches, w_br, z, z, z, z)


UP_ROW_CHUNK = 128


def _up_kernel(x_ref, wu_ref, wg_ref, cwu_ref, cwg_ref, cbu_ref, cbg_ref, o_ref, *, seq):
    rows, tn = o_ref.shape
    sub = lax.broadcasted_iota(jnp.int32, (SUBLANE, tn), 0)

    def conv(v, w_ref, b_ref):
        prev, nxt = [], []
        for q in range(rows // seq):
            vq = v[q * seq:(q + 1) * seq]
            head, tail = vq[:SUBLANE], vq[seq - SUBLANE:]
            prev += [jnp.where(sub >= 1, pltpu.roll(tail, 1, axis=0), 0.0), vq[:seq - SUBLANE]]
            nxt += [vq[SUBLANE:], jnp.where(sub < SUBLANE - 1, pltpu.roll(head, SUBLANE - 1, axis=0), 0.0)]
        prev, nxt = jnp.concatenate(prev, axis=0), jnp.concatenate(nxt, axis=0)
        return b_ref[...] + prev * w_ref[0:1, :] + v * w_ref[1:2, :] + nxt * w_ref[2:3, :]

    wu, wg = wu_ref[...].astype(BF16), wg_ref[...].astype(BF16)

    def row_chunked_dot(w):
        return jnp.concatenate([_dot(x_ref[r:r + UP_ROW_CHUNK, :], w) for r in range(0, rows, UP_ROW_CHUNK)],
                               axis=0)

    u = conv(row_chunked_dot(wu), cwu_ref, cbu_ref)
    g = conv(row_chunked_dot(wg), cwg_ref, cbg_ref)
    o_ref[...] = (g * jax.nn.sigmoid(g) * u).astype(BF16)


def _up(h, w_up, conv_w, conv_b, layer, seq, tm=2048, tn=256):
    m = h.shape[0]
    tm = min(tm, m)
    nj = D_FF // tn
    w_spec = lambda off: pl.BlockSpec((None, D_MODEL, tn), lambda i, j: (layer, 0, off + j))
    cw_spec = lambda off: pl.BlockSpec((None, 3, tn), lambda i, j: (layer, 0, off + j))
    cb_spec = lambda off: pl.BlockSpec((None, 1, tn), lambda i, j: (layer, 0, off + j))
    conv_b3 = conv_b.reshape(DEPTH, 1, 2 * D_FF)
    return pl.pallas_call(
        functools.partial(_up_kernel, seq=seq),
        grid=(m // tm, nj),
        in_specs=[pl.BlockSpec((tm, D_MODEL), lambda i, j: (i, 0), pipeline_mode=pl.Buffered(1)),
                  w_spec(0), w_spec(nj),
                  cw_spec(0), cw_spec(nj), cb_spec(0), cb_spec(nj)],
        out_specs=pl.BlockSpec((tm, tn), lambda i, j: (i, j)),
        out_shape=jax.ShapeDtypeStruct((m, D_FF), BF16),
        compiler_params=_cparams("parallel", "arbitrary"), name="ffn_up",
    )(h, w_up, w_up, conv_w, conv_w, conv_b3, conv_b3)


def _rope_tables(seq):
    t = jnp.arange(seq)
    half = HEAD_DIM // 2
    inv = ROPE_THETA ** (-jnp.arange(0, half, 2, dtype=F32) / half)
    ang_r = (t // GRID_W).astype(F32)[:, None] * inv[None, :]
    ang_c = (t % GRID_W).astype(F32)[:, None] * inv[None, :]
    cos = jnp.concatenate([jnp.cos(ang_r), jnp.cos(ang_r), jnp.cos(ang_c), jnp.cos(ang_c)], axis=1)
    sin = jnp.concatenate([-jnp.sin(ang_r), jnp.sin(ang_r), -jnp.sin(ang_c), jnp.sin(ang_c)], axis=1)
    return cos, sin


def kernel(x_prompt, x_sample, cache_gk, cache_gv, cache_wk, cache_wv, state_lru, c, c_ctx, w_ada, b_ada, w_in, b_in, pool_w, pool_scale, qn_g, kn_g, sink, lru_conv_w, lru_conv_b, lru_lambda, lru_wa, lru_ba, lru_wx, lru_bx, w_br, w_o, ln1_g, ln1_b, w_up, ffn_conv_w, ffn_conv_b, w_down, ln2_g, ln2_b):
    nb_c, seq_c, _ = x_prompt.shape
    nb_s, seq_s, _ = x_sample.shape
    past = cache_gk.shape[2]

    c_all = jnp.concatenate([c_ctx[None, :], c, jnp.zeros((ADA_ROWS - 1 - nb_s, D_MODEL), F32)], axis=0)
    ada = _ada(c_all, w_ada, b_ada)
    ada3 = [ada[l].reshape(ADA_ROWS * 6, 1, D_MODEL) for l in range(DEPTH)]
    cos, sin = _rope_tables(seq_s)
    caches = [a.reshape(nb_s, DEPTH, past, KV_W) for a in (cache_gk, cache_gv, cache_wk, cache_wv)]
    zero_state = jnp.zeros((nb_c, 1, 2, LRU_W), F32)

    groups = [dict(x=x_prompt.reshape(nb_c * seq_c, D_MODEL), nb=nb_c, seq=seq_c, cond=None),
              dict(x=x_sample.reshape(nb_s * seq_s, D_MODEL), nb=nb_s, seq=seq_s, cond=seq_s)]
    for g in groups:
        g["h"] = _lnmod(g["x"], ada3[0], g["cond"])
    new_cache = [[] for _ in range(5)]

    for l in range(DEPTH):
        w_br_l = _cast_layer(w_br.reshape(DEPTH, N_BRANCH * BRANCH_W, D_MODEL), l, 512).reshape(
            N_BRANCH, BRANCH_W, D_MODEL)
        w_down_l = _cast_layer(w_down, l, D_FF // 16)
        pool_w_l = pool_w[l].astype(BF16)
        wa_l = lru_wa[l].astype(BF16)
        wx_l = lru_wx[l].astype(BF16)
        for gi, g in enumerate(groups):
            nb, seq = g["nb"], g["seq"]
            z = _matmul(g["h"], w_in, b_in[l], 2048, 512, "in_proj", layer=l, single_buffer_x=True)
            o_pool = _pool(z, pool_w_l, pool_scale[l], nb, seq)
            if gi == 0:
                o_g, o_w, gk, gv, wk, wv = _ctx_attn(z, sink[l], qn_g[l], kn_g[l], nb, seq)
                h0, h0_layer = zero_state, 0
            else:
                o_g = _lat_global(z, caches[0], caches[1], l, cos, sin, qn_g[l], kn_g[l], nb, seq)
                o_w = _lat_window(z, caches[2], caches[3], l, cos, sin, sink[l], nb, seq)
                h0, h0_layer = state_lru, l
            o_lru, st = _lru(z, h0, h0_layer, lru_conv_w[l], lru_conv_b[l], lru_lambda[l],
                             wa_l, lru_ba[l], wx_l, lru_bx[l], nb, seq)
            if gi == 0:
                for lst, a in zip(new_cache, (gk, gv, wk, wv, st)):
                    lst.append(a)
            merged = _merge((o_pool, o_g, o_w, o_lru), w_br_l, z)
            mix = _matmul(merged, w_o, None, 2048, 512, "out_proj", layer=l, single_buffer_x=True)
            latent = gi == 1
            x1, h2 = _post(g["x"], mix, ada3[l], 2, ln1_g[l], ln1_b[l], nb, seq, latent,
                           nxt=(ada3[l], 3, 4), h_interleaved=True)
            act = _up(h2, w_up, ffn_conv_w, ffn_conv_b, l, seq)
            ff = _matmul(act, w_down_l, None, 512, 512, "ffn_down")
            nxt = (ada3[l + 1], 0, 1) if l + 1 < DEPTH else None
            g["x"], g["h"] = _post(x1, ff, ada3[l], 5, ln2_g[l], ln2_b[l], nb, seq, latent,
                                   nxt=nxt, y_interleaved=True)

    y_prompt = groups[0]["x"].reshape(nb_c, seq_c, D_MODEL)
    y_sample = groups[1]["x"].reshape(nb_s, seq_s, D_MODEL)
    kv = [jnp.stack([a.reshape(nb_c, seq_c, N_KV, HEAD_DIM) for a in new_cache[i]], axis=1) for i in range(4)]
    new_lru = jnp.stack(new_cache[4], axis=1)
    return (y_prompt, y_sample, kv[0], kv[1], kv[2], kv[3], new_lru)
```

```python
import functools

import jax
import jax.numpy as jnp
from jax import lax
from jax.experimental import pallas as pl
from jax.experimental.pallas import tpu as pltpu

F32 = jnp.float32
BF16 = jnp.bfloat16

D_MODEL = 4096
DEPTH = 2
GRID_W = 64
BRANCH_W = D_MODEL // 4
N_BRANCH = 4
HEAD_DIM = 128
N_HEADS = BRANCH_W // HEAD_DIM
N_KV = N_HEADS // 4
N_GROUP = N_HEADS // N_KV
Q_W = N_HEADS * HEAD_DIM
KV_W = N_KV * HEAD_DIM
WINDOW = 128
ROPE_THETA = 10000.0
ATTN_SCALE = HEAD_DIM ** -0.5
NEG_INF = -1e30
POOL_WINDOWS = (2, 4, 8, 16)
POOL_GW = BRANCH_W // len(POOL_WINDOWS)
LRU_W = BRANCH_W
LRU_BW = 128
LRU_C = 8.0
D_FF = 11008
ALPHA = (2 * DEPTH) ** 0.25
IN_SIZES = (BRANCH_W, Q_W, KV_W, KV_W, Q_W, KV_W, KV_W, LRU_W, LRU_W, N_BRANCH * D_MODEL)
IN_W = sum(IN_SIZES)

OFF_POOL = 0
OFF_QG = OFF_POOL + BRANCH_W
OFF_KG = OFF_QG + Q_W
OFF_VG = OFF_KG + KV_W
OFF_QW = OFF_VG + KV_W
OFF_KW = OFF_QW + Q_W
OFF_VW = OFF_KW + KV_W
OFF_XL = OFF_VW + KV_W
OFF_YL = OFF_XL + LRU_W
OFF_GATE = OFF_YL + LRU_W

LANE = 128
SUBLANE = 8
ADA_ROWS = 16
GQ_W = N_GROUP * HEAD_DIM
VMEM_LIMIT = 56 << 20


def _cparams(*sem):
    return pltpu.CompilerParams(dimension_semantics=sem, vmem_limit_bytes=VMEM_LIMIT)


def _dot(a, b):
    return jnp.dot(a, b, preferred_element_type=F32)


def _dot_nt(a, b):
    return lax.dot_general(a, b, (((1,), (1,)), ((), ())), preferred_element_type=F32)


def _ln(x, eps=1e-6):
    mu = jnp.mean(x, axis=-1, keepdims=True)
    xc = x - mu
    var = jnp.mean(xc * xc, axis=-1, keepdims=True)
    return xc * lax.rsqrt(var + eps)


def _rms(x, g, eps=1e-6):
    return x * lax.rsqrt(jnp.mean(x * x, axis=-1, keepdims=True) + eps) * g


def _ada_kernel(c_ref, w_ref, b_ref, o_ref):
    c = c_ref[...]
    s = (c * jax.nn.sigmoid(c)).astype(BF16)
    o_ref[...] = _dot(s, w_ref[...].astype(BF16)) + b_ref[...]


def _ada(c_all, w_ada, b_ada, tn=512):
    n = w_ada.shape[-1]
    return pl.pallas_call(
        _ada_kernel,
        grid=(DEPTH, n // tn),
        in_specs=[pl.BlockSpec((ADA_ROWS, D_MODEL), lambda l, j: (0, 0)),
                  pl.BlockSpec((None, D_MODEL, tn), lambda l, j: (l, 0, j)),
                  pl.BlockSpec((None, 1, tn), lambda l, j: (l, 0, j))],
        out_specs=pl.BlockSpec((None, ADA_ROWS, tn), lambda l, j: (l, 0, j)),
        out_shape=jax.ShapeDtypeStruct((DEPTH, ADA_ROWS, n), F32),
        compiler_params=_cparams("parallel", "parallel"),
        name="ada",
    )(c_all, w_ada, b_ada.reshape(DEPTH, 1, n))


def _ada_spec(which, tm, rows_per_cond):
    if rows_per_cond is None:
        return pl.BlockSpec((None, 1, D_MODEL), lambda i: (which, 0, 0))
    return pl.BlockSpec((None, 1, D_MODEL), lambda i: ((1 + (i * tm) // rows_per_cond) * 6 + which, 0, 0))


def _lnmod_kernel(x_ref, sh_ref, sc_ref, h_ref):
    h_ref[...] = (_ln(x_ref[...]) * (1.0 + sc_ref[...]) + sh_ref[...]).astype(BF16)


def _lnmod(x, ada3, rows_per_cond, tm=256):
    m = x.shape[0]
    return pl.pallas_call(
        _lnmod_kernel,
        grid=(m // tm,),
        in_specs=[pl.BlockSpec((tm, D_MODEL), lambda i: (i, 0)),
                  _ada_spec(0, tm, rows_per_cond), _ada_spec(1, tm, rows_per_cond)],
        out_specs=pl.BlockSpec((tm, D_MODEL), lambda i: (i, 0)),
        out_shape=jax.ShapeDtypeStruct((m, D_MODEL), BF16),
        compiler_params=_cparams("parallel"),
        name="lnmod",
    )(x, ada3, ada3)


def _post_kernel(x_ref, y_ref, g_ref, lg_ref, lb_ref, sh_ref, sc_ref, xo_ref, h_ref, *, y_interleaved, h_interleaved):
    y = y_ref[...]
    if y_interleaved:
        y = jnp.swapaxes(y, 0, 1)
    xn = _ln(ALPHA * x_ref[...] + g_ref[...] * y) * lg_ref[...] + lb_ref[...]
    xo_ref[...] = xn
    if h_ref is not None:
        h = _ln(xn) * (1.0 + sc_ref[...]) + sh_ref[...]
        if h_interleaved:
            h = jnp.swapaxes(h, 0, 1)
        h_ref[...] = h.astype(BF16)


def _post_last_kernel(x_ref, y_ref, g_ref, lg_ref, lb_ref, xo_ref, *, y_interleaved):
    _post_kernel(x_ref, y_ref, g_ref, lg_ref, lb_ref, None, None, xo_ref, None,
                 y_interleaved=y_interleaved, h_interleaved=False)


POST_ROWS = 32


def _post(x, y, ada3, gate_idx, ln_g, ln_b, nb, seq, latent, nxt=None, y_interleaved=False, h_interleaved=False):
    nt = seq // SUBLANE
    nat_shape, int_shape = (nb, SUBLANE, nt, D_MODEL), (nb, nt, SUBLANE, D_MODEL)
    nat = pl.BlockSpec((None, SUBLANE, POST_ROWS, D_MODEL), lambda q, j: (q, 0, j, 0))
    itl = pl.BlockSpec((None, POST_ROWS, SUBLANE, D_MODEL), lambda q, j: (q, j, 0, 0))
    vec = pl.BlockSpec((1, D_MODEL), lambda q, j: (0, 0))
    ada = lambda which: pl.BlockSpec((None, 1, D_MODEL),
                                     lambda q, j: (((1 + q) * 6 if latent else 0) + which, 0, 0))
    in_specs = [nat, itl if y_interleaved else nat, ada(gate_idx), vec, vec]
    args = [x.reshape(nat_shape), y.reshape(int_shape if y_interleaved else nat_shape), ada3,
            ln_g.reshape(1, D_MODEL), ln_b.reshape(1, D_MODEL)]
    grid = (nb, nt // POST_ROWS)
    if nxt is None:
        xo = pl.pallas_call(
            functools.partial(_post_last_kernel, y_interleaved=y_interleaved),
            grid=grid, in_specs=in_specs, out_specs=nat,
            out_shape=jax.ShapeDtypeStruct(nat_shape, F32),
            compiler_params=_cparams("parallel", "parallel"), name="post_last",
        )(*args)
        return xo.reshape(nb * seq, D_MODEL), None
    nxt_ada3, sh_idx, sc_idx = nxt
    in_specs += [ada(sh_idx), ada(sc_idx)]
    args += [nxt_ada3, nxt_ada3]
    xo, h = pl.pallas_call(
        functools.partial(_post_kernel, y_interleaved=y_interleaved, h_interleaved=h_interleaved),
        grid=grid, in_specs=in_specs, out_specs=(nat, itl if h_interleaved else nat),
        out_shape=(jax.ShapeDtypeStruct(nat_shape, F32),
                   jax.ShapeDtypeStruct(int_shape if h_interleaved else nat_shape, BF16)),
        compiler_params=_cparams("parallel", "parallel"), name="post",
    )(*args)
    return xo.reshape(nb * seq, D_MODEL), h.reshape(nb * seq, D_MODEL)


def _mm_bias_kernel(x_ref, w_ref, b_ref, o_ref):
    o_ref[...] = (_dot(x_ref[...], w_ref[...].astype(BF16)) + b_ref[...]).astype(o_ref.dtype)


def _mm_kernel(x_ref, w_ref, o_ref):
    o_ref[...] = _dot(x_ref[...], w_ref[...].astype(BF16)).astype(o_ref.dtype)


def _matmul(x, w, b, tm, tn, name, layer=None, single_buffer_x=False):
    m, k = x.shape
    n = w.shape[-1]
    tm = min(tm, m)
    x_mode = dict(pipeline_mode=pl.Buffered(1)) if single_buffer_x else {}
    if layer is None:
        w_spec = pl.BlockSpec((k, tn), lambda i, j: (0, j))
    else:
        w_spec = pl.BlockSpec((None, k, tn), lambda i, j: (layer, 0, j))
    in_specs = [pl.BlockSpec((tm, k), lambda i, j: (i, 0), **x_mode), w_spec]
    args = [x, w]
    body = _mm_kernel
    if b is not None:
        in_specs.append(pl.BlockSpec((1, tn), lambda i, j: (0, j)))
        args.append(b.reshape(1, n))
        body = _mm_bias_kernel
    return pl.pallas_call(
        body, grid=(m // tm, n // tn), in_specs=in_specs,
        out_specs=pl.BlockSpec((tm, tn), lambda i, j: (i, j)),
        out_shape=jax.ShapeDtypeStruct((m, n), F32),
        compiler_params=_cparams("parallel", "arbitrary"), name=name,
    )(*args)


def _cast_kernel(w_ref, o_ref):
    o_ref[...] = w_ref[...].astype(BF16)


def _cast_layer(w, layer, rows_per_step):
    _, r, c = w.shape
    return pl.pallas_call(
        _cast_kernel,
        grid=(r // rows_per_step,),
        in_specs=[pl.BlockSpec((None, rows_per_step, c), lambda i: (layer, i, 0))],
        out_specs=pl.BlockSpec((rows_per_step, c), lambda i: (i, 0)),
        out_shape=jax.ShapeDtypeStruct((r, c), BF16),
        compiler_params=_cparams("parallel"), name="cast_w",
    )(w)


POOL_ROW_BLOCK = 256
POOL_HALO = LANE


def _pool_kernel(u_ref, w_ref, sc_ref, o_ref):
    seq = u_ref.shape[0]
    rb = min(seq, POOL_ROW_BLOCK)
    for r0 in range(0, seq, rb):
        k0, k1 = max(0, r0 - POOL_HALO), min(seq, r0 + rb + POOL_HALO)
        t = r0 + lax.broadcasted_iota(jnp.int32, (rb, k1 - k0), 0)
        s = k0 + lax.broadcasted_iota(jnp.int32, (rb, k1 - k0), 1)
        d = s - t
        tcol = r0 + lax.broadcasted_iota(jnp.int32, (rb, 1), 0)
        for gi, win in enumerate(POOL_WINDOWS):
            half = win // 2
            cols = slice(gi * POOL_GW, (gi + 1) * POOL_GW)
            band = jnp.where((d >= -half) & (d < half), 1.0, 0.0).astype(BF16)
            cnt = jnp.minimum(tcol + half, seq) - jnp.maximum(tcol - half, 0)
            keys = u_ref[k0:k1, cols]
            k_hi = keys.astype(BF16)
            k_lo = (keys - k_hi.astype(F32)).astype(BF16)
            mean = (_dot(band, k_hi) + _dot(band, k_lo)) / cnt.astype(F32)
            y = _dot((mean - u_ref[r0:r0 + rb, cols]).astype(BF16), w_ref[gi])
            o_ref[r0:r0 + rb, cols] = (y * sc_ref[:, cols]).astype(BF16)


def _pool(z, pool_w, pool_scale, nb, seq):
    return pl.pallas_call(
        _pool_kernel,
        grid=(nb,),
        in_specs=[pl.BlockSpec((seq, BRANCH_W), lambda b: (b, OFF_POOL // BRANCH_W)),
                  pl.BlockSpec((len(POOL_WINDOWS), POOL_GW, POOL_GW), lambda b: (0, 0, 0)),
                  pl.BlockSpec((1, BRANCH_W), lambda b: (0, 0))],
        out_specs=pl.BlockSpec((seq, BRANCH_W), lambda b: (b, 0)),
        out_shape=jax.ShapeDtypeStruct((nb * seq, BRANCH_W), BF16),
        compiler_params=_cparams("parallel"), name="pool",
    )(z, pool_w, pool_scale.reshape(1, BRANCH_W))


def _softmax_attend(q, k, v, sink):
    s = _dot_nt(q, k) * ATTN_SCALE
    m = jnp.max(s, axis=-1, keepdims=True)
    if sink is not None:
        m = jnp.maximum(m, sink)
    e = jnp.exp(s - m)
    den = jnp.sum(e, axis=-1, keepdims=True)
    if sink is not None:
        den = den + jnp.exp(sink - m)
    return _dot(e.astype(BF16), v) * (1.0 / den)


def _ctx_attn_kernel(sink_ref, qg_ref, kg_ref, vg_ref, qw_ref, kw_ref, vw_ref, qn_ref, kn_ref,
                     og_ref, ow_ref, gk_ref, gv_ref, wk_ref, wv_ref):
    kvh = pl.program_id(1)
    kg = _rms(kg_ref[...], kn_ref[...])
    gk_ref[...] = kg
    gv_ref[...] = vg_ref[...]
    wk_ref[...] = kw_ref[...]
    wv_ref[...] = vw_ref[...]
    kg_b, vg_b = kg.astype(BF16), vg_ref[...].astype(BF16)
    kw_b, vw_b = kw_ref[...].astype(BF16), vw_ref[...].astype(BF16)
    for h in range(N_GROUP):
        cols = slice(h * HEAD_DIM, (h + 1) * HEAD_DIM)
        qg = _rms(qg_ref[:, cols], qn_ref[...]).astype(BF16)
        og_ref[:, cols] = _softmax_attend(qg, kg_b, vg_b, None).astype(BF16)
        sink = sink_ref[kvh * N_GROUP + h]
        ow_ref[:, cols] = _softmax_attend(qw_ref[:, cols].astype(BF16), kw_b, vw_b, sink).astype(BF16)


def _ctx_attn(z, sink, qn_g, kn_g, nb, seq):
    q_spec = lambda off: pl.BlockSpec((seq, GQ_W), lambda b, k: (b, off // GQ_W + k))
    kv_spec = lambda off: pl.BlockSpec((seq, HEAD_DIM), lambda b, k: (b, off // HEAD_DIM + k))
    vec = pl.BlockSpec((1, HEAD_DIM), lambda b, k: (0, 0))
    o_spec = pl.BlockSpec((seq, GQ_W), lambda b, k: (b, k))
    c_spec = pl.BlockSpec((seq, HEAD_DIM), lambda b, k: (b, k))
    m = nb * seq
    return pl.pallas_call(
        _ctx_attn_kernel,
        grid=(nb, N_KV),
        in_specs=[pl.BlockSpec(memory_space=pltpu.SMEM),
                  q_spec(OFF_QG), kv_spec(OFF_KG), kv_spec(OFF_VG),
                  q_spec(OFF_QW), kv_spec(OFF_KW), kv_spec(OFF_VW), vec, vec],
        out_specs=(o_spec, o_spec, c_spec, c_spec, c_spec, c_spec),
        out_shape=(jax.ShapeDtypeStruct((m, Q_W), BF16), jax.ShapeDtypeStruct((m, Q_W), BF16))
        + tuple(jax.ShapeDtypeStruct((m, KV_W), F32) for _ in range(4)),
        compiler_params=_cparams("parallel", "parallel"), name="ctx_attn",
    )(sink, z, z, z, z, z, z, qn_g.reshape(1, HEAD_DIM), kn_g.reshape(1, HEAD_DIM))


def _rope(x, cos, sin, low_half):
    partner = jnp.where(low_half, pltpu.roll(x, HEAD_DIM - 32, axis=1), pltpu.roll(x, 32, axis=1))
    return x * cos + partner * sin


def _low_half(rows):
    lane = lax.broadcasted_iota(jnp.int32, (rows, HEAD_DIM), 1)
    return (lane & 63) < 32


def _lat_global_kernel(q_ref, k_ref, v_ref, ck_ref, cv_ref, cos_ref, sin_ref, qn_ref, kn_ref,
                       o_ref, kall, vall, *, tq):
    seq = q_ref.shape[0]
    past = ck_ref.shape[0]
    kall[0:past, :] = ck_ref[...].astype(BF16)
    vall[0:past, :] = cv_ref[...].astype(BF16)
    kn = _rope(_rms(k_ref[...], kn_ref[...]), cos_ref[...], sin_ref[...], _low_half(seq))
    kall[past:past + seq, :] = kn.astype(BF16)
    vall[past:past + seq, :] = v_ref[...].astype(BF16)
    low = _low_half(tq)

    def q_block(qb, carry):
        rows = pl.ds(pl.multiple_of(qb * tq, tq), tq)
        cos, sin = cos_ref[rows, :], sin_ref[rows, :]
        for h in range(N_GROUP):
            cols = slice(h * HEAD_DIM, (h + 1) * HEAD_DIM)
            q = _rope(_rms(q_ref[rows, cols], qn_ref[...]), cos, sin, low).astype(BF16)
            o_ref[rows, cols] = _softmax_attend(q, kall[...], vall[...], None).astype(BF16)
        return carry

    lax.fori_loop(0, seq // tq, q_block, 0)


def _lat_global(z, cache_k, cache_v, layer, cos, sin, qn_g, kn_g, nb, seq, tq=256):
    past = cache_k.shape[2]
    q_spec = pl.BlockSpec((seq, GQ_W), lambda b, k: (b, OFF_QG // GQ_W + k))
    kv_spec = lambda off: pl.BlockSpec((seq, HEAD_DIM), lambda b, k: (b, off // HEAD_DIM + k))
    c_spec = pl.BlockSpec((None, None, past, HEAD_DIM), lambda b, k: (b, layer, 0, k))
    tab = pl.BlockSpec((seq, HEAD_DIM), lambda b, k: (0, 0))
    vec = pl.BlockSpec((1, HEAD_DIM), lambda b, k: (0, 0))
    return pl.pallas_call(
        functools.partial(_lat_global_kernel, tq=tq),
        grid=(nb, N_KV),
        in_specs=[q_spec, kv_spec(OFF_KG), kv_spec(OFF_VG), c_spec, c_spec, tab, tab, vec, vec],
        out_specs=pl.BlockSpec((seq, GQ_W), lambda b, k: (b, k)),
        out_shape=jax.ShapeDtypeStruct((nb * seq, Q_W), BF16),
        scratch_shapes=[pltpu.VMEM((past + seq, HEAD_DIM), BF16), pltpu.VMEM((past + seq, HEAD_DIM), BF16)],
        compiler_params=_cparams("parallel", "parallel"), name="lat_global",
    )(z, z, z, cache_k, cache_v, cos, sin, qn_g.reshape(1, HEAD_DIM), kn_g.reshape(1, HEAD_DIM))


def _lat_window_kernel(sink_ref, q_ref, k_ref, v_ref, ck_ref, cv_ref, cos_ref, sin_ref,
                       o_ref, kpad, vpad):
    seq = q_ref.shape[0]
    blk = WINDOW
    kvh = pl.program_id(1)
    zeros = jnp.zeros((blk, HEAD_DIM), BF16)
    kpad[0:blk, :] = zeros
    vpad[0:blk, :] = zeros
    kpad[blk + seq:2 * blk + seq, :] = zeros
    vpad[blk + seq:2 * blk + seq, :] = zeros
    kpad[blk:blk + seq, :] = _rope(k_ref[...], cos_ref[...], sin_ref[...], _low_half(seq)).astype(BF16)
    vpad[blk:blk + seq, :] = v_ref[...].astype(BF16)
    ck = ck_ref[...].astype(BF16)
    cv = cv_ref[...].astype(BF16)
    low = _low_half(blk)
    rows4 = N_GROUP * blk
    head = lax.broadcasted_iota(jnp.int32, (rows4, 1), 0) // blk
    sink = jnp.zeros((rows4, 1), F32)
    for h in range(N_GROUP):
        sink = jnp.where(head == h, sink_ref[kvh * N_GROUP + h], sink)
    qi = lax.broadcasted_iota(jnp.int32, (rows4, 3 * blk), 0) & (blk - 1)
    kj = lax.broadcasted_iota(jnp.int32, (rows4, 3 * blk), 1)
    band_bias = jnp.where(jnp.abs(qi + blk - kj) <= WINDOW, 0.0, NEG_INF)

    def q_block(n, carry):
        r0 = pl.multiple_of(n * blk, blk)
        rows = pl.ds(r0, blk)
        cos, sin = cos_ref[rows, :], sin_ref[rows, :]
        q4 = jnp.concatenate(
            [_rope(q_ref[rows, h * HEAD_DIM:(h + 1) * HEAD_DIM], cos, sin, low) for h in range(N_GROUP)],
            axis=0).astype(BF16)
        in_seq = ((n - 1) * blk + kj).astype(jnp.uint32) < seq
        s_loc = jnp.where(in_seq, _dot_nt(q4, kpad[pl.ds(r0, 3 * blk), :]) * ATTN_SCALE + band_bias, NEG_INF)
        s_ctx = _dot_nt(q4, ck) * ATTN_SCALE
        m = jnp.maximum(jnp.maximum(jnp.max(s_loc, axis=-1, keepdims=True),
                                    jnp.max(s_ctx, axis=-1, keepdims=True)), sink)
        e_loc = jnp.exp(s_loc - m)
        e_ctx = jnp.exp(s_ctx - m)
        den = (jnp.sum(e_loc, axis=-1, keepdims=True) + jnp.sum(e_ctx, axis=-1, keepdims=True)
               + jnp.exp(sink - m))
        o4 = (_dot(e_loc.astype(BF16), vpad[pl.ds(r0, 3 * blk), :]) + _dot(e_ctx.astype(BF16), cv)) * (1.0 / den)
        for h in range(N_GROUP):
            o_ref[rows, h * HEAD_DIM:(h + 1) * HEAD_DIM] = o4[h * blk:(h + 1) * blk].astype(BF16)
        return carry

    lax.fori_loop(0, seq // blk, q_block, 0)


def _lat_window(z, cache_k, cache_v, layer, cos, sin, sink, nb, seq):
    past = cache_k.shape[2]
    q_spec = pl.BlockSpec((seq, GQ_W), lambda b, k: (b, OFF_QW // GQ_W + k))
    kv_spec = lambda off: pl.BlockSpec((seq, HEAD_DIM), lambda b, k: (b, off // HEAD_DIM + k))
    c_spec = pl.BlockSpec((None, None, past, HEAD_DIM), lambda b, k: (b, layer, 0, k))
    tab = pl.BlockSpec((seq, HEAD_DIM), lambda b, k: (0, 0))
    return pl.pallas_call(
        _lat_window_kernel,
        grid=(nb, N_KV),
        in_specs=[pl.BlockSpec(memory_space=pltpu.SMEM), q_spec, kv_spec(OFF_KW), kv_spec(OFF_VW),
                  c_spec, c_spec, tab, tab],
        out_specs=pl.BlockSpec((seq, GQ_W), lambda b, k: (b, k)),
        out_shape=jax.ShapeDtypeStruct((nb * seq, Q_W), BF16),
        scratch_shapes=[pltpu.VMEM((seq + 2 * WINDOW, HEAD_DIM), BF16),
                        pltpu.VMEM((seq + 2 * WINDOW, HEAD_DIM), BF16)],
        compiler_params=_cparams("parallel", "parallel"), name="lat_window",
    )(sink, z, z, z, cache_k, cache_v, cos, sin)


def _interleave(v, n_tiles):
    return jnp.swapaxes(v.reshape(SUBLANE, n_tiles, v.shape[-1]), 0, 1).reshape(v.shape)


def _deinterleave(v, n_tiles):
    return jnp.swapaxes(v.reshape(n_tiles, SUBLANE, v.shape[-1]), 0, 1).reshape(v.shape)


def _lru_kernel(x_ref, y_ref, cw_ref, cb_ref, lam_ref, wa_ref, ba_ref, wx_ref, bx_ref, h0_ref,
                o_ref, st_ref, a_f, b_f, a_b, b_b):
    seq, width = x_ref.shape
    n_tiles = seq // SUBLANE
    sub = lax.broadcasted_iota(jnp.int32, (SUBLANE, width), 0)

    def from_prev_segment(tile):
        return jnp.where(sub >= 1, pltpu.roll(tile, 1, axis=0), 0.0)

    def from_next_segment(tile):
        return jnp.where(sub < SUBLANE - 1, pltpu.roll(tile, SUBLANE - 1, axis=0), 0.0)

    x = _interleave(x_ref[...], n_tiles)
    last, last2, first = x[seq - SUBLANE:], x[seq - 2 * SUBLANE:seq - SUBLANE], x[:SUBLANE]
    xm1 = jnp.concatenate([from_prev_segment(last), x[:seq - SUBLANE]], axis=0)
    xm2 = jnp.concatenate([from_prev_segment(last2), from_prev_segment(last), x[:seq - 2 * SUBLANE]], axis=0)
    xp1 = jnp.concatenate([x[SUBLANE:], from_next_segment(first)], axis=0)
    xc = (cb_ref[...] + xm2 * cw_ref[0:1, :] + xm1 * cw_ref[1:2, :] + x * cw_ref[2:3, :] + xp1 * cw_ref[3:4, :])
    xcb = xc.astype(BF16)

    def block_diag(w_ref, d):
        return jnp.concatenate(
            [_dot(xcb[:, j * LRU_BW:(j + 1) * LRU_BW], w_ref[d, j]) for j in range(width // LRU_BW)], axis=1)

    for d, (a_s, b_s) in enumerate(((a_f, b_f), (a_b, b_b))):
        r = jax.nn.sigmoid(block_diag(wa_ref, d) + ba_ref[d:d + 1, :])
        gate_i = jax.nn.sigmoid(block_diag(wx_ref, d) + bx_ref[d:d + 1, :])
        neg_lam = -lam_ref[d:d + 1, :]
        softplus = jnp.maximum(neg_lam, 0.0) + jnp.log1p(jnp.exp(-jnp.abs(neg_lam)))
        log_a = -LRU_C * r * softplus
        a_s[...] = jnp.exp(log_a)
        th = jnp.tanh(log_a)
        b_s[...] = jnp.sqrt(-2.0 * th / (1.0 - th)) * (gate_i * xc)

    def local_step(i, carry):
        hf, pf, hb, pb = carry
        rf = pl.ds(pl.multiple_of(i * SUBLANE, SUBLANE), SUBLANE)
        rb = pl.ds(pl.multiple_of((n_tiles - 1 - i) * SUBLANE, SUBLANE), SUBLANE)
        af, ab = a_f[rf, :], a_b[rb, :]
        hf, pf = af * hf + b_f[rf, :], af * pf
        hb, pb = ab * hb + b_b[rb, :], ab * pb
        a_f[rf, :], b_f[rf, :] = pf, hf
        a_b[rb, :], b_b[rb, :] = pb, hb
        return hf, pf, hb, pb

    zeros, ones = jnp.zeros((SUBLANE, width), F32), jnp.ones((SUBLANE, width), F32)
    hf, pf, hb, pb = lax.fori_loop(0, n_tiles, local_step, (zeros, ones, zeros, ones), unroll=8)

    for s in (1, 2, 4):
        take_f, take_b = sub >= s, sub < SUBLANE - s
        hf = jnp.where(take_f, pf * pltpu.roll(hf, s, axis=0) + hf, hf)
        pf = jnp.where(take_f, pf * pltpu.roll(pf, s, axis=0), pf)
        hb = jnp.where(take_b, pb * pltpu.roll(hb, SUBLANE - s, axis=0) + hb, hb)
        pb = jnp.where(take_b, pb * pltpu.roll(pb, SUBLANE - s, axis=0), pb)
    h0f = jnp.broadcast_to(h0_ref[0:1, :], (SUBLANE, width))
    h0b = jnp.broadcast_to(h0_ref[1:2, :], (SUBLANE, width))
    out_f = pf * h0f + hf
    out_b = pb * h0b + hb
    st_ref[0:1, :] = out_f[SUBLANE - 1:SUBLANE, :]
    st_ref[1:2, :] = out_b[0:1, :]
    in_f = jnp.where(sub >= 1, pltpu.roll(out_f, 1, axis=0), h0f)
    in_b = jnp.where(sub < SUBLANE - 1, pltpu.roll(out_b, SUBLANE - 1, axis=0), h0b)

    def tiles(ref):
        return ref[...].reshape(n_tiles, SUBLANE, width)

    h = (tiles(b_f) + tiles(a_f) * in_f[None]) + (tiles(b_b) + tiles(a_b) * in_b[None])
    h = _deinterleave(h.reshape(seq, width), n_tiles)
    o_ref[...] = (h * jax.nn.gelu(y_ref[...])).astype(BF16)


def _lru(z, h0, h0_layer, conv_w, conv_b, lam, wa, ba, wx, bx, nb, seq, cw=512):
    nc = LRU_W // cw
    nd = cw // LRU_BW
    vec2 = pl.BlockSpec((2, cw), lambda b, c: (0, c))
    wspec = pl.BlockSpec((2, nd, LRU_BW, LRU_BW), lambda b, c: (0, c, 0, 0))
    return pl.pallas_call(
        _lru_kernel,
        grid=(nb, nc),
        in_specs=[pl.BlockSpec((seq, cw), lambda b, c: (b, OFF_XL // cw + c)),
                  pl.BlockSpec((seq, cw), lambda b, c: (b, OFF_YL // cw + c)),
                  pl.BlockSpec((4, cw), lambda b, c: (0, c)),
                  pl.BlockSpec((1, cw), lambda b, c: (0, c)),
                  vec2, wspec, vec2, wspec, vec2,
                  pl.BlockSpec((None, None, 2, cw), lambda b, c: (b, h0_layer, 0, c))],
        out_specs=(pl.BlockSpec((seq, cw), lambda b, c: (b, c)),
                   pl.BlockSpec((None, 2, cw), lambda b, c: (b, 0, c))),
        out_shape=(jax.ShapeDtypeStruct((nb * seq, LRU_W), BF16), jax.ShapeDtypeStruct((nb, 2, LRU_W), F32)),
        scratch_shapes=[pltpu.VMEM((seq, cw), F32)] * 4,
        compiler_params=_cparams("parallel", "parallel"), name="lru",
    )(z, z, conv_w, conv_b.reshape(1, LRU_W), lam, wa, ba, wx, bx, h0)


MERGE_ROW_CHUNK = 128


def _merge_kernel(b0, b1, b2, b3, w_ref, g0, g1, g2, g3, o_ref):
    for r in range(0, o_ref.shape[0], MERGE_ROW_CHUNK):
        rows = slice(r, r + MERGE_ROW_CHUNK)
        acc = None
        for n, (br, gr) in enumerate(((b0, g0), (b1, g1), (b2, g2), (b3, g3))):
            t = jax.nn.sigmoid(gr[rows, :]) * _dot(br[rows, :], w_ref[n])
            acc = t if acc is None else acc + t
        o_ref[rows, :] = acc.astype(BF16)


def _merge(branches, w_br, z, tm=2048, tn=256):
    m = z.shape[0]
    tm = min(tm, m)
    br = pl.BlockSpec((tm, BRANCH_W), lambda i, j: (i, 0), pipeline_mode=pl.Buffered(1))
    gate = lambda n: pl.BlockSpec((tm, tn), lambda i, j: (i, (OFF_GATE + n * D_MODEL) // tn + j))
    return pl.pallas_call(
        _merge_kernel,
        grid=(m // tm, D_MODEL // tn),
        in_specs=[br, br, br, br, pl.BlockSpec((N_BRANCH, BRANCH_W, tn), lambda i, j: (0, 0, j)),
                  gate(0), gate(1), gate(2), gate(3)],
        out_specs=pl.BlockSpec((tm, tn), lambda i, j: (i, j)),
        out_shape=jax.ShapeDtypeStruct((m, D_MODEL), BF16),
        compiler_params=_cparams("parallel", "arbitrary"), name="merge",
    )(*branches, w_br, z, z, z, z)


UP_ROW_CHUNK = 128


def _up_kernel(x_ref, wu_ref, wg_ref, cwu_ref, cwg_ref, cbu_ref, cbg_ref, o_ref, *, seq):
    rows, tn = o_ref.shape
    sub = lax.broadcasted_iota(jnp.int32, (SUBLANE, tn), 0)

    def conv(v, w_ref, b_ref):
        prev, nxt = [], []
        for q in range(rows // seq):
            vq = v[q * seq:(q + 1) * seq]
            head, tail = vq[:SUBLANE], vq[seq - SUBLANE:]
            prev += [jnp.where(sub >= 1, pltpu.roll(tail, 1, axis=0), 0.0), vq[:seq - SUBLANE]]
            nxt += [vq[SUBLANE:], jnp.where(sub < SUBLANE - 1, pltpu.roll(head, SUBLANE - 1, axis=0), 0.0)]
        prev, nxt = jnp.concatenate(prev, axis=0), jnp.concatenate(nxt, axis=0)
        return b_ref[...] + prev * w_ref[0:1, :] + v * w_ref[1:2, :] + nxt * w_ref[2:3, :]

    wu, wg = wu_ref[...].astype(BF16), wg_ref[...].astype(BF16)

    def row_chunked_dot(w):
        return jnp.concatenate([_dot(x_ref[r:r + UP_ROW_CHUNK, :], w) for r in range(0, rows, UP_ROW_CHUNK)],
                               axis=0)

    u = conv(row_chunked_dot(wu), cwu_ref, cbu_ref)
    g = conv(row_chunked_dot(wg), cwg_ref, cbg_ref)
    o_ref[...] = (g * jax.nn.sigmoid(g) * u).astype(BF16)


def _up(h, w_up, conv_w, conv_b, layer, seq, tm=2048, tn=256):
    m = h.shape[0]
    tm = min(tm, m)
    nj = D_FF // tn
    w_spec = lambda off: pl.BlockSpec((None, D_MODEL, tn), lambda i, j: (layer, 0, off + j))
    cw_spec = lambda off: pl.BlockSpec((None, 3, tn), lambda i, j: (layer, 0, off + j))
    cb_spec = lambda off: pl.BlockSpec((None, 1, tn), lambda i, j: (layer, 0, off + j))
    conv_b3 = conv_b.reshape(DEPTH, 1, 2 * D_FF)
    return pl.pallas_call(
        functools.partial(_up_kernel, seq=seq),
        grid=(m // tm, nj),
        in_specs=[pl.BlockSpec((tm, D_MODEL), lambda i, j: (i, 0), pipeline_mode=pl.Buffered(1)),
                  w_spec(0), w_spec(nj),
                  cw_spec(0), cw_spec(nj), cb_spec(0), cb_spec(nj)],
        out_specs=pl.BlockSpec((tm, tn), lambda i, j: (i, j)),
        out_shape=jax.ShapeDtypeStruct((m, D_FF), BF16),
        compiler_params=_cparams("parallel", "arbitrary"), name="ffn_up",
    )(h, w_up, w_up, conv_w, conv_w, conv_b3, conv_b3)


def _rope_tables(seq):
    t = jnp.arange(seq)
    half = HEAD_DIM // 2
    inv = ROPE_THETA ** (-jnp.arange(0, half, 2, dtype=F32) / half)
    ang_r = (t // GRID_W).astype(F32)[:, None] * inv[None, :]
    ang_c = (t % GRID_W).astype(F32)[:, None] * inv[None, :]
    cos = jnp.concatenate([jnp.cos(ang_r), jnp.cos(ang_r), jnp.cos(ang_c), jnp.cos(ang_c)], axis=1)
    sin = jnp.concatenate([-jnp.sin(ang_r), jnp.sin(ang_r), -jnp.sin(ang_c), jnp.sin(ang_c)], axis=1)
    return cos, sin


def kernel(x_prompt, x_sample, cache_gk, cache_gv, cache_wk, cache_wv, state_lru, c, c_ctx, w_ada, b_ada, w_in, b_in, pool_w, pool_scale, qn_g, kn_g, sink, lru_conv_w, lru_conv_b, lru_lambda, lru_wa, lru_ba, lru_wx, lru_bx, w_br, w_o, ln1_g, ln1_b, w_up, ffn_conv_w, ffn_conv_b, w_down, ln2_g, ln2_b):
    nb_c, seq_c, _ = x_prompt.shape
    nb_s, seq_s, _ = x_sample.shape
    past = cache_gk.shape[2]

    c_all = jnp.concatenate([c_ctx[None, :], c, jnp.zeros((ADA_ROWS - 1 - nb_s, D_MODEL), F32)], axis=0)
    ada = _ada(c_all, w_ada, b_ada)
    ada3 = [ada[l].reshape(ADA_ROWS * 6, 1, D_MODEL) for l in range(DEPTH)]
    cos, sin = _rope_tables(seq_s)
    caches = [a.reshape(nb_s, DEPTH, past, KV_W) for a in (cache_gk, cache_gv, cache_wk, cache_wv)]
    zero_state = jnp.zeros((nb_c, 1, 2, LRU_W), F32)

    groups = [dict(x=x_prompt.reshape(nb_c * seq_c, D_MODEL), nb=nb_c, seq=seq_c, cond=None),
              dict(x=x_sample.reshape(nb_s * seq_s, D_MODEL), nb=nb_s, seq=seq_s, cond=seq_s)]
    for g in groups:
        g["h"] = _lnmod(g["x"], ada3[0], g["cond"])
    new_cache = [[] for _ in range(5)]

    for l in range(DEPTH):
        w_br_l = _cast_layer(w_br.reshape(DEPTH, N_BRANCH * BRANCH_W, D_MODEL), l, 512).reshape(
            N_BRANCH, BRANCH_W, D_MODEL)
        w_down_l = _cast_layer(w_down, l, D_FF // 16)
        pool_w_l = pool_w[l].astype(BF16)
        wa_l = lru_wa[l].astype(BF16)
        wx_l = lru_wx[l].astype(BF16)
        for gi, g in enumerate(groups):
            nb, seq = g["nb"], g["seq"]
            z = _matmul(g["h"], w_in, b_in[l], 2048, 512, "in_proj", layer=l, single_buffer_x=True)
            o_pool = _pool(z, pool_w_l, pool_scale[l], nb, seq)
            if gi == 0:
                o_g, o_w, gk, gv, wk, wv = _ctx_attn(z, sink[l], qn_g[l], kn_g[l], nb, seq)
                h0, h0_layer = zero_state, 0
            else:
                o_g = _lat_global(z, caches[0], caches[1], l, cos, sin, qn_g[l], kn_g[l], nb, seq)
                o_w = _lat_window(z, caches[2], caches[3], l, cos, sin, sink[l], nb, seq)
                h0, h0_layer = state_lru, l
            o_lru, st = _lru(z, h0, h0_layer, lru_conv_w[l], lru_conv_b[l], lru_lambda[l],
                             wa_l, lru_ba[l], wx_l, lru_bx[l], nb, seq)
            if gi == 0:
                for lst, a in zip(new_cache, (gk, gv, wk, wv, st)):
                    lst.append(a)
            merged = _merge((o_pool, o_g, o_w, o_lru), w_br_l, z)
            mix = _matmul(merged, w_o, None, 2048, 512, "out_proj", layer=l, single_buffer_x=True)
            latent = gi == 1
            x1, h2 = _post(g["x"], mix, ada3[l], 2, ln1_g[l], ln1_b[l], nb, seq, latent,
                           nxt=(ada3[l], 3, 4), h_interleaved=True)
            act = _up(h2, w_up, ffn_conv_w, ffn_conv_b, l, seq)
            ff = _matmul(act, w_down_l, None, 512, 512, "ffn_down")
            nxt = (ada3[l + 1], 0, 1) if l + 1 < DEPTH else None
            g["x"], g["h"] = _post(x1, ff, ada3[l], 5, ln2_g[l], ln2_b[l], nb, seq, latent,
                                   nxt=nxt, y_interleaved=True)

    y_prompt = groups[0]["x"].reshape(nb_c, seq_c, D_MODEL)
    y_sample = groups[1]["x"].reshape(nb_s, seq_s, D_MODEL)
    kv = [jnp.stack([a.reshape(nb_c, seq_c, N_KV, HEAD_DIM) for a in new_cache[i]], axis=1) for i in range(4)]
    new_lru = jnp.stack(new_cache[4], axis=1)
    return (y_prompt, y_sample, kv[0], kv[1], kv[2], kv[3], new_lru)
```

```python
import functools

import jax
import jax.numpy as jnp
from jax import lax
from jax.experimental import pallas as pl
from jax.experimental.pallas import tpu as pltpu

F32 = jnp.float32
BF16 = jnp.bfloat16

D_MODEL = 4096
DEPTH = 2
GRID_W = 64
BRANCH_W = D_MODEL // 4
N_BRANCH = 4
HEAD_DIM = 128
N_HEADS = BRANCH_W // HEAD_DIM
N_KV = N_HEADS // 4
N_GROUP = N_HEADS // N_KV
Q_W = N_HEADS * HEAD_DIM
KV_W = N_KV * HEAD_DIM
WINDOW = 128
ROPE_THETA = 10000.0
ATTN_SCALE = HEAD_DIM ** -0.5
NEG_INF = -1e30
POOL_WINDOWS = (2, 4, 8, 16)
POOL_GW = BRANCH_W // len(POOL_WINDOWS)
LRU_W = BRANCH_W
LRU_BW = 128
LRU_C = 8.0
D_FF = 11008
ALPHA = (2 * DEPTH) ** 0.25
IN_SIZES = (BRANCH_W, Q_W, KV_W, KV_W, Q_W, KV_W, KV_W, LRU_W, LRU_W, N_BRANCH * D_MODEL)
IN_W = sum(IN_SIZES)

OFF_POOL = 0
OFF_QG = OFF_POOL + BRANCH_W
OFF_KG = OFF_QG + Q_W
OFF_VG = OFF_KG + KV_W
OFF_QW = OFF_VG + KV_W
OFF_KW = OFF_QW + Q_W
OFF_VW = OFF_KW + KV_W
OFF_XL = OFF_VW + KV_W
OFF_YL = OFF_XL + LRU_W
OFF_GATE = OFF_YL + LRU_W

LANE = 128
SUBLANE = 8
ADA_ROWS = 16
GQ_W = N_GROUP * HEAD_DIM
VMEM_LIMIT = 56 << 20


def _cparams(*sem):
    return pltpu.CompilerParams(dimension_semantics=sem, vmem_limit_bytes=VMEM_LIMIT)


def _dot(a, b):
    return jnp.dot(a, b, preferred_element_type=F32)


def _dot_nt(a, b):
    return lax.dot_general(a, b, (((1,), (1,)), ((), ())), preferred_element_type=F32)


def _ln(x, eps=1e-6):
    mu = jnp.mean(x, axis=-1, keepdims=True)
    xc = x - mu
    var = jnp.mean(xc * xc, axis=-1, keepdims=True)
    return xc * lax.rsqrt(var + eps)


def _rms(x, g, eps=1e-6):
    return x * lax.rsqrt(jnp.mean(x * x, axis=-1, keepdims=True) + eps) * g


def _ada_kernel(c_ref, w_ref, b_ref, o_ref):
    c = c_ref[...]
    s = (c * jax.nn.sigmoid(c)).astype(BF16)
    o_ref[...] = _dot(s, w_ref[...].astype(BF16)) + b_ref[...]


def _ada(c_all, w_ada, b_ada, tn=512):
    n = w_ada.shape[-1]
    return pl.pallas_call(
        _ada_kernel,
        grid=(DEPTH, n // tn),
        in_specs=[pl.BlockSpec((ADA_ROWS, D_MODEL), lambda l, j: (0, 0)),
                  pl.BlockSpec((None, D_MODEL, tn), lambda l, j: (l, 0, j)),
                  pl.BlockSpec((None, 1, tn), lambda l, j: (l, 0, j))],
        out_specs=pl.BlockSpec((None, ADA_ROWS, tn), lambda l, j: (l, 0, j)),
        out_shape=jax.ShapeDtypeStruct((DEPTH, ADA_ROWS, n), F32),
        compiler_params=_cparams("parallel", "parallel"),
        name="ada",
    )(c_all, w_ada, b_ada.reshape(DEPTH, 1, n))


def _ada_spec(which, tm, rows_per_cond):
    if rows_per_cond is None:
        return pl.BlockSpec((None, 1, D_MODEL), lambda i: (which, 0, 0))
    return pl.BlockSpec((None, 1, D_MODEL), lambda i: ((1 + (i * tm) // rows_per_cond) * 6 + which, 0, 0))


def _lnmod_kernel(x_ref, sh_ref, sc_ref, h_ref):
    h_ref[...] = (_ln(x_ref[...]) * (1.0 + sc_ref[...]) + sh_ref[...]).astype(BF16)


def _lnmod(x, ada3, rows_per_cond, tm=256):
    m = x.shape[0]
    return pl.pallas_call(
        _lnmod_kernel,
        grid=(m // tm,),
        in_specs=[pl.BlockSpec((tm, D_MODEL), lambda i: (i, 0)),
                  _ada_spec(0, tm, rows_per_cond), _ada_spec(1, tm, rows_per_cond)],
        out_specs=pl.BlockSpec((tm, D_MODEL), lambda i: (i, 0)),
        out_shape=jax.ShapeDtypeStruct((m, D_MODEL), BF16),
        compiler_params=_cparams("parallel"),
        name="lnmod",
    )(x, ada3, ada3)


def _post_kernel(x_ref, y_ref, g_ref, lg_ref, lb_ref, sh_ref, sc_ref, xo_ref, h_ref, *, y_interleaved, h_interleaved):
    y = y_ref[...]
    if y_interleaved:
        y = jnp.swapaxes(y, 0, 1)
    xn = _ln(ALPHA * x_ref[...] + g_ref[...] * y) * lg_ref[...] + lb_ref[...]
    xo_ref[...] = xn
    if h_ref is not None:
        h = _ln(xn) * (1.0 + sc_ref[...]) + sh_ref[...]
        if h_interleaved:
            h = jnp.swapaxes(h, 0, 1)
        h_ref[...] = h.astype(BF16)


def _post_last_kernel(x_ref, y_ref, g_ref, lg_ref, lb_ref, xo_ref, *, y_interleaved):
    _post_kernel(x_ref, y_ref, g_ref, lg_ref, lb_ref, None, None, xo_ref, None,
                 y_interleaved=y_interleaved, h_interleaved=False)


POST_ROWS = 32


def _post(x, y, ada3, gate_idx, ln_g, ln_b, nb, seq, latent, nxt=None, y_interleaved=False, h_interleaved=False):
    nt = seq // SUBLANE
    nat_shape, int_shape = (nb, SUBLANE, nt, D_MODEL), (nb, nt, SUBLANE, D_MODEL)
    nat = pl.BlockSpec((None, SUBLANE, POST_ROWS, D_MODEL), lambda q, j: (q, 0, j, 0))
    itl = pl.BlockSpec((None, POST_ROWS, SUBLANE, D_MODEL), lambda q, j: (q, j, 0, 0))
    vec = pl.BlockSpec((1, D_MODEL), lambda q, j: (0, 0))
    ada = lambda which: pl.BlockSpec((None, 1, D_MODEL),
                                     lambda q, j: (((1 + q) * 6 if latent else 0) + which, 0, 0))
    in_specs = [nat, itl if y_interleaved else nat, ada(gate_idx), vec, vec]
    args = [x.reshape(nat_shape), y.reshape(int_shape if y_interleaved else nat_shape), ada3,
            ln_g.reshape(1, D_MODEL), ln_b.reshape(1, D_MODEL)]
    grid = (nb, nt // POST_ROWS)
    if nxt is None:
        xo = pl.pallas_call(
            functools.partial(_post_last_kernel, y_interleaved=y_interleaved),
            grid=grid, in_specs=in_specs, out_specs=nat,
            out_shape=jax.ShapeDtypeStruct(nat_shape, F32),
            compiler_params=_cparams("parallel", "parallel"), name="post_last",
        )(*args)
        return xo.reshape(nb * seq, D_MODEL), None
    nxt_ada3, sh_idx, sc_idx = nxt
    in_specs += [ada(sh_idx), ada(sc_idx)]
    args += [nxt_ada3, nxt_ada3]
    xo, h = pl.pallas_call(
        functools.partial(_post_kernel, y_interleaved=y_interleaved, h_interleaved=h_interleaved),
        grid=grid, in_specs=in_specs, out_specs=(nat, itl if h_interleaved else nat),
        out_shape=(jax.ShapeDtypeStruct(nat_shape, F32),
                   jax.ShapeDtypeStruct(int_shape if h_interleaved else nat_shape, BF16)),
        compiler_params=_cparams("parallel", "parallel"), name="post",
    )(*args)
    return xo.reshape(nb * seq, D_MODEL), h.reshape(nb * seq, D_MODEL)


def _mm_bias_kernel(x_ref, w_ref, b_ref, o_ref):
    o_ref[...] = (_dot(x_ref[...], w_ref[...].astype(BF16)) + b_ref[...]).astype(o_ref.dtype)


def _mm_kernel(x_ref, w_ref, o_ref):
    o_ref[...] = _dot(x_ref[...], w_ref[...].astype(BF16)).astype(o_ref.dtype)


def _matmul(x, w, b, tm, tn, name, layer=None, single_buffer_x=False):
    m, k = x.shape
    n = w.shape[-1]
    tm = min(tm, m)
    x_mode = dict(pipeline_mode=pl.Buffered(1)) if single_buffer_x else {}
    if layer is None:
        w_spec = pl.BlockSpec((k, tn), lambda i, j: (0, j))
    else:
        w_spec = pl.BlockSpec((None, k, tn), lambda i, j: (layer, 0, j))
    in_specs = [pl.BlockSpec((tm, k), lambda i, j: (i, 0), **x_mode), w_spec]
    args = [x, w]
    body = _mm_kernel
    if b is not None:
        in_specs.append(pl.BlockSpec((1, tn), lambda i, j: (0, j)))
        args.append(b.reshape(1, n))
        body = _mm_bias_kernel
    return pl.pallas_call(
        body, grid=(m // tm, n // tn), in_specs=in_specs,
        out_specs=pl.BlockSpec((tm, tn), lambda i, j: (i, j)),
        out_shape=jax.ShapeDtypeStruct((m, n), F32),
        compiler_params=_cparams("parallel", "arbitrary"), name=name,
    )(*args)


def _cast_kernel(w_ref, o_ref):
    o_ref[...] = w_ref[...].astype(BF16)


def _cast_layer(w, layer, rows_per_step):
    _, r, c = w.shape
    return pl.pallas_call(
        _cast_kernel,
        grid=(r // rows_per_step,),
        in_specs=[pl.BlockSpec((None, rows_per_step, c), lambda i: (layer, i, 0))],
        out_specs=pl.BlockSpec((rows_per_step, c), lambda i: (i, 0)),
        out_shape=jax.ShapeDtypeStruct((r, c), BF16),
        compiler_params=_cparams("parallel"), name="cast_w",
    )(w)


POOL_ROW_BLOCK = 256
POOL_HALO = LANE


def _pool_kernel(u_ref, w_ref, sc_ref, o_ref):
    seq = u_ref.shape[0]
    rb = min(seq, POOL_ROW_BLOCK)
    for r0 in range(0, seq, rb):
        k0, k1 = max(0, r0 - POOL_HALO), min(seq, r0 + rb + POOL_HALO)
        t = r0 + lax.broadcasted_iota(jnp.int32, (rb, k1 - k0), 0)
        s = k0 + lax.broadcasted_iota(jnp.int32, (rb, k1 - k0), 1)
        d = s - t
        tcol = r0 + lax.broadcasted_iota(jnp.int32, (rb, 1), 0)
        for gi, win in enumerate(POOL_WINDOWS):
            half = win // 2
            cols = slice(gi * POOL_GW, (gi + 1) * POOL_GW)
            band = jnp.where((d >= -half) & (d < half), 1.0, 0.0).astype(BF16)
            cnt = jnp.minimum(tcol + half, seq) - jnp.maximum(tcol - half, 0)
            keys = u_ref[k0:k1, cols]
            k_hi = keys.astype(BF16)
            k_lo = (keys - k_hi.astype(F32)).astype(BF16)
            mean = (_dot(band, k_hi) + _dot(band, k_lo)) / cnt.astype(F32)
            y = _dot((mean - u_ref[r0:r0 + rb, cols]).astype(BF16), w_ref[gi])
            o_ref[r0:r0 + rb, cols] = (y * sc_ref[:, cols]).astype(BF16)


def _pool(z, pool_w, pool_scale, nb, seq):
    return pl.pallas_call(
        _pool_kernel,
        grid=(nb,),
        in_specs=[pl.BlockSpec((seq, BRANCH_W), lambda b: (b, OFF_POOL // BRANCH_W)),
                  pl.BlockSpec((len(POOL_WINDOWS), POOL_GW, POOL_GW), lambda b: (0, 0, 0)),
                  pl.BlockSpec((1, BRANCH_W), lambda b: (0, 0))],
        out_specs=pl.BlockSpec((seq, BRANCH_W), lambda b: (b, 0)),
        out_shape=jax.ShapeDtypeStruct((nb * seq, BRANCH_W), BF16),
        compiler_params=_cparams("parallel"), name="pool",
    )(z, pool_w, pool_scale.reshape(1, BRANCH_W))


def _softmax_attend(q, k, v, sink):
    s = _dot_nt(q, k) * ATTN_SCALE
    m = jnp.max(s, axis=-1, keepdims=True)
    if sink is not None:
        m = jnp.maximum(m, sink)
    e = jnp.exp(s - m)
    den = jnp.sum(e, axis=-1, keepdims=True)
    if sink is not None:
        den = den + jnp.exp(sink - m)
    return _dot(e.astype(BF16), v) * (1.0 / den)


def _ctx_attn_kernel(sink_ref, qg_ref, kg_ref, vg_ref, qw_ref, kw_ref, vw_ref, qn_ref, kn_ref,
                     og_ref, ow_ref, gk_ref, gv_ref, wk_ref, wv_ref):
    kvh = pl.program_id(1)
    kg = _rms(kg_ref[...], kn_ref[...])
    gk_ref[...] = kg
    gv_ref[...] = vg_ref[...]
    wk_ref[...] = kw_ref[...]
    wv_ref[...] = vw_ref[...]
    kg_b, vg_b = kg.astype(BF16), vg_ref[...].astype(BF16)
    kw_b, vw_b = kw_ref[...].astype(BF16), vw_ref[...].astype(BF16)
    for h in range(N_GROUP):
        cols = slice(h * HEAD_DIM, (h + 1) * HEAD_DIM)
        qg = _rms(qg_ref[:, cols], qn_ref[...]).astype(BF16)
        og_ref[:, cols] = _softmax_attend(qg, kg_b, vg_b, None).astype(BF16)
        sink = sink_ref[kvh * N_GROUP + h]
        ow_ref[:, cols] = _softmax_attend(qw_ref[:, cols].astype(BF16), kw_b, vw_b, sink).astype(BF16)


def _ctx_attn(z, sink, qn_g, kn_g, nb, seq):
    q_spec = lambda off: pl.BlockSpec((seq, GQ_W), lambda b, k: (b, off // GQ_W + k))
    kv_spec = lambda off: pl.BlockSpec((seq, HEAD_DIM), lambda b, k: (b, off // HEAD_DIM + k))
    vec = pl.BlockSpec((1, HEAD_DIM), lambda b, k: (0, 0))
    o_spec = pl.BlockSpec((seq, GQ_W), lambda b, k: (b, k))
    c_spec = pl.BlockSpec((seq, HEAD_DIM), lambda b, k: (b, k))
    m = nb * seq
    return pl.pallas_call(
        _ctx_attn_kernel,
        grid=(nb, N_KV),
        in_specs=[pl.BlockSpec(memory_space=pltpu.SMEM),
                  q_spec(OFF_QG), kv_spec(OFF_KG), kv_spec(OFF_VG),
                  q_spec(OFF_QW), kv_spec(OFF_KW), kv_spec(OFF_VW), vec, vec],
        out_specs=(o_spec, o_spec, c_spec, c_spec, c_spec, c_spec),
        out_shape=(jax.ShapeDtypeStruct((m, Q_W), BF16), jax.ShapeDtypeStruct((m, Q_W), BF16))
        + tuple(jax.ShapeDtypeStruct((m, KV_W), F32) for _ in range(4)),
        compiler_params=_cparams("parallel", "parallel"), name="ctx_attn",
    )(sink, z, z, z, z, z, z, qn_g.reshape(1, HEAD_DIM), kn_g.reshape(1, HEAD_DIM))


def _rope(x, cos, sin, low_half):
    partner = jnp.where(low_half, pltpu.roll(x, HEAD_DIM - 32, axis=1), pltpu.roll(x, 32, axis=1))
    return x * cos + partner * sin


def _low_half(rows):
    lane = lax.broadcasted_iota(jnp.int32, (rows, HEAD_DIM), 1)
    return (lane & 63) < 32


def _lat_global_kernel(q_ref, k_ref, v_ref, ck_ref, cv_ref, cos_ref, sin_ref, qn_ref, kn_ref,
                       o_ref, kall, vall, *, tq):
    seq = q_ref.shape[0]
    past = ck_ref.shape[0]
    kall[0:past, :] = ck_ref[...].astype(BF16)
    vall[0:past, :] = cv_ref[...].astype(BF16)
    kn = _rope(_rms(k_ref[...], kn_ref[...]), cos_ref[...], sin_ref[...], _low_half(seq))
    kall[past:past + seq, :] = kn.astype(BF16)
    vall[past:past + seq, :] = v_ref[...].astype(BF16)
    low = _low_half(tq)

    def q_block(qb, carry):
        rows = pl.ds(pl.multiple_of(qb * tq, tq), tq)
        cos, sin = cos_ref[rows, :], sin_ref[rows, :]
        for h in range(N_GROUP):
            cols = slice(h * HEAD_DIM, (h + 1) * HEAD_DIM)
            q = _rope(_rms(q_ref[rows, cols], qn_ref[...]), cos, sin, low).astype(BF16)
            o_ref[rows, cols] = _softmax_attend(q, kall[...], vall[...], None).astype(BF16)
        return carry

    lax.fori_loop(0, seq // tq, q_block, 0)


def _lat_global(z, cache_k, cache_v, layer, cos, sin, qn_g, kn_g, nb, seq, tq=256):
    past = cache_k.shape[2]
    q_spec = pl.BlockSpec((seq, GQ_W), lambda b, k: (b, OFF_QG // GQ_W + k))
    kv_spec = lambda off: pl.BlockSpec((seq, HEAD_DIM), lambda b, k: (b, off // HEAD_DIM + k))
    c_spec = pl.BlockSpec((None, None, past, HEAD_DIM), lambda b, k: (b, layer, 0, k))
    tab = pl.BlockSpec((seq, HEAD_DIM), lambda b, k: (0, 0))
    vec = pl.BlockSpec((1, HEAD_DIM), lambda b, k: (0, 0))
    return pl.pallas_call(
        functools.partial(_lat_global_kernel, tq=tq),
        grid=(nb, N_KV),
        in_specs=[q_spec, kv_spec(OFF_KG), kv_spec(OFF_VG), c_spec, c_spec, tab, tab, vec, vec],
        out_specs=pl.BlockSpec((seq, GQ_W), lambda b, k: (b, k)),
        out_shape=jax.ShapeDtypeStruct((nb * seq, Q_W), BF16),
        scratch_shapes=[pltpu.VMEM((past + seq, HEAD_DIM), BF16), pltpu.VMEM((past + seq, HEAD_DIM), BF16)],
        compiler_params=_cparams("parallel", "parallel"), name="lat_global",
    )(z, z, z, cache_k, cache_v, cos, sin, qn_g.reshape(1, HEAD_DIM), kn_g.reshape(1, HEAD_DIM))


def _lat_window_kernel(sink_ref, q_ref, k_ref, v_ref, ck_ref, cv_ref, cos_ref, sin_ref,
                       o_ref, kpad, vpad):
    seq = q_ref.shape[0]
    blk = WINDOW
    kvh = pl.program_id(1)
    zeros = jnp.zeros((blk, HEAD_DIM), BF16)
    kpad[0:blk, :] = zeros
    vpad[0:blk, :] = zeros
    kpad[blk + seq:2 * blk + seq, :] = zeros
    vpad[blk + seq:2 * blk + seq, :] = zeros
    kpad[blk:blk + seq, :] = _rope(k_ref[...], cos_ref[...], sin_ref[...], _low_half(seq)).astype(BF16)
    vpad[blk:blk + seq, :] = v_ref[...].astype(BF16)
    ck = ck_ref[...].astype(BF16)
    cv = cv_ref[...].astype(BF16)
    low = _low_half(blk)
    rows4 = N_GROUP * blk
    head = lax.broadcasted_iota(jnp.int32, (rows4, 1), 0) // blk
    sink = jnp.zeros((rows4, 1), F32)
    for h in range(N_GROUP):
        sink = jnp.where(head == h, sink_ref[kvh * N_GROUP + h], sink)
    qi = lax.broadcasted_iota(jnp.int32, (rows4, 3 * blk), 0) & (blk - 1)
    kj = lax.broadcasted_iota(jnp.int32, (rows4, 3 * blk), 1)
    band_bias = jnp.where(jnp.abs(qi + blk - kj) <= WINDOW, 0.0, NEG_INF)

    def q_block(n, carry):
        r0 = pl.multiple_of(n * blk, blk)
        rows = pl.ds(r0, blk)
        cos, sin = cos_ref[rows, :], sin_ref[rows, :]
        q4 = jnp.concatenate(
            [_rope(q_ref[rows, h * HEAD_DIM:(h + 1) * HEAD_DIM], cos, sin, low) for h in range(N_GROUP)],
            axis=0).astype(BF16)
        in_seq = ((n - 1) * blk + kj).astype(jnp.uint32) < seq
        s_loc = jnp.where(in_seq, _dot_nt(q4, kpad[pl.ds(r0, 3 * blk), :]) * ATTN_SCALE + band_bias, NEG_INF)
        s_ctx = _dot_nt(q4, ck) * ATTN_SCALE
        m = jnp.maximum(jnp.maximum(jnp.max(s_loc, axis=-1, keepdims=True),
                                    jnp.max(s_ctx, axis=-1, keepdims=True)), sink)
        e_loc = jnp.exp(s_loc - m)
        e_ctx = jnp.exp(s_ctx - m)
        den = (jnp.sum(e_loc, axis=-1, keepdims=True) + jnp.sum(e_ctx, axis=-1, keepdims=True)
               + jnp.exp(sink - m))
        o4 = (_dot(e_loc.astype(BF16), vpad[pl.ds(r0, 3 * blk), :]) + _dot(e_ctx.astype(BF16), cv)) * (1.0 / den)
        for h in range(N_GROUP):
            o_ref[rows, h * HEAD_DIM:(h + 1) * HEAD_DIM] = o4[h * blk:(h + 1) * blk].astype(BF16)
        return carry

    lax.fori_loop(0, seq // blk, q_block, 0)


def _lat_window(z, cache_k, cache_v, layer, cos, sin, sink, nb, seq):
    past = cache_k.shape[2]
    q_spec = pl.BlockSpec((seq, GQ_W), lambda b, k: (b, OFF_QW // GQ_W + k))
    kv_spec = lambda off: pl.BlockSpec((seq, HEAD_DIM), lambda b, k: (b, off // HEAD_DIM + k))
    c_spec = pl.BlockSpec((None, None, past, HEAD_DIM), lambda b, k: (b, layer, 0, k))
    tab = pl.BlockSpec((seq, HEAD_DIM), lambda b, k: (0, 0))
    return pl.pallas_call(
        _lat_window_kernel,
        grid=(nb, N_KV),
        in_specs=[pl.BlockSpec(memory_space=pltpu.SMEM), q_spec, kv_spec(OFF_KW), kv_spec(OFF_VW),
                  c_spec, c_spec, tab, tab],
        out_specs=pl.BlockSpec((seq, GQ_W), lambda b, k: (b, k)),
        out_shape=jax.ShapeDtypeStruct((nb * seq, Q_W), BF16),
        scratch_shapes=[pltpu.VMEM((seq + 2 * WINDOW, HEAD_DIM), BF16),
                        pltpu.VMEM((seq + 2 * WINDOW, HEAD_DIM), BF16)],
        compiler_params=_cparams("parallel", "parallel"), name="lat_window",
    )(sink, z, z, z, cache_k, cache_v, cos, sin)


def _interleave(v, n_tiles):
    return jnp.swapaxes(v.reshape(SUBLANE, n_tiles, v.shape[-1]), 0, 1).reshape(v.shape)


def _deinterleave(v, n_tiles):
    return jnp.swapaxes(v.reshape(n_tiles, SUBLANE, v.shape[-1]), 0, 1).reshape(v.shape)


def _lru_kernel(x_ref, y_ref, cw_ref, cb_ref, lam_ref, wa_ref, ba_ref, wx_ref, bx_ref, h0_ref,
                o_ref, st_ref, a_f, b_f, a_b, b_b):
    seq, width = x_ref.shape
    n_tiles = seq // SUBLANE
    sub = lax.broadcasted_iota(jnp.int32, (SUBLANE, width), 0)

    def from_prev_segment(tile):
        return jnp.where(sub >= 1, pltpu.roll(tile, 1, axis=0), 0.0)

    def from_next_segment(tile):
        return jnp.where(sub < SUBLANE - 1, pltpu.roll(tile, SUBLANE - 1, axis=0), 0.0)

    x = _interleave(x_ref[...], n_tiles)
    last, last2, first = x[seq - SUBLANE:], x[seq - 2 * SUBLANE:seq - SUBLANE], x[:SUBLANE]
    xm1 = jnp.concatenate([from_prev_segment(last), x[:seq - SUBLANE]], axis=0)
    xm2 = jnp.concatenate([from_prev_segment(last2), from_prev_segment(last), x[:seq - 2 * SUBLANE]], axis=0)
    xp1 = jnp.concatenate([x[SUBLANE:], from_next_segment(first)], axis=0)
    xc = (cb_ref[...] + xm2 * cw_ref[0:1, :] + xm1 * cw_ref[1:2, :] + x * cw_ref[2:3, :] + xp1 * cw_ref[3:4, :])
    xcb = xc.astype(BF16)

    def block_diag(w_ref, d):
        return jnp.concatenate(
            [_dot(xcb[:, j * LRU_BW:(j + 1) * LRU_BW], w_ref[d, j]) for j in range(width // LRU_BW)], axis=1)

    for d, (a_s, b_s) in enumerate(((a_f, b_f), (a_b, b_b))):
        r = jax.nn.sigmoid(block_diag(wa_ref, d) + ba_ref[d:d + 1, :])
        gate_i = jax.nn.sigmoid(block_diag(wx_ref, d) + bx_ref[d:d + 1, :])
        neg_lam = -lam_ref[d:d + 1, :]
        softplus = jnp.maximum(neg_lam, 0.0) + jnp.log1p(jnp.exp(-jnp.abs(neg_lam)))
        log_a = -LRU_C * r * softplus
        a_s[...] = jnp.exp(log_a)
        th = jnp.tanh(log_a)
        b_s[...] = jnp.sqrt(-2.0 * th / (1.0 - th)) * (gate_i * xc)

    def local_step(i, carry):
        hf, pf, hb, pb = carry
        rf = pl.ds(pl.multiple_of(i * SUBLANE, SUBLANE), SUBLANE)
        rb = pl.ds(pl.multiple_of((n_tiles - 1 - i) * SUBLANE, SUBLANE), SUBLANE)
        af, ab = a_f[rf, :], a_b[rb, :]
        hf, pf = af * hf + b_f[rf, :], af * pf
        hb, pb = ab * hb + b_b[rb, :], ab * pb
        a_f[rf, :], b_f[rf, :] = pf, hf
        a_b[rb, :], b_b[rb, :] = pb, hb
        return hf, pf, hb, pb

    zeros, ones = jnp.zeros((SUBLANE, width), F32), jnp.ones((SUBLANE, width), F32)
    hf, pf, hb, pb = lax.fori_loop(0, n_tiles, local_step, (zeros, ones, zeros, ones), unroll=8)

    for s in (1, 2, 4):
        take_f, take_b = sub >= s, sub < SUBLANE - s
        hf = jnp.where(take_f, pf * pltpu.roll(hf, s, axis=0) + hf, hf)
        pf = jnp.where(take_f, pf * pltpu.roll(pf, s, axis=0), pf)
        hb = jnp.where(take_b, pb * pltpu.roll(hb, SUBLANE - s, axis=0) + hb, hb)
        pb = jnp.where(take_b, pb * pltpu.roll(pb, SUBLANE - s, axis=0), pb)
    h0f = jnp.broadcast_to(h0_ref[0:1, :], (SUBLANE, width))
    h0b = jnp.broadcast_to(h0_ref[1:2, :], (SUBLANE, width))
    out_f = pf * h0f + hf
    out_b = pb * h0b + hb
    st_ref[0:1, :] = out_f[SUBLANE - 1:SUBLANE, :]
    st_ref[1:2, :] = out_b[0:1, :]
    in_f = jnp.where(sub >= 1, pltpu.roll(out_f, 1, axis=0), h0f)
    in_b = jnp.where(sub < SUBLANE - 1, pltpu.roll(out_b, SUBLANE - 1, axis=0), h0b)

    def tiles(ref):
        return ref[...].reshape(n_tiles, SUBLANE, width)

    h = (tiles(b_f) + tiles(a_f) * in_f[None]) + (tiles(b_b) + tiles(a_b) * in_b[None])
    h = _deinterleave(h.reshape(seq, width), n_tiles)
    o_ref[...] = (h * jax.nn.gelu(y_ref[...])).astype(BF16)


def _lru(z, h0, h0_layer, conv_w, conv_b, lam, wa, ba, wx, bx, nb, seq, cw=1024):
    nc = LRU_W // cw
    nd = cw // LRU_BW
    vec2 = pl.BlockSpec((2, cw), lambda b, c: (0, c))
    wspec = pl.BlockSpec((2, nd, LRU_BW, LRU_BW), lambda b, c: (0, c, 0, 0))
    return pl.pallas_call(
        _lru_kernel,
        grid=(nb, nc),
        in_specs=[pl.BlockSpec((seq, cw), lambda b, c: (b, OFF_XL // cw + c)),
                  pl.BlockSpec((seq, cw), lambda b, c: (b, OFF_YL // cw + c)),
                  pl.BlockSpec((4, cw), lambda b, c: (0, c)),
                  pl.BlockSpec((1, cw), lambda b, c: (0, c)),
                  vec2, wspec, vec2, wspec, vec2,
                  pl.BlockSpec((None, None, 2, cw), lambda b, c: (b, h0_layer, 0, c))],
        out_specs=(pl.BlockSpec((seq, cw), lambda b, c: (b, c)),
                   pl.BlockSpec((None, 2, cw), lambda b, c: (b, 0, c))),
        out_shape=(jax.ShapeDtypeStruct((nb * seq, LRU_W), BF16), jax.ShapeDtypeStruct((nb, 2, LRU_W), F32)),
        scratch_shapes=[pltpu.VMEM((seq, cw), F32)] * 4,
        compiler_params=_cparams("parallel", "parallel"), name="lru",
    )(z, z, conv_w, conv_b.reshape(1, LRU_W), lam, wa, ba, wx, bx, h0)


MERGE_ROW_CHUNK = 128


def _merge_kernel(b0, b1, b2, b3, w_ref, g0, g1, g2, g3, o_ref):
    for r in range(0, o_ref.shape[0], MERGE_ROW_CHUNK):
        rows = slice(r, r + MERGE_ROW_CHUNK)
        acc = None
        for n, (br, gr) in enumerate(((b0, g0), (b1, g1), (b2, g2), (b3, g3))):
            t = jax.nn.sigmoid(gr[rows, :]) * _dot(br[rows, :], w_ref[n])
            acc = t if acc is None else acc + t
        o_ref[rows, :] = acc.astype(BF16)


def _merge(branches, w_br, z, tm=2048, tn=256):
    m = z.shape[0]
    tm = min(tm, m)
    br = pl.BlockSpec((tm, BRANCH_W), lambda i, j: (i, 0), pipeline_mode=pl.Buffered(1))
    gate = lambda n: pl.BlockSpec((tm, tn), lambda i, j: (i, (OFF_GATE + n * D_MODEL) // tn + j))
    return pl.pallas_call(
        _merge_kernel,
        grid=(m // tm, D_MODEL // tn),
        in_specs=[br, br, br, br, pl.BlockSpec((N_BRANCH, BRANCH_W, tn), lambda i, j: (0, 0, j)),
                  gate(0), gate(1), gate(2), gate(3)],
        out_specs=pl.BlockSpec((tm, tn), lambda i, j: (i, j)),
        out_shape=jax.ShapeDtypeStruct((m, D_MODEL), BF16),
        compiler_params=_cparams("parallel", "arbitrary"), name="merge",
    )(*branches, w_br, z, z, z, z)


UP_ROW_CHUNK = 128


def _up_kernel(x_ref, wu_ref, wg_ref, cwu_ref, cwg_ref, cbu_ref, cbg_ref, o_ref, *, seq):
    rows, tn = o_ref.shape
    sub = lax.broadcasted_iota(jnp.int32, (SUBLANE, tn), 0)

    def conv(v, w_ref, b_ref):
        prev, nxt = [], []
        for q in range(rows // seq):
            vq = v[q * seq:(q + 1) * seq]
            head, tail = vq[:SUBLANE], vq[seq - SUBLANE:]
            prev += [jnp.where(sub >= 1, pltpu.roll(tail, 1, axis=0), 0.0), vq[:seq - SUBLANE]]
            nxt += [vq[SUBLANE:], jnp.where(sub < SUBLANE - 1, pltpu.roll(head, SUBLANE - 1, axis=0), 0.0)]
        prev, nxt = jnp.concatenate(prev, axis=0), jnp.concatenate(nxt, axis=0)
        return b_ref[...] + prev * w_ref[0:1, :] + v * w_ref[1:2, :] + nxt * w_ref[2:3, :]

    wu, wg = wu_ref[...].astype(BF16), wg_ref[...].astype(BF16)

    def row_chunked_dot(w):
        return jnp.concatenate([_dot(x_ref[r:r + UP_ROW_CHUNK, :], w) for r in range(0, rows, UP_ROW_CHUNK)],
                               axis=0)

    u = conv(row_chunked_dot(wu), cwu_ref, cbu_ref)
    g = conv(row_chunked_dot(wg), cwg_ref, cbg_ref)
    o_ref[...] = (g * jax.nn.sigmoid(g) * u).astype(BF16)


def _up(h, w_up, conv_w, conv_b, layer, seq, tm=2048, tn=256):
    m = h.shape[0]
    tm = min(tm, m)
    nj = D_FF // tn
    w_spec = lambda off: pl.BlockSpec((None, D_MODEL, tn), lambda i, j: (layer, 0, off + j))
    cw_spec = lambda off: pl.BlockSpec((None, 3, tn), lambda i, j: (layer, 0, off + j))
    cb_spec = lambda off: pl.BlockSpec((None, 1, tn), lambda i, j: (layer, 0, off + j))
    conv_b3 = conv_b.reshape(DEPTH, 1, 2 * D_FF)
    return pl.pallas_call(
        functools.partial(_up_kernel, seq=seq),
        grid=(m // tm, nj),
        in_specs=[pl.BlockSpec((tm, D_MODEL), lambda i, j: (i, 0), pipeline_mode=pl.Buffered(1)),
                  w_spec(0), w_spec(nj),
                  cw_spec(0), cw_spec(nj), cb_spec(0), cb_spec(nj)],
        out_specs=pl.BlockSpec((tm, tn), lambda i, j: (i, j)),
        out_shape=jax.ShapeDtypeStruct((m, D_FF), BF16),
        compiler_params=_cparams("parallel", "arbitrary"), name="ffn_up",
    )(h, w_up, w_up, conv_w, conv_w, conv_b3, conv_b3)


def _rope_tables(seq):
    t = jnp.arange(seq)
    half = HEAD_DIM // 2
    inv = ROPE_THETA ** (-jnp.arange(0, half, 2, dtype=F32) / half)
    ang_r = (t // GRID_W).astype(F32)[:, None] * inv[None, :]
    ang_c = (t % GRID_W).astype(F32)[:, None] * inv[None, :]
    cos = jnp.concatenate([jnp.cos(ang_r), jnp.cos(ang_r), jnp.cos(ang_c), jnp.cos(ang_c)], axis=1)
    sin = jnp.concatenate([-jnp.sin(ang_r), jnp.sin(ang_r), -jnp.sin(ang_c), jnp.sin(ang_c)], axis=1)
    return cos, sin


def kernel(x_prompt, x_sample, cache_gk, cache_gv, cache_wk, cache_wv, state_lru, c, c_ctx, w_ada, b_ada, w_in, b_in, pool_w, pool_scale, qn_g, kn_g, sink, lru_conv_w, lru_conv_b, lru_lambda, lru_wa, lru_ba, lru_wx, lru_bx, w_br, w_o, ln1_g, ln1_b, w_up, ffn_conv_w, ffn_conv_b, w_down, ln2_g, ln2_b):
    nb_c, seq_c, _ = x_prompt.shape
    nb_s, seq_s, _ = x_sample.shape
    past = cache_gk.shape[2]

    c_all = jnp.concatenate([c_ctx[None, :], c, jnp.zeros((ADA_ROWS - 1 - nb_s, D_MODEL), F32)], axis=0)
    ada = _ada(c_all, w_ada, b_ada)
    ada3 = [ada[l].reshape(ADA_ROWS * 6, 1, D_MODEL) for l in range(DEPTH)]
    cos, sin = _rope_tables(seq_s)
    caches = [a.reshape(nb_s, DEPTH, past, KV_W) for a in (cache_gk, cache_gv, cache_wk, cache_wv)]
    zero_state = jnp.zeros((nb_c, 1, 2, LRU_W), F32)

    groups = [dict(x=x_prompt.reshape(nb_c * seq_c, D_MODEL), nb=nb_c, seq=seq_c, cond=None),
              dict(x=x_sample.reshape(nb_s * seq_s, D_MODEL), nb=nb_s, seq=seq_s, cond=seq_s)]
    for g in groups:
        g["h"] = _lnmod(g["x"], ada3[0], g["cond"])
    new_cache = [[] for _ in range(5)]

    for l in range(DEPTH):
        w_br_l = _cast_layer(w_br.reshape(DEPTH, N_BRANCH * BRANCH_W, D_MODEL), l, 512).reshape(
            N_BRANCH, BRANCH_W, D_MODEL)
        w_down_l = _cast_layer(w_down, l, D_FF // 16)
        pool_w_l = pool_w[l].astype(BF16)
        wa_l = lru_wa[l].astype(BF16)
        wx_l = lru_wx[l].astype(BF16)
        for gi, g in enumerate(groups):
            nb, seq = g["nb"], g["seq"]
            z = _matmul(g["h"], w_in, b_in[l], 2048, 512, "in_proj", layer=l, single_buffer_x=True)
            o_pool = _pool(z, pool_w_l, pool_scale[l], nb, seq)
            if gi == 0:
                o_g, o_w, gk, gv, wk, wv = _ctx_attn(z, sink[l], qn_g[l], kn_g[l], nb, seq)
                h0, h0_layer = zero_state, 0
            else:
                o_g = _lat_global(z, caches[0], caches[1], l, cos, sin, qn_g[l], kn_g[l], nb, seq)
                o_w = _lat_window(z, caches[2], caches[3], l, cos, sin, sink[l], nb, seq)
                h0, h0_layer = state_lru, l
            o_lru, st = _lru(z, h0, h0_layer, lru_conv_w[l], lru_conv_b[l], lru_lambda[l],
                             wa_l, lru_ba[l], wx_l, lru_bx[l], nb, seq)
            if gi == 0:
                for lst, a in zip(new_cache, (gk, gv, wk, wv, st)):
                    lst.append(a)
            merged = _merge((o_pool, o_g, o_w, o_lru), w_br_l, z)
            mix = _matmul(merged, w_o, None, 2048, 512, "out_proj", layer=l, single_buffer_x=True)
            latent = gi == 1
            x1, h2 = _post(g["x"], mix, ada3[l], 2, ln1_g[l], ln1_b[l], nb, seq, latent,
                           nxt=(ada3[l], 3, 4), h_interleaved=True)
            act = _up(h2, w_up, ffn_conv_w, ffn_conv_b, l, seq)
            ff = _matmul(act, w_down_l, None, 512, 512, "ffn_down")
            nxt = (ada3[l + 1], 0, 1) if l + 1 < DEPTH else None
            g["x"], g["h"] = _post(x1, ff, ada3[l], 5, ln2_g[l], ln2_b[l], nb, seq, latent,
                                   nxt=nxt, y_interleaved=True)

    y_prompt = groups[0]["x"].reshape(nb_c, seq_c, D_MODEL)
    y_sample = groups[1]["x"].reshape(nb_s, seq_s, D_MODEL)
    kv = [jnp.stack([a.reshape(nb_c, seq_c, N_KV, HEAD_DIM) for a in new_cache[i]], axis=1) for i in range(4)]
    new_lru = jnp.stack(new_cache[4], axis=1)
    return (y_prompt, y_sample, kv[0], kv[1], kv[2], kv[3], new_lru)
```
